```python
import jax
import jax.numpy as jnp
from jax import lax
import numpy as np

D_MODEL = 1024
BATCH = 8
SEQ = 8192
DEPTH = 2

D_PLE = 256
GRID_W = 64
D_NA = D_MODEL // 2
NA_HEADS = 8
NA_HEAD_DIM = D_NA // NA_HEADS
NA_WIN_H = 8
NA_WIN_W = 16
NA_QBLOCK_W = 16
NA_KBLOCK_W = 32
NA_BIAS_H = 2 * NA_WIN_H - 1
NA_BIAS_W = 2 * NA_WIN_W - 1
D_ML = D_MODEL - D_NA
ML_HEADS = 4
ML_HEAD_DIM = D_ML // ML_HEADS
ML_CONV = 5
ML_CHUNK = 128
N_GATES = 4 * ML_HEADS
D_IN = 3 * D_NA + 4 * D_ML + N_GATES
D_MIX = D_NA + D_ML
N_GROUPS = 4
EXPERTS_PER_GROUP = 8
N_EXPERTS = N_GROUPS * EXPERTS_PER_GROUP
TOP_K = 2
D_EXPERT = 512
MOE_BLOCK = 128
EPS = 1e-6
NEG_INF = -1e30

kernel_name = 'hybrid_na2d_mlstm_hmoe_encoder'


def rmsnorm(x, g):
    xf = x.astype(jnp.float32)
    y = xf * lax.rsqrt(jnp.mean(xf * xf, axis=-1, keepdims=True) + EPS)
    return (y * g.astype(jnp.float32)).astype(x.dtype)


def head_rmsnorm(y, g, n_heads):
    B, S, D = y.shape
    dh = D // n_heads
    return rmsnorm(y.reshape(B, S, n_heads, dh), g.reshape(n_heads, dh)).reshape(B, S, D)


def _clamped_starts(n, win):
    return np.clip(np.arange(n) - win // 2, 0, n - win)


def neighbourhood_attention(q, k, v, rpb):
    B, S, H, Dh = q.shape
    rows = S // GRID_W
    kh = min(NA_WIN_H, rows)
    n_cb = GRID_W // NA_QBLOCK_W
    r = np.arange(rows)
    key_rows = _clamped_starts(rows, kh)[:, None] + np.arange(kh)[None, :]
    col_start = _clamped_starts(GRID_W, NA_WIN_W)
    q_cols = np.arange(GRID_W).reshape(n_cb, NA_QBLOCK_W)
    kb_start = np.minimum(col_start[q_cols[:, 0]], GRID_W - NA_KBLOCK_W)
    key_cols = kb_start[:, None] + np.arange(NA_KBLOCK_W)[None, :]
    cs = col_start[q_cols][:, :, None]
    kc = key_cols[:, None, :]
    in_window = (kc >= cs) & (kc < cs + NA_WIN_W)
    dr = key_rows - r[:, None] + (NA_WIN_H - 1)
    dc = np.clip(kc - q_cols[:, :, None], 1 - NA_WIN_W, NA_WIN_W - 1) + (NA_WIN_W - 1)
    bias = rpb.astype(jnp.float32)[:, dr[:, None, None, :, None], dc[None, :, :, None, :]]
    bias = jnp.where(in_window[None, None, :, :, None, :], bias, NEG_INF)
    qg = q.reshape(B, rows, n_cb, NA_QBLOCK_W, H, Dh)
    ridx = key_rows[:, None, :, None]
    cidx = key_cols[None, :, None, :]
    kg = k.reshape(B, rows, GRID_W, H, Dh)[:, ridx, cidx]
    vg = v.reshape(B, rows, GRID_W, H, Dh)[:, ridx, cidx]
    s = jnp.einsum('brjqhd,brjklhd->bhrjqkl', qg, kg).astype(jnp.float32) * (Dh ** -0.5) + bias[None]
    probs = jax.nn.softmax(s.reshape(B, H, rows, n_cb, NA_QBLOCK_W, kh * NA_KBLOCK_W), axis=-1).astype(v.dtype)
    out = jnp.einsum('bhrjqn,brjnhd->brjqhd', probs, vg.reshape(B, rows, n_cb, kh * NA_KBLOCK_W, H, Dh))
    return out.reshape(B, S, H * Dh)


def mlstm_chunkwise(q, k, v, i_pre, logf):
    B, H, S, Dh = q.shape
    L = ML_CHUNK
    nc = S // L
    qc = q.reshape(B, H, nc, L, Dh)
    kc = k.reshape(B, H, nc, L, Dh)
    vc = v.reshape(B, H, nc, L, Dh)
    ic = i_pre.reshape(B, H, nc, L)
    b = jnp.cumsum(logf.reshape(B, H, nc, L), axis=-1)
    g = b[..., -1]
    a = g[..., None] - b + ic

    def chunk_step(carry, xs):
        C, n, m = carry
        k_c, v_c, a_c, g_c = xs
        m_new = jnp.maximum(g_c + m, jnp.max(a_c, axis=-1))
        w = jnp.exp(a_c - m_new[..., None])
        decay = jnp.exp(g_c + m - m_new)
        C_new = decay[..., None, None] * C + jnp.einsum('bhl,bhld,bhle->bhde', w, v_c, k_c)
        n_new = decay[..., None] * n + jnp.einsum('bhl,bhle->bhe', w, k_c)
        return (C_new, n_new, m_new), (C, n, m)

    init = (jnp.zeros((B, H, Dh, Dh), q.dtype), jnp.zeros((B, H, Dh), q.dtype), jnp.zeros((B, H), q.dtype))
    xs = (jnp.moveaxis(kc, 2, 0), jnp.moveaxis(vc, 2, 0), jnp.moveaxis(a, 2, 0), jnp.moveaxis(g, 2, 0))
    _, (C_prev, n_prev, m_prev) = lax.scan(chunk_step, init, xs)
    C_prev = jnp.moveaxis(C_prev, 0, 2)
    n_prev = jnp.moveaxis(n_prev, 0, 2)
    m_prev = jnp.moveaxis(m_prev, 0, 2)
    lower = np.tril(np.ones((L, L), dtype=bool))
    log_d = jnp.where(lower, b[..., :, None] - b[..., None, :] + ic[..., None, :], -jnp.inf)
    log_inter = b + m_prev[..., None]
    m_out = jnp.maximum(log_inter, jnp.max(log_d, axis=-1))
    s = jnp.einsum('bhcld,bhcsd->bhcls', qc, kc) * jnp.exp(log_d - m_out[..., None])
    w_inter = jnp.exp(log_inter - m_out)
    num = jnp.einsum('bhcls,bhcsd->bhcld', s, vc) + w_inter[..., None] * jnp.einsum('bhcde,bhcle->bhcld', C_prev, qc)
    den = jnp.sum(s, axis=-1) + w_inter * jnp.einsum('bhce,bhcle->bhcl', n_prev, qc)
    h = num / jnp.maximum(jnp.abs(den), jnp.exp(-m_out))[..., None]
    return h.reshape(B, H, S, Dh)


def hybrid_mixer(a, w_in, b_gate, conv_w, conv_b, rpb, g_na, g_ml, w_out):
    B, S, _ = a.shape
    z = jnp.einsum('bsd,de->bse', a, w_in)
    q_na, k_na, v_na, qk_ml, v_ml, o_ml, gate_pre = jnp.split(
        z, [D_NA, 2 * D_NA, 3 * D_NA, 3 * D_NA + 2 * D_ML, 3 * D_NA + 3 * D_ML, 3 * D_NA + 4 * D_ML], axis=-1)
    shape_na = (B, S, NA_HEADS, NA_HEAD_DIM)
    y_na = neighbourhood_attention(q_na.reshape(shape_na), k_na.reshape(shape_na), v_na.reshape(shape_na), rpb)
    qk = lax.conv_general_dilated(qk_ml, conv_w[:, None, :].astype(qk_ml.dtype), window_strides=(1,),
                                  padding=((ML_CONV // 2, ML_CONV // 2),),
                                  dimension_numbers=('NWC', 'WIO', 'NWC'), feature_group_count=2 * D_ML)
    qk = jax.nn.silu(qk + conv_b)
    q_ml, k_ml = jnp.split(qk, 2, axis=-1)

    def to_heads(t):
        return t.reshape(B, S, ML_HEADS, ML_HEAD_DIM).transpose(0, 2, 1, 3).astype(jnp.float32)

    q_h = to_heads(q_ml) * (ML_HEAD_DIM ** -0.5)
    k_h = to_heads(k_ml)
    v_h = to_heads(v_ml)
    gates = (gate_pre.astype(jnp.float32) + b_gate.astype(jnp.float32)).reshape(B, S, 4, ML_HEADS).transpose(2, 0, 3, 1)
    i_fw, f_fw, i_bw, f_bw = gates[0], gates[1], gates[2], gates[3]
    h_fw = mlstm_chunkwise(q_h, k_h, v_h, i_fw, jax.nn.log_sigmoid(f_fw))
    flip = lambda t: jnp.flip(t, axis=2)
    h_bw = flip(mlstm_chunkwise(flip(q_h), flip(k_h), flip(v_h), flip(i_bw), flip(jax.nn.log_sigmoid(f_bw))))
    h_ml = (h_fw + h_bw).transpose(0, 2, 1, 3).reshape(B, S, D_ML).astype(a.dtype)
    y_ml = jax.nn.sigmoid(o_ml) * head_rmsnorm(h_ml, g_ml, ML_HEADS)
    y = jnp.concatenate([head_rmsnorm(y_na, g_na, NA_HEADS), y_ml], axis=-1)
    return jnp.einsum('bse,ed->bsd', y, w_out)


def routed_experts(t, expert_id, weights, w_gate, w_up, w_down):
    T, D = t.shape
    P = T * TOP_K
    n_blocks = P // MOE_BLOCK + N_EXPERTS
    flat_e = expert_id.reshape(-1).astype(jnp.int32)
    flat_tok = jnp.arange(P, dtype=jnp.int32) // TOP_K
    flat_w = weights.reshape(-1)
    order = jnp.argsort(flat_e)
    se, stok, sw = flat_e[order], flat_tok[order], flat_w[order]
    counts = jnp.bincount(flat_e, length=N_EXPERTS)
    padded = ((counts + MOE_BLOCK - 1) // MOE_BLOCK) * MOE_BLOCK
    pad_end = jnp.cumsum(padded)
    pad_start = pad_end - padded
    start = jnp.cumsum(counts) - counts
    dest = pad_start[se] + (jnp.arange(P, dtype=jnp.int32) - start[se])
    buf_tok = jnp.zeros((n_blocks * MOE_BLOCK,), jnp.int32).at[dest].set(stok)
    buf_w = jnp.zeros((n_blocks * MOE_BLOCK,), t.dtype).at[dest].set(sw.astype(t.dtype))
    block_e = jnp.minimum(jnp.searchsorted(pad_end, jnp.arange(n_blocks, dtype=jnp.int32) * MOE_BLOCK, side='right'),
                          N_EXPERTS - 1)

    def block_ffn(args):
        tok, wt, e = args
        xb = t[tok]
        hid = jax.nn.silu(xb @ w_gate[e]) * (xb @ w_up[e])
        return (hid @ w_down[e]) * wt[:, None]

    yb = lax.map(block_ffn, (buf_tok.reshape(n_blocks, MOE_BLOCK), buf_w.reshape(n_blocks, MOE_BLOCK), block_e))
    return jax.ops.segment_sum(yb.reshape(-1, D), buf_tok, num_segments=T)


def hierarchical_moe(x, w_rg, b_rg, w_re, b_re, w_gate, w_up, w_down):
    B, S, D = x.shape
    t = x.reshape(B * S, D)
    T = t.shape[0]
    p_group = jax.nn.softmax((t @ w_rg).astype(jnp.float32) + b_rg.astype(jnp.float32), axis=-1)
    pg_top, g_idx = lax.top_k(p_group, 1)
    e_logits = ((t @ w_re).astype(jnp.float32) + b_re.astype(jnp.float32)).reshape(T, N_GROUPS, EXPERTS_PER_GROUP)
    e_logits = e_logits[jnp.arange(T), g_idx[:, 0]]
    pe_top, e_local = lax.top_k(jax.nn.softmax(e_logits, axis=-1), TOP_K)
    weights = pg_top * pe_top / jnp.sum(pe_top, axis=-1, keepdims=True)
    expert_id = g_idx * EXPERTS_PER_GROUP + e_local
    y = routed_experts(t, expert_id, weights, w_gate, w_up, w_down)
    return y.reshape(B, S, D)


def per_layer_embedding(h, p_i, g, w_ple, w_pg):
    gate = jax.nn.sigmoid(jnp.einsum('bsd,de->bse', rmsnorm(h, g), w_pg))
    return jnp.einsum('bsp,pd->bsd', p_i, w_ple) * gate


def setup_inputs(seed: int = 0) -> dict:
    key = jax.random.key(seed)
    ks = jax.random.split(key, 26)
    f32 = jnp.float32

    def nrm(k, shape, scale):
        return jax.random.normal(k, shape, f32) * scale

    def gain(k, shape):
        return 1.0 + nrm(k, shape, 0.02)

    f_bias = jnp.linspace(3.0, 6.0, ML_HEADS, dtype=f32)
    b_gate = jnp.concatenate([
        nrm(ks[2], (DEPTH, ML_HEADS), 0.1),
        f_bias + nrm(ks[3], (DEPTH, ML_HEADS), 0.1),
        nrm(ks[4], (DEPTH, ML_HEADS), 0.1),
        f_bias + nrm(ks[5], (DEPTH, ML_HEADS), 0.1)], axis=-1)
    return {
        'x': nrm(ks[0], (BATCH, SEQ, D_MODEL), 1.0),
        'p': nrm(ks[1], (DEPTH, BATCH, SEQ, D_PLE), 1.0),
        'w_in': nrm(ks[6], (DEPTH, D_MODEL, D_IN), D_MODEL ** -0.5),
        'b_gate': b_gate,
        'conv_w': nrm(ks[7], (DEPTH, ML_CONV, 2 * D_ML), ML_CONV ** -0.5),
        'conv_b': nrm(ks[8], (DEPTH, 2 * D_ML), 0.02),
        'rpb': nrm(ks[9], (DEPTH, NA_HEADS, NA_BIAS_H, NA_BIAS_W), 0.1),
        'g_na': gain(ks[10], (DEPTH, D_NA)),
        'g_ml': gain(ks[11], (DEPTH, D_ML)),
        'w_out': nrm(ks[12], (DEPTH, D_MIX, D_MODEL), D_MIX ** -0.5),
        'g_mix': gain(ks[13], (DEPTH, D_MODEL)),
        'g_moe': gain(ks[14], (DEPTH, D_MODEL)),
        'w_route_group': nrm(ks[15], (DEPTH, D_MODEL, N_GROUPS), D_MODEL ** -0.5),
        'b_route_group': nrm(ks[16], (DEPTH, N_GROUPS), 0.01),
        'w_route_expert': nrm(ks[17], (DEPTH, D_MODEL, N_EXPERTS), D_MODEL ** -0.5),
        'b_route_expert': nrm(ks[18], (DEPTH, N_EXPERTS), 0.01),
        'w_exp_gate': nrm(ks[19], (DEPTH, N_EXPERTS, D_MODEL, D_EXPERT), D_MODEL ** -0.5),
        'w_exp_up': nrm(ks[20], (DEPTH, N_EXPERTS, D_MODEL, D_EXPERT), D_MODEL ** -0.5),
        'w_exp_down': nrm(ks[21], (DEPTH, N_EXPERTS, D_EXPERT, D_MODEL), D_EXPERT ** -0.5),
        'g_ple': gain(ks[22], (DEPTH, D_MODEL)),
        'w_ple': nrm(ks[23], (DEPTH, D_PLE, D_MODEL), D_PLE ** -0.5),
        'w_ple_gate': nrm(ks[24], (DEPTH, D_MODEL, D_MODEL), D_MODEL ** -0.5),
        'g_final': gain(ks[25], (D_MODEL,)),
    }


def reference(x, p, w_in, b_gate, conv_w, conv_b, rpb, g_na, g_ml, w_out, g_mix, g_moe,
              w_route_group, b_route_group, w_route_expert, b_route_expert,
              w_exp_gate, w_exp_up, w_exp_down, g_ple, w_ple, w_ple_gate, g_final):
    h = x
    for i in range(DEPTH):
        h = h + hybrid_mixer(rmsnorm(h, g_mix[i]), w_in[i], b_gate[i], conv_w[i], conv_b[i], rpb[i],
                             g_na[i], g_ml[i], w_out[i])
        h = h + hierarchical_moe(rmsnorm(h, g_moe[i]), w_route_group[i], b_route_group[i],
                                 w_route_expert[i], b_route_expert[i],
                                 w_exp_gate[i], w_exp_up[i], w_exp_down[i])
        h = h + per_layer_embedding(h, p[i], g_ple[i], w_ple[i], w_ple_gate[i])
    return rmsnorm(h, g_final)
```

```python
import functools

import numpy as np
import jax
import jax.numpy as jnp
from jax import lax
from jax.experimental import pallas as pl
from jax.experimental.pallas import tpu as pltpu

F32 = jnp.float32
BF16 = jnp.bfloat16

D_MODEL = 1024
GRID_W = 64
D_NA = 512
NA_HEADS = 8
NA_HEAD_DIM = 64
NA_WIN_H = 8
NA_WIN_W = 16
D_ML = 512
ML_HEADS = 4
ML_HEAD_DIM = 128
ML_CONV = 5
ML_CHUNK = 128
N_GROUPS = 4
EXPERTS_PER_GROUP = 8
N_EXPERTS = 32
D_EXPERT = 512
D_PLE = 256
EPS = 1e-6
NEG_INF = -1e30

LANES = 128
SUBLANES = 8
ROW_TILE = 512
MOE_BLOCK = 512
HALO = SUBLANES
EXPERT_LANE0 = 8
VMEM_LIMIT = 56 * 1024 * 1024


def _sigmoid(x):
    return 1.0 / (1.0 + jnp.exp(-x))


def _log_sigmoid(x):
    return jnp.minimum(x, 0.0) - jnp.log1p(jnp.exp(-jnp.abs(x)))


def _rms_scale(x):
    return x * lax.rsqrt(jnp.mean(x * x, axis=-1, keepdims=True) + EPS)


def _cumsum_chunk(x, axis, reverse):
    n = x.shape[axis]
    idx = lax.broadcasted_iota(jnp.int32, x.shape, axis)
    sh = 1
    while sh < n:
        if reverse:
            x = x + jnp.where(idx < n - sh, pltpu.roll(x, n - sh, axis), 0.0)
        else:
            x = x + jnp.where(idx >= sh, pltpu.roll(x, sh, axis), 0.0)
        sh *= 2
    return x


def _in_proj_kernel(hp_ref, h_ref, hn_ref, g_ref, wna_ref, wqk_ref, wvo_ref, wgc_ref, wgr_ref,
                    bgc_ref, bgr_ref, cw_ref, cb_ref,
                    qna_ref, kna_ref, vna_ref, qml_ref, kml_ref, vml_ref, oml_ref, gc_ref, gr_ref,
                    zbuf, *, tiles_per_seq):
    i = pl.program_id(0)
    tm = h_ref.shape[0]
    g = g_ref[...]

    def norm(x):
        return (_rms_scale(x) * g).astype(BF16)

    a = norm(h_ref[...])
    qna_ref[...] = (jnp.dot(a, wna_ref[:, 0:D_NA], preferred_element_type=F32) * (NA_HEAD_DIM ** -0.5)).astype(BF16)
    kna_ref[...] = jnp.dot(a, wna_ref[:, D_NA:2 * D_NA], preferred_element_type=F32).astype(BF16)
    vna_ref[...] = jnp.dot(a, wna_ref[:, 2 * D_NA:3 * D_NA], preferred_element_type=F32).astype(BF16)
    vml_ref[...] = jnp.dot(a, wvo_ref[:, 0:D_ML], preferred_element_type=F32).astype(BF16)
    oml_ref[...] = jnp.dot(a, wvo_ref[:, D_ML:2 * D_ML], preferred_element_type=F32).astype(BF16)

    pos = i % tiles_per_seq
    zp = jnp.dot(norm(hp_ref[...]), wqk_ref[...], preferred_element_type=F32)
    zn = jnp.dot(norm(hn_ref[...]), wqk_ref[...], preferred_element_type=F32)
    zbuf[0:HALO, :] = jnp.where(pos == 0, 0.0, zp)
    zbuf[HALO:HALO + tm, :] = jnp.dot(a, wqk_ref[...], preferred_element_type=F32)
    zbuf[HALO + tm:2 * HALO + tm, :] = jnp.where(pos == tiles_per_seq - 1, 0.0, zn)
    cstep = 256
    for c0 in range(0, 2 * D_ML, cstep):
        acc = cb_ref[:, c0:c0 + cstep]
        for j in range(ML_CONV):
            r0 = HALO - ML_CONV // 2 + j
            acc = acc + zbuf[r0:r0 + tm, c0:c0 + cstep] * cw_ref[j:j + 1, c0:c0 + cstep]
        y = acc * _sigmoid(acc)
        if c0 < D_ML:
            qml_ref[:, c0:c0 + cstep] = (y * (ML_HEAD_DIM ** -0.5)).astype(BF16)
        else:
            kml_ref[:, c0 - D_ML:c0 - D_ML + cstep] = y.astype(BF16)

    gl = jnp.dot(a, wgc_ref[...], preferred_element_type=F32) + bgc_ref[...]
    lane = lax.broadcasted_iota(jnp.int32, (ML_CHUNK, LANES), 1)
    fw_f = (lane >= ML_HEADS) & (lane < 2 * ML_HEADS)
    bw_f = (lane >= 3 * ML_HEADS) & (lane < 4 * ML_HEADS)
    for c in range(tm // ML_CHUNK):
        sl = gl[c * ML_CHUNK:(c + 1) * ML_CHUNK, :]
        lf = _log_sigmoid(sl)
        pre = _cumsum_chunk(lf, 0, False)
        suf = _cumsum_chunk(lf, 0, True)
        gc_ref[c * ML_CHUNK:(c + 1) * ML_CHUNK, :] = jnp.where(fw_f, pre, jnp.where(bw_f, suf, sl))
    gr = lax.dot_general(wgr_ref[...], a, (((1,), (1,)), ((), ())), preferred_element_type=F32) + bgr_ref[...]
    row = lax.broadcasted_iota(jnp.int32, (4 * ML_HEADS, ML_CHUNK), 0)
    fw_r = (row >= ML_HEADS) & (row < 2 * ML_HEADS)
    bw_r = (row >= 3 * ML_HEADS) & (row < 4 * ML_HEADS)
    for c in range(tm // ML_CHUNK):
        sl = gr[:, c * ML_CHUNK:(c + 1) * ML_CHUNK]
        lf = _log_sigmoid(sl)
        pre = _cumsum_chunk(lf, 1, False)
        suf = _cumsum_chunk(lf, 1, True)
        gr_ref[:, c * ML_CHUNK:(c + 1) * ML_CHUNK] = jnp.where(fw_r, pre, jnp.where(bw_r, suf, sl))


def _in_proj(h, seq, g_mix, w_in, b_gate, conv_w, conv_b):
    t = h.shape[0]
    tm = ROW_TILE
    n_tiles = t // tm
    tiles_per_seq = seq // tm
    nh = 4 * ML_HEADS
    o_qk = 3 * D_NA
    o_v = o_qk + 2 * D_ML
    o_g = o_v + 2 * D_ML
    wna = w_in[:, :o_qk].astype(BF16)
    wqk = w_in[:, o_qk:o_v].astype(BF16)
    wvo = w_in[:, o_v:o_g].astype(BF16)
    wg = w_in[:, o_g:o_g + nh]
    wgc = jnp.pad(wg, ((0, 0), (0, LANES - nh))).astype(BF16)
    wgr = wg.T.astype(BF16)
    bgc = jnp.pad(b_gate, (0, LANES - nh)).reshape(1, LANES)
    bgr = b_gate.reshape(nh, 1)
    cw = jnp.pad(conv_w, ((0, SUBLANES - ML_CONV), (0, 0)))
    cb = conv_b.reshape(1, 2 * D_ML)
    full = lambda shape: pl.BlockSpec(shape, lambda i: (0,) * len(shape))
    row_blk = lambda w: pl.BlockSpec((tm, w), lambda i: (i, 0))
    hb = tm // HALO
    out_bf = jax.ShapeDtypeStruct((t, D_NA), BF16)
    return pl.pallas_call(
        functools.partial(_in_proj_kernel, tiles_per_seq=tiles_per_seq),
        grid=(n_tiles,),
        in_specs=[
            pl.BlockSpec((HALO, D_MODEL), lambda i: (jnp.maximum(i * hb - 1, 0), 0)),
            row_blk(D_MODEL),
            pl.BlockSpec((HALO, D_MODEL), lambda i: (jnp.minimum((i + 1) * hb, t // HALO - 1), 0)),
            full((1, D_MODEL)), full((D_MODEL, o_qk)), full((D_MODEL, 2 * D_ML)), full((D_MODEL, 2 * D_ML)),
            full((D_MODEL, LANES)), full((nh, D_MODEL)), full((1, LANES)), full((nh, 1)),
            full((SUBLANES, 2 * D_ML)), full((1, 2 * D_ML)),
        ],
        out_specs=[row_blk(D_NA)] * 7 + [row_blk(LANES), pl.BlockSpec((nh, tm), lambda i: (0, i))],
        out_shape=[out_bf] * 7 + [jax.ShapeDtypeStruct((t, LANES), F32), jax.ShapeDtypeStruct((nh, t), F32)],
        scratch_shapes=[pltpu.VMEM((tm + 2 * HALO, 2 * D_ML), F32)],
        compiler_params=pltpu.CompilerParams(dimension_semantics=("arbitrary",), vmem_limit_bytes=VMEM_LIMIT),
        name="in_proj",
    )(h, h, h, g_mix.reshape(1, D_MODEL), wna, wqk, wvo, wgc, wgr, bgc, bgr, cw, cb)


def _na_bias_table(rpb, rows):
    kh = min(NA_WIN_H, rows)
    o = np.arange(kh)[:, None]
    kr = np.arange(kh)[None, :]
    dr = kr - o + (NA_WIN_H - 1)
    qc = np.arange(GRID_W)[:, None]
    kc = np.arange(GRID_W)[None, :]
    cs = np.clip(qc - NA_WIN_W // 2, 0, GRID_W - NA_WIN_W)
    inw = (kc >= cs) & (kc < cs + NA_WIN_W)
    dc = np.clip(kc - qc, 1 - NA_WIN_W, NA_WIN_W - 1) + (NA_WIN_W - 1)
    b = rpb.astype(F32)[:, dr[:, None, :, None], dc[None, :, None, :]]
    b = jnp.where(inw[None, None, :, None, :], b, NEG_INF)
    return b.reshape(rpb.shape[0], kh, GRID_W, kh * GRID_W)


def _na_kernel(q_ref, k_ref, v_ref, bias_ref, g_ref, o_ref, *, rows, kh):
    nkeys = kh * GRID_W
    lane = lax.broadcasted_iota(jnp.int32, (GRID_W, LANES), 1)
    g = g_ref[...]

    def row_body(r, carry):
        rs = jnp.clip(r - kh // 2, 0, rows - kh)
        off = r - rs
        q = q_ref[0, pl.ds(pl.multiple_of(r * GRID_W, GRID_W), GRID_W), :]
        kb = k_ref[0, pl.ds(pl.multiple_of(rs * GRID_W, GRID_W), nkeys), :]
        vb = v_ref[0, pl.ds(pl.multiple_of(rs * GRID_W, GRID_W), nkeys), :]
        y = None
        for hh in range(LANES // NA_HEAD_DIM):
            msk = (lane >= hh * NA_HEAD_DIM) & (lane < (hh + 1) * NA_HEAD_DIM)
            qh = jnp.where(msk, q, jnp.zeros_like(q))
            s = lax.dot_general(qh, kb, (((1,), (1,)), ((), ())), preferred_element_type=F32)
            s = s + bias_ref[hh, off]
            m = jnp.max(s, axis=-1, keepdims=True)
            p = jnp.exp(s - m)
            l = jnp.sum(p, axis=-1, keepdims=True)
            pv = jnp.dot(p.astype(BF16), vb, preferred_element_type=F32)
            oh = jnp.where(msk, pv / l, 0.0)
            ms = jnp.sum(oh * oh, axis=-1, keepdims=True) * (1.0 / NA_HEAD_DIM)
            yh = oh * lax.rsqrt(ms + EPS)
            y = yh if y is None else y + yh
        o_ref[0, pl.ds(pl.multiple_of(r * GRID_W, GRID_W), GRID_W), :] = (y * g).astype(BF16)
        return carry

    lax.fori_loop(0, rows, row_body, 0)


def _na_attention(q, k, v, bias, g_na, batch, seq):
    rows = seq // GRID_W
    kh = min(NA_WIN_H, rows)
    hpb = LANES // NA_HEAD_DIM
    nblk = D_NA // LANES
    q3, k3, v3 = (x.reshape(batch, seq, D_NA) for x in (q, k, v))
    seq_blk = pl.BlockSpec((1, seq, LANES), lambda b, j: (b, 0, j))
    out = pl.pallas_call(
        functools.partial(_na_kernel, rows=rows, kh=kh),
        grid=(batch, nblk),
        in_specs=[seq_blk, seq_blk, seq_blk,
                  pl.BlockSpec((hpb, kh, GRID_W, kh * GRID_W), lambda b, j: (j, 0, 0, 0)),
                  pl.BlockSpec((1, LANES), lambda b, j: (0, j))],
        out_specs=seq_blk,
        out_shape=jax.ShapeDtypeStruct((batch, seq, D_NA), BF16),
        compiler_params=pltpu.CompilerParams(dimension_semantics=("arbitrary", "arbitrary"),
                                             vmem_limit_bytes=VMEM_LIMIT),
        name="na_attn",
    )(q3, k3, v3, bias, g_na.reshape(1, D_NA))
    return out.reshape(batch * seq, D_NA)


def _mlstm_kernel(q_ref, k_ref, v_ref, gr_ref, gc_ref, o_ref, mf_ref, mb_ref, *, n_chunks):
    L = ML_CHUNK
    hd = pl.program_id(1)
    lane = lax.broadcasted_iota(jnp.int32, (L, LANES), 1)
    grow_i = lax.broadcasted_iota(jnp.int32, (4 * ML_HEADS, L), 0)
    row_i = lax.broadcasted_iota(jnp.int32, (L, L), 0)
    col_i = lax.broadcasted_iota(jnp.int32, (L, L), 1)
    ones = jnp.ones((L, ML_HEAD_DIM), BF16)

    def step(c, m_prev, st_ref, reverse):
        off = pl.multiple_of(c * L, L)
        base = 2 * ML_HEADS if reverse else 0
        qc = q_ref[0, pl.ds(off, L), :]
        kc = k_ref[0, pl.ds(off, L), :]
        vc = v_ref[0, pl.ds(off, L), :]
        gcol = gc_ref[pl.ds(off, L), :]
        grow = gr_ref[:, pl.ds(off, L)]
        i_row = jnp.sum(jnp.where(grow_i == base + hd, grow, 0.0), axis=0, keepdims=True)
        b_row = jnp.sum(jnp.where(grow_i == base + ML_HEADS + hd, grow, 0.0), axis=0, keepdims=True)
        i_col = jnp.sum(jnp.where(lane == base + hd, gcol, 0.0), axis=1, keepdims=True)
        b_col = jnp.sum(jnp.where(lane == base + ML_HEADS + hd, gcol, 0.0), axis=1, keepdims=True)
        gtot = b_row[:, 0:1] if reverse else b_row[:, L - 1:L]
        a_max = jnp.max(gtot - b_row + i_row, axis=1, keepdims=True)
        m_new = jnp.maximum(gtot + m_prev, a_max)
        w_col = jnp.exp(gtot - b_col + i_col - m_new)
        decay = jnp.exp(gtot + m_prev - m_new)
        causal = (col_i >= row_i) if reverse else (col_i <= row_i)
        log_d = jnp.where(causal, b_col - b_row + i_row, -jnp.inf)
        log_inter = b_col + m_prev
        m_out = jnp.maximum(log_inter, jnp.max(log_d, axis=1, keepdims=True))
        s = lax.dot_general(qc, kc, (((1,), (1,)), ((), ())), preferred_element_type=F32) * jnp.exp(log_d - m_out)
        w_inter = jnp.exp(log_inter - m_out)
        vext = jnp.concatenate([vc, ones], axis=1)
        state = st_ref[...]
        nd = (jnp.dot(s.astype(BF16), vext, preferred_element_type=F32)
              + w_inter * jnp.dot(qc, state.astype(BF16), preferred_element_type=F32))
        num = nd[:, :ML_HEAD_DIM]
        den = nd[:, ML_HEAD_DIM:]
        h = num / jnp.maximum(jnp.abs(den), jnp.exp(-m_out))
        wv = (w_col * vext.astype(F32)).astype(BF16)
        st_ref[...] = decay * state + lax.dot_general(kc, wv, (((0,), (0,)), ((), ())), preferred_element_type=F32)
        return h, m_new

    mf_ref[...] = jnp.zeros_like(mf_ref)
    mb_ref[...] = jnp.zeros_like(mb_ref)

    def first_half(j, carry):
        m_f, m_b = carry
        jb = n_chunks - 1 - j
        h_f, m_f = step(j, m_f, mf_ref, False)
        o_ref[0, pl.ds(pl.multiple_of(j * L, L), L), :] = h_f
        h_b, m_b = step(jb, m_b, mb_ref, True)
        o_ref[0, pl.ds(pl.multiple_of(jb * L, L), L), :] = h_b
        return m_f, m_b

    def second_half(j, carry):
        m_f, m_b = carry
        jb = n_chunks - 1 - j
        h_f, m_f = step(j, m_f, mf_ref, False)
        o_ref[0, pl.ds(pl.multiple_of(j * L, L), L), :] += h_f
        h_b, m_b = step(jb, m_b, mb_ref, True)
        o_ref[0, pl.ds(pl.multiple_of(jb * L, L), L), :] += h_b
        return m_f, m_b

    zero = jnp.zeros((1, 1), F32)
    carry = lax.fori_loop(0, n_chunks // 2, first_half, (zero, zero))
    lax.fori_loop(n_chunks // 2, n_chunks, second_half, carry)


def _mlstm(q, k, v, g_row, g_col, batch, seq):
    n_chunks = seq // ML_CHUNK
    assert n_chunks % 2 == 0
    q3, k3, v3 = (x.reshape(batch, seq, D_ML) for x in (q, k, v))
    seq_blk = pl.BlockSpec((1, seq, ML_HEAD_DIM), lambda b, j: (b, 0, j))
    out = pl.pallas_call(
        functools.partial(_mlstm_kernel, n_chunks=n_chunks),
        grid=(batch, ML_HEADS),
        in_specs=[seq_blk, seq_blk, seq_blk,
                  pl.BlockSpec((4 * ML_HEADS, seq), lambda b, j: (0, b)),
                  pl.BlockSpec((seq, LANES), lambda b, j: (b, 0))],
        out_specs=seq_blk,
        out_shape=jax.ShapeDtypeStruct((batch, seq, D_ML), F32),
        scratch_shapes=[pltpu.VMEM((ML_HEAD_DIM, 2 * ML_HEAD_DIM), F32)] * 2,
        compiler_params=pltpu.CompilerParams(dimension_semantics=("arbitrary", "arbitrary"),
                                             vmem_limit_bytes=VMEM_LIMIT),
        name="mlstm",
    )(q3, k3, v3, g_row, g_col)
    return out.reshape(batch * seq, D_ML)


def _mix_out_kernel(h_ref, yna_ref, hml_ref, oml_ref, gml_ref, wo_ref, gmoe_ref, wrh_ref, wrl_ref, br_ref, tri_ref,
                    h1_ref, t_ref, info_ref, cnt_ref):
    i = pl.program_id(0)
    tm = h_ref.shape[0]

    @pl.when(i == 0)
    def _():
        cnt_ref[...] = jnp.zeros_like(cnt_ref)

    hml = hml_ref[...]
    parts = [_rms_scale(hml[:, d * ML_HEAD_DIM:(d + 1) * ML_HEAD_DIM]) for d in range(ML_HEADS)]
    yml = _sigmoid(oml_ref[...].astype(F32)) * (jnp.concatenate(parts, axis=1) * gml_ref[...])
    mix = (jnp.dot(yna_ref[...], wo_ref[0:D_NA, :], preferred_element_type=F32)
           + jnp.dot(yml.astype(BF16), wo_ref[D_NA:D_NA + D_ML, :], preferred_element_type=F32))
    h1 = h_ref[...] + mix
    h1_ref[...] = h1
    t = _rms_scale(h1) * gmoe_ref[...]
    t_ref[...] = t

    t_hi = t.astype(BF16)
    t_lo = (t - t_hi.astype(F32)).astype(BF16)
    logits = (jnp.dot(t_hi, wrh_ref[...], preferred_element_type=F32)
              + jnp.dot(t_lo, wrh_ref[...], preferred_element_type=F32)
              + jnp.dot(t_hi, wrl_ref[...], preferred_element_type=F32)) + br_ref[...]
    lane = lax.broadcasted_iota(jnp.int32, (tm, LANES), 1)
    big = jnp.int32(LANES)

    def top(x):
        mx = jnp.max(x, axis=1, keepdims=True)
        idx = jnp.min(jnp.where(x == mx, lane, big), axis=1, keepdims=True)
        return mx, idx

    lg = jnp.where(lane < N_GROUPS, logits, -jnp.inf)
    g_max, g_idx = top(lg)
    pg_top = 1.0 / jnp.sum(jnp.exp(lg - g_max), axis=1, keepdims=True)
    lo = EXPERT_LANE0 + g_idx * EXPERTS_PER_GROUP
    le = jnp.where((lane >= lo) & (lane < lo + EXPERTS_PER_GROUP), logits, -jnp.inf)
    e1_max, i1 = top(le)
    e_sum = jnp.sum(jnp.exp(le - e1_max), axis=1, keepdims=True)
    e2_max, i2 = top(jnp.where(lane == i1, -jnp.inf, le))
    p1 = 1.0 / e_sum
    p2 = jnp.exp(e2_max - e1_max) / e_sum
    w1 = pg_top * p1 / (p1 + p2)
    w2 = pg_top * p2 / (p1 + p2)
    e1 = i1 - EXPERT_LANE0
    e2 = i2 - EXPERT_LANE0

    oh1 = (lane == e1).astype(F32)
    oh2 = (lane == e2).astype(F32)
    oh = oh1 + oh2
    before = jnp.dot(tri_ref[...], oh.astype(BF16), preferred_element_type=F32) + cnt_ref[...]
    r1 = jnp.sum(oh1 * before, axis=1, keepdims=True)
    r2 = jnp.sum(oh2 * before, axis=1, keepdims=True)
    cnt_ref[...] = cnt_ref[...] + jnp.sum(oh, axis=0, keepdims=True)
    cols = [e1.astype(F32), e2.astype(F32), w1, w2, r1, r2]
    info = jnp.zeros((tm, LANES), F32)
    for j, cval in enumerate(cols):
        info = jnp.where(lane == j, cval, info)
    info_ref[...] = info


def _mix_out(h, yna, hml, oml, g_ml, w_out, g_moe, w_rg, b_rg, w_re, b_re):
    t = h.shape[0]
    tm = ROW_TILE
    wr = jnp.zeros((D_MODEL, LANES), F32)
    wr = wr.at[:, :N_GROUPS].set(w_rg).at[:, EXPERT_LANE0:EXPERT_LANE0 + N_EXPERTS].set(w_re)
    wrh = wr.astype(BF16)
    wrl = (wr - wrh.astype(F32)).astype(BF16)
    br = jnp.zeros((1, LANES), F32)
    br = br.at[0, :N_GROUPS].set(b_rg).at[0, EXPERT_LANE0:EXPERT_LANE0 + N_EXPERTS].set(b_re)
    tri = jnp.asarray(np.tril(np.ones((tm, tm), np.float32), -1), BF16)
    full = lambda shape: pl.BlockSpec(shape, lambda i: (0,) * len(shape))
    row_blk = lambda w: pl.BlockSpec((tm, w), lambda i: (i, 0))
    return pl.pallas_call(
        _mix_out_kernel,
        grid=(t // tm,),
        in_specs=[row_blk(D_MODEL), row_blk(D_NA), row_blk(D_ML), row_blk(D_ML), full((1, D_ML)),
                  full((D_MODEL, D_MODEL)), full((1, D_MODEL)), full((D_MODEL, LANES)), full((D_MODEL, LANES)),
                  full((1, LANES)), full((tm, tm))],
        out_specs=[row_blk(D_MODEL), row_blk(D_MODEL), row_blk(LANES), full((1, LANES))],
        out_shape=[jax.ShapeDtypeStruct((t, D_MODEL), F32), jax.ShapeDtypeStruct((t, D_MODEL), F32),
                   jax.ShapeDtypeStruct((t, LANES), F32), jax.ShapeDtypeStruct((1, LANES), F32)],
        compiler_params=pltpu.CompilerParams(dimension_semantics=("arbitrary",), vmem_limit_bytes=VMEM_LIMIT),
        name="mix_out",
    )(h, yna, hml, oml, g_ml.reshape(1, D_ML), w_out.astype(BF16), g_moe.reshape(1, D_MODEL), wrh, wrl, br, tri)


def _dispatch_kernel(dest_ref, t_ref, xs_in_ref, xs_ref, sem):
    del xs_in_ref
    tm = t_ref.shape[0]

    def row_copy(j, d):
        return pltpu.make_async_copy(t_ref.at[pl.ds(j, 1)], xs_ref.at[pl.ds(d, 1)], sem)

    def issue(j, carry):
        row_copy(j, dest_ref[0, 0, j]).start()
        row_copy(j, dest_ref[0, 0, tm + j]).start()
        return carry

    lax.fori_loop(0, tm, issue, 0)
    for _ in range(2):
        pltpu.make_async_copy(t_ref, xs_ref.at[pl.ds(0, tm)], sem).wait()


def _dispatch(t, dest, n_rows):
    tt = t.shape[0]
    tm = ROW_TILE
    n_tiles = tt // tm
    dest3 = jnp.concatenate([dest[:, 0].reshape(n_tiles, 1, tm), dest[:, 1].reshape(n_tiles, 1, tm)], axis=2)
    xs0 = jnp.zeros((n_rows, D_MODEL), F32)
    return pl.pallas_call(
        _dispatch_kernel,
        grid=(n_tiles,),
        in_specs=[pl.BlockSpec((1, 1, 2 * tm), lambda i: (i, 0, 0), memory_space=pltpu.SMEM),
                  pl.BlockSpec((tm, D_MODEL), lambda i: (i, 0)),
                  pl.BlockSpec(memory_space=pl.ANY)],
        out_specs=pl.BlockSpec(memory_space=pl.ANY),
        out_shape=jax.ShapeDtypeStruct((n_rows, D_MODEL), F32),
        scratch_shapes=[pltpu.SemaphoreType.DMA],
        input_output_aliases={2: 0},
        compiler_params=pltpu.CompilerParams(dimension_semantics=("arbitrary",), vmem_limit_bytes=VMEM_LIMIT),
        name="dispatch",
    )(dest3, t, xs0)


def _ffn_kernel(be_ref, nu_ref, x_ref, wg_ref, wu_ref, wd_ref, y_ref):
    del be_ref

    @pl.when(pl.program_id(0) < nu_ref[0])
    def _():
        x = x_ref[...].astype(BF16)
        g = jnp.dot(x, wg_ref[0], preferred_element_type=F32)
        u = jnp.dot(x, wu_ref[0], preferred_element_type=F32)
        hid = (g * _sigmoid(g) * u).astype(BF16)
        y_ref[...] = jnp.dot(hid, wd_ref[0], preferred_element_type=F32)

    @pl.when(pl.program_id(0) >= nu_ref[0])
    def _():
        y_ref[...] = jnp.zeros_like(y_ref)


def _expert_ffn(xs, block_expert, n_used, w_gate, w_up, w_down):
    n_rows = xs.shape[0]
    nb = n_rows // MOE_BLOCK
    blk = lambda i, be, nu: (jnp.minimum(i, nu[0] - 1), 0)
    wsel = lambda i, be, nu: (be[jnp.minimum(i, nu[0] - 1)], 0, 0)
    return pl.pallas_call(
        _ffn_kernel,
        grid_spec=pltpu.PrefetchScalarGridSpec(
            num_scalar_prefetch=2,
            grid=(nb,),
            in_specs=[pl.BlockSpec((MOE_BLOCK, D_MODEL), blk),
                      pl.BlockSpec((1, D_MODEL, D_EXPERT), wsel),
                      pl.BlockSpec((1, D_MODEL, D_EXPERT), wsel),
                      pl.BlockSpec((1, D_EXPERT, D_MODEL), wsel)],
            out_specs=pl.BlockSpec((MOE_BLOCK, D_MODEL), lambda i, be, nu: (i, 0)),
        ),
        out_shape=jax.ShapeDtypeStruct((n_rows, D_MODEL), F32),
        compiler_params=pltpu.CompilerParams(dimension_semantics=("arbitrary",), vmem_limit_bytes=VMEM_LIMIT),
        name="expert_ffn",
    )(block_expert, n_used, xs, w_gate.astype(BF16), w_up.astype(BF16), w_down.astype(BF16))


def _combine_kernel(dest_ref, h1_ref, info_ref, p_ref, gple_ref, wple_ref, wpg_ref, gfin_ref, yb_ref,
                    o_ref, buf, sem, *, final):
    tm = h1_ref.shape[0]

    def row_copy(d, slot):
        return pltpu.make_async_copy(yb_ref.at[pl.ds(d, 1)], buf.at[pl.ds(slot, 1)], sem)

    def issue(j, carry):
        row_copy(dest_ref[0, 0, j], j).start()
        row_copy(dest_ref[0, 0, tm + j], tm + j).start()
        return carry

    lax.fori_loop(0, tm, issue, 0)
    pltpu.make_async_copy(yb_ref.at[pl.ds(0, 2 * tm)], buf, sem).wait()

    info = info_ref[...]
    moe = info[:, 2:3] * buf[0:tm, :] + info[:, 3:4] * buf[tm:2 * tm, :]
    h2 = h1_ref[...] + moe
    a = (_rms_scale(h2) * gple_ref[...]).astype(BF16)
    gate = _sigmoid(jnp.dot(a, wpg_ref[...], preferred_element_type=F32))
    emb = jnp.dot(p_ref[...].astype(BF16), wple_ref[...], preferred_element_type=F32)
    h3 = h2 + emb * gate
    if final:
        h3 = _rms_scale(h3) * gfin_ref[...]
    o_ref[...] = h3


def _combine(h1, yb, dest, info, p_i, g_ple, w_ple, w_pg, g_final, final):
    tt = h1.shape[0]
    tm = ROW_TILE
    n_tiles = tt // tm
    dest3 = jnp.concatenate([dest[:, 0].reshape(n_tiles, 1, tm), dest[:, 1].reshape(n_tiles, 1, tm)], axis=2)
    full = lambda shape: pl.BlockSpec(shape, lambda i: (0,) * len(shape))
    row_blk = lambda w: pl.BlockSpec((tm, w), lambda i: (i, 0))
    return pl.pallas_call(
        functools.partial(_combine_kernel, final=final),
        grid=(n_tiles,),
        in_specs=[pl.BlockSpec((1, 1, 2 * tm), lambda i: (i, 0, 0), memory_space=pltpu.SMEM),
                  row_blk(D_MODEL), row_blk(LANES), row_blk(D_PLE), full((1, D_MODEL)),
                  full((D_PLE, D_MODEL)), full((D_MODEL, D_MODEL)), full((1, D_MODEL)),
                  pl.BlockSpec(memory_space=pl.ANY)],
        out_specs=row_blk(D_MODEL),
        out_shape=jax.ShapeDtypeStruct((tt, D_MODEL), F32),
        scratch_shapes=[pltpu.VMEM((2 * tm, D_MODEL), F32), pltpu.SemaphoreType.DMA],
        compiler_params=pltpu.CompilerParams(dimension_semantics=("arbitrary",), vmem_limit_bytes=VMEM_LIMIT),
        name="combine",
    )(dest3, h1, info, p_i, g_ple.reshape(1, D_MODEL), w_ple.astype(BF16), w_pg.astype(BF16),
      g_final.reshape(1, D_MODEL), yb)


def _routing_tables(info, counts_f, n_blocks):
    e = info[:, 0:2].astype(jnp.int32)
    rank = info[:, 4:6].astype(jnp.int32)
    counts = counts_f[0, :N_EXPERTS].astype(jnp.int32)
    padded = ((counts + MOE_BLOCK - 1) // MOE_BLOCK) * MOE_BLOCK
    pad_end = jnp.cumsum(padded)
    pad_start = pad_end - padded
    dest = pad_start[e] + rank
    n_used = (pad_end[-1] // MOE_BLOCK).astype(jnp.int32).reshape(1)
    starts = jnp.arange(n_blocks, dtype=jnp.int32) * MOE_BLOCK
    block_expert = jnp.minimum(jnp.searchsorted(pad_end, starts, side="right"), N_EXPERTS - 1).astype(jnp.int32)
    return dest, block_expert, n_used


def kernel(x, p, w_in, b_gate, conv_w, conv_b, rpb, g_na, g_ml, w_out, g_mix, g_moe, w_route_group, b_route_group, w_route_expert, b_route_expert, w_exp_gate, w_exp_up, w_exp_down, g_ple, w_ple, w_ple_gate, g_final):
    batch, seq, _ = x.shape
    depth = w_in.shape[0]
    tt = batch * seq
    assert seq % ROW_TILE == 0 and seq % GRID_W == 0
    n_blocks = (2 * tt) // MOE_BLOCK + N_EXPERTS
    h = x.reshape(tt, D_MODEL)
    for i in range(depth):
        qna, kna, vna, qml, kml, vml, oml, g_col, g_row = _in_proj(h, seq, g_mix[i], w_in[i], b_gate[i], conv_w[i], conv_b[i])
        yna = _na_attention(qna, kna, vna, _na_bias_table(rpb[i], seq // GRID_W), g_na[i], batch, seq)
        hml = _mlstm(qml, kml, vml, g_row, g_col, batch, seq)
        h1, t, info, counts = _mix_out(h, yna, hml, oml, g_ml[i], w_out[i], g_moe[i],
                                       w_route_group[i], b_route_group[i], w_route_expert[i], b_route_expert[i])
        dest, block_expert, n_used = _routing_tables(info, counts, n_blocks)
        xs = _dispatch(t, dest, n_blocks * MOE_BLOCK)
        yb = _expert_ffn(xs, block_expert, n_used, w_exp_gate[i], w_exp_up[i], w_exp_down[i])
        h = _combine(h1, yb, dest, info, p[i].reshape(tt, D_PLE), g_ple[i], w_ple[i], w_ple_gate[i], g_final,
                     final=(i == depth - 1))
    return h.reshape(batch, seq, D_MODEL)
```

```python
import functools

import numpy as np
import jax
import jax.numpy as jnp
from jax import lax
from jax.experimental import pallas as pl
from jax.experimental.pallas import tpu as pltpu

F32 = jnp.float32
BF16 = jnp.bfloat16

D_MODEL = 1024
GRID_W = 64
D_NA = 512
NA_HEADS = 8
NA_HEAD_DIM = 64
NA_WIN_H = 8
NA_WIN_W = 16
D_ML = 512
ML_HEADS = 4
ML_HEAD_DIM = 128
ML_CONV = 5
ML_CHUNK = 128
N_GROUPS = 4
EXPERTS_PER_GROUP = 8
N_EXPERTS = 32
D_EXPERT = 512
D_PLE = 256
EPS = 1e-6
NEG_INF = -1e30

LANES = 128
SUBLANES = 8
ROW_TILE = 512
MOE_BLOCK = 512
HALO = SUBLANES
EXPERT_LANE0 = 8
VMEM_LIMIT = 56 * 1024 * 1024


def _sigmoid(x):
    return 1.0 / (1.0 + jnp.exp(-x))


def _log_sigmoid(x):
    return jnp.minimum(x, 0.0) - jnp.log1p(jnp.exp(-jnp.abs(x)))


def _rms_scale(x):
    return x * lax.rsqrt(jnp.mean(x * x, axis=-1, keepdims=True) + EPS)


def _cumsum_chunk(x, axis, reverse):
    n = x.shape[axis]
    idx = lax.broadcasted_iota(jnp.int32, x.shape, axis)
    sh = 1
    while sh < n:
        if reverse:
            x = x + jnp.where(idx < n - sh, pltpu.roll(x, n - sh, axis), 0.0)
        else:
            x = x + jnp.where(idx >= sh, pltpu.roll(x, sh, axis), 0.0)
        sh *= 2
    return x


def _in_proj_kernel(hp_ref, h_ref, hn_ref, g_ref, wna_ref, wqk_ref, wvo_ref, wgc_ref, wgr_ref,
                    bgc_ref, bgr_ref, cw_ref, cb_ref,
                    qna_ref, kna_ref, vna_ref, qml_ref, kml_ref, vml_ref, oml_ref, gc_ref, gr_ref,
                    zbuf, *, tiles_per_seq):
    i = pl.program_id(0)
    tm = h_ref.shape[0]
    g = g_ref[...]

    def norm(x):
        return (_rms_scale(x) * g).astype(BF16)

    a = norm(h_ref[...])
    qna_ref[...] = (jnp.dot(a, wna_ref[:, 0:D_NA], preferred_element_type=F32) * (NA_HEAD_DIM ** -0.5)).astype(BF16)
    kna_ref[...] = jnp.dot(a, wna_ref[:, D_NA:2 * D_NA], preferred_element_type=F32).astype(BF16)
    vna_ref[...] = jnp.dot(a, wna_ref[:, 2 * D_NA:3 * D_NA], preferred_element_type=F32).astype(BF16)
    vml_ref[...] = jnp.dot(a, wvo_ref[:, 0:D_ML], preferred_element_type=F32).astype(BF16)
    oml_ref[...] = jnp.dot(a, wvo_ref[:, D_ML:2 * D_ML], preferred_element_type=F32).astype(BF16)

    pos = i % tiles_per_seq
    zp = jnp.dot(norm(hp_ref[...]), wqk_ref[...], preferred_element_type=F32)
    zn = jnp.dot(norm(hn_ref[...]), wqk_ref[...], preferred_element_type=F32)
    zbuf[0:HALO, :] = jnp.where(pos == 0, 0.0, zp)
    zbuf[HALO:HALO + tm, :] = jnp.dot(a, wqk_ref[...], preferred_element_type=F32)
    zbuf[HALO + tm:2 * HALO + tm, :] = jnp.where(pos == tiles_per_seq - 1, 0.0, zn)
    cstep = 256
    for c0 in range(0, 2 * D_ML, cstep):
        acc = cb_ref[:, c0:c0 + cstep]
        for j in range(ML_CONV):
            r0 = HALO - ML_CONV // 2 + j
            acc = acc + zbuf[r0:r0 + tm, c0:c0 + cstep] * cw_ref[j:j + 1, c0:c0 + cstep]
        y = acc * _sigmoid(acc)
        if c0 < D_ML:
            qml_ref[:, c0:c0 + cstep] = (y * (ML_HEAD_DIM ** -0.5)).astype(BF16)
        else:
            kml_ref[:, c0 - D_ML:c0 - D_ML + cstep] = y.astype(BF16)

    gl = jnp.dot(a, wgc_ref[...], preferred_element_type=F32) + bgc_ref[...]
    lane = lax.broadcasted_iota(jnp.int32, (ML_CHUNK, LANES), 1)
    fw_f = (lane >= ML_HEADS) & (lane < 2 * ML_HEADS)
    bw_f = (lane >= 3 * ML_HEADS) & (lane < 4 * ML_HEADS)
    for c in range(tm // ML_CHUNK):
        sl = gl[c * ML_CHUNK:(c + 1) * ML_CHUNK, :]
        lf = _log_sigmoid(sl)
        pre = _cumsum_chunk(lf, 0, False)
        suf = _cumsum_chunk(lf, 0, True)
        gc_ref[c * ML_CHUNK:(c + 1) * ML_CHUNK, :] = jnp.where(fw_f, pre, jnp.where(bw_f, suf, sl))
    gr = lax.dot_general(wgr_ref[...], a, (((1,), (1,)), ((), ())), preferred_element_type=F32) + bgr_ref[...]
    row = lax.broadcasted_iota(jnp.int32, (4 * ML_HEADS, ML_CHUNK), 0)
    fw_r = (row >= ML_HEADS) & (row < 2 * ML_HEADS)
    bw_r = (row >= 3 * ML_HEADS) & (row < 4 * ML_HEADS)
    for c in range(tm // ML_CHUNK):
        sl = gr[:, c * ML_CHUNK:(c + 1) * ML_CHUNK]
        lf = _log_sigmoid(sl)
        pre = _cumsum_chunk(lf, 1, False)
        suf = _cumsum_chunk(lf, 1, True)
        gr_ref[:, c * ML_CHUNK:(c + 1) * ML_CHUNK] = jnp.where(fw_r, pre, jnp.where(bw_r, suf, sl))


def _in_proj(h, seq, g_mix, w_in, b_gate, conv_w, conv_b):
    t = h.shape[0]
    tm = ROW_TILE
    n_tiles = t // tm
    tiles_per_seq = seq // tm
    nh = 4 * ML_HEADS
    o_qk = 3 * D_NA
    o_v = o_qk + 2 * D_ML
    o_g = o_v + 2 * D_ML
    wna = w_in[:, :o_qk].astype(BF16)
    wqk = w_in[:, o_qk:o_v].astype(BF16)
    wvo = w_in[:, o_v:o_g].astype(BF16)
    wg = w_in[:, o_g:o_g + nh]
    wgc = jnp.pad(wg, ((0, 0), (0, LANES - nh))).astype(BF16)
    wgr = wg.T.astype(BF16)
    bgc = jnp.pad(b_gate, (0, LANES - nh)).reshape(1, LANES)
    bgr = b_gate.reshape(nh, 1)
    cw = jnp.pad(conv_w, ((0, SUBLANES - ML_CONV), (0, 0)))
    cb = conv_b.reshape(1, 2 * D_ML)
    full = lambda shape: pl.BlockSpec(shape, lambda i: (0,) * len(shape))
    row_blk = lambda w: pl.BlockSpec((tm, w), lambda i: (i, 0))
    hb = tm // HALO
    out_bf = jax.ShapeDtypeStruct((t, D_NA), BF16)
    return pl.pallas_call(
        functools.partial(_in_proj_kernel, tiles_per_seq=tiles_per_seq),
        grid=(n_tiles,),
        in_specs=[
            pl.BlockSpec((HALO, D_MODEL), lambda i: (jnp.maximum(i * hb - 1, 0), 0)),
            row_blk(D_MODEL),
            pl.BlockSpec((HALO, D_MODEL), lambda i: (jnp.minimum((i + 1) * hb, t // HALO - 1), 0)),
            full((1, D_MODEL)), full((D_MODEL, o_qk)), full((D_MODEL, 2 * D_ML)), full((D_MODEL, 2 * D_ML)),
            full((D_MODEL, LANES)), full((nh, D_MODEL)), full((1, LANES)), full((nh, 1)),
            full((SUBLANES, 2 * D_ML)), full((1, 2 * D_ML)),
        ],
        out_specs=[row_blk(D_NA)] * 7 + [row_blk(LANES), pl.BlockSpec((nh, tm), lambda i: (0, i))],
        out_shape=[out_bf] * 7 + [jax.ShapeDtypeStruct((t, LANES), F32), jax.ShapeDtypeStruct((nh, t), F32)],
        scratch_shapes=[pltpu.VMEM((tm + 2 * HALO, 2 * D_ML), F32)],
        compiler_params=pltpu.CompilerParams(dimension_semantics=("arbitrary",), vmem_limit_bytes=VMEM_LIMIT),
        name="in_proj",
    )(h, h, h, g_mix.reshape(1, D_MODEL), wna, wqk, wvo, wgc, wgr, bgc, bgr, cw, cb)


def _na_bias_table(rpb, rows):
    kh = min(NA_WIN_H, rows)
    o = np.arange(kh)[:, None]
    kr = np.arange(kh)[None, :]
    dr = kr - o + (NA_WIN_H - 1)
    qc = np.arange(GRID_W)[:, None]
    kc = np.arange(GRID_W)[None, :]
    cs = np.clip(qc - NA_WIN_W // 2, 0, GRID_W - NA_WIN_W)
    inw = (kc >= cs) & (kc < cs + NA_WIN_W)
    dc = np.clip(kc - qc, 1 - NA_WIN_W, NA_WIN_W - 1) + (NA_WIN_W - 1)
    sel_r = jnp.asarray(dr[:, :, None] == np.arange(2 * NA_WIN_H - 1), F32)
    sel_c = jnp.asarray(dc[:, :, None] == np.arange(2 * NA_WIN_W - 1), F32)
    rows_sel = jnp.einsum("hrc,okr->hokc", rpb.astype(F32), sel_r, precision=lax.Precision.HIGHEST)
    b = jnp.einsum("hokc,qlc->hoqkl", rows_sel, sel_c, precision=lax.Precision.HIGHEST)
    b = jnp.where(inw[None, None, :, None, :], b, NEG_INF)
    return b.reshape(rpb.shape[0], kh, GRID_W, kh * GRID_W)


NA_HEADS_PER_BLOCK = 4
NA_BAND_ROWS = 16
NA_UNROLL = 2


def _na_kernel(q_ref, k_ref, v_ref, bias_ref, g_ref, o_ref, *, rows, kh, band):
    nkeys = kh * GRID_W
    hpb = NA_HEADS_PER_BLOCK
    m_rows = hpb * GRID_W
    width = hpb * NA_HEAD_DIM
    band_i = pl.program_id(2)
    lane_head = lax.shift_right_logical(lax.broadcasted_iota(jnp.int32, (m_rows, width), 1), 6)
    row_head = lax.shift_right_logical(lax.broadcasted_iota(jnp.int32, (m_rows, width), 0), 6)
    own = lane_head == row_head
    g = g_ref[...]

    def one_row(rl):
        r = band_i * band + rl
        rs = jnp.clip(r - kh // 2, 0, rows - kh)
        off = r - rs
        q = q_ref[0, pl.ds(pl.multiple_of(rl * GRID_W, GRID_W), GRID_W), :]
        kb = k_ref[0, pl.ds(pl.multiple_of(rs * GRID_W, GRID_W), nkeys), :]
        vb = v_ref[0, pl.ds(pl.multiple_of(rs * GRID_W, GRID_W), nkeys), :]
        qs = jnp.where(own, jnp.concatenate([q] * hpb, axis=0), jnp.zeros((m_rows, width), BF16))
        s = lax.dot_general(qs, kb, (((1,), (1,)), ((), ())), preferred_element_type=F32)
        s = s + bias_ref[:, off].reshape(m_rows, nkeys)
        m = jnp.max(s, axis=-1, keepdims=True)
        p = jnp.exp(s - m)
        l = jnp.sum(p, axis=-1, keepdims=True)
        pv = jnp.dot(p.astype(BF16), vb, preferred_element_type=F32)
        oh = jnp.where(own, pv * (1.0 / l), 0.0)
        ms = jnp.sum(oh * oh, axis=-1, keepdims=True) * (1.0 / NA_HEAD_DIM)
        yn = oh * lax.rsqrt(ms + EPS)
        y = yn[0:GRID_W]
        for hh in range(1, hpb):
            y = y + yn[hh * GRID_W:(hh + 1) * GRID_W]
        o_ref[0, pl.ds(pl.multiple_of(rl * GRID_W, GRID_W), GRID_W), :] = (y * g).astype(BF16)

    def body(it, carry):
        for u in range(NA_UNROLL):
            one_row(it * NA_UNROLL + u)
        return carry

    lax.fori_loop(0, band // NA_UNROLL, body, 0)


def _na_attention(q, k, v, bias, g_na, batch, seq):
    rows = seq // GRID_W
    kh = min(NA_WIN_H, rows)
    hpb = NA_HEADS_PER_BLOCK
    width = hpb * NA_HEAD_DIM
    band = min(NA_BAND_ROWS, rows)
    assert rows % band == 0 and band % NA_UNROLL == 0
    q3, k3, v3 = (x.reshape(batch, seq, D_NA) for x in (q, k, v))
    seq_blk = pl.BlockSpec((1, seq, width), lambda b, j, r: (b, 0, j))
    band_blk = pl.BlockSpec((1, band * GRID_W, width), lambda b, j, r: (b, r, j))
    out = pl.pallas_call(
        functools.partial(_na_kernel, rows=rows, kh=kh, band=band),
        grid=(batch, D_NA // width, rows // band),
        in_specs=[band_blk, seq_blk, seq_blk,
                  pl.BlockSpec((hpb, kh, GRID_W, kh * GRID_W), lambda b, j, r: (j, 0, 0, 0)),
                  pl.BlockSpec((1, width), lambda b, j, r: (0, j))],
        out_specs=band_blk,
        out_shape=jax.ShapeDtypeStruct((batch, seq, D_NA), BF16),
        compiler_params=pltpu.CompilerParams(dimension_semantics=("arbitrary",) * 3,
                                             vmem_limit_bytes=VMEM_LIMIT),
        name="na_attn",
    )(q3, k3, v3, bias, g_na.reshape(1, D_NA))
    return out.reshape(batch * seq, D_NA)


def _mlstm_kernel(q_ref, k_ref, v_ref, gr_ref, gc_ref, o_ref, mf_ref, mb_ref, *, n_chunks):
    L = ML_CHUNK
    hd = pl.program_id(1)
    lane = lax.broadcasted_iota(jnp.int32, (L, LANES), 1)
    grow_i = lax.broadcasted_iota(jnp.int32, (4 * ML_HEADS, L), 0)
    row_i = lax.broadcasted_iota(jnp.int32, (L, L), 0)
    col_i = lax.broadcasted_iota(jnp.int32, (L, L), 1)
    ones = jnp.ones((L, ML_HEAD_DIM), BF16)

    def step(c, m_prev, st_ref, reverse):
        off = pl.multiple_of(c * L, L)
        base = 2 * ML_HEADS if reverse else 0
        qc = q_ref[0, pl.ds(off, L), :]
        kc = k_ref[0, pl.ds(off, L), :]
        vc = v_ref[0, pl.ds(off, L), :]
        gcol = gc_ref[pl.ds(off, L), :]
        grow = gr_ref[:, pl.ds(off, L)]
        i_row = jnp.sum(jnp.where(grow_i == base + hd, grow, 0.0), axis=0, keepdims=True)
        b_row = jnp.sum(jnp.where(grow_i == base + ML_HEADS + hd, grow, 0.0), axis=0, keepdims=True)
        i_col = jnp.sum(jnp.where(lane == base + hd, gcol, 0.0), axis=1, keepdims=True)
        b_col = jnp.sum(jnp.where(lane == base + ML_HEADS + hd, gcol, 0.0), axis=1, keepdims=True)
        gtot = b_row[:, 0:1] if reverse else b_row[:, L - 1:L]
        a_max = jnp.max(gtot - b_row + i_row, axis=1, keepdims=True)
        m_new = jnp.maximum(gtot + m_prev, a_max)
        w_col = jnp.exp(gtot - b_col + i_col - m_new)
        decay = jnp.exp(gtot + m_prev - m_new)
        causal = (col_i >= row_i) if reverse else (col_i <= row_i)
        log_d = jnp.where(causal, b_col - b_row + i_row, -jnp.inf)
        log_inter = b_col + m_prev
        m_out = jnp.maximum(log_inter, jnp.max(log_d, axis=1, keepdims=True))
        s = lax.dot_general(qc, kc, (((1,), (1,)), ((), ())), preferred_element_type=F32) * jnp.exp(log_d - m_out)
        w_inter = jnp.exp(log_inter - m_out)
        vext = jnp.concatenate([vc, ones], axis=1)
        state = st_ref[...]
        nd = (jnp.dot(s.astype(BF16), vext, preferred_element_type=F32)
              + w_inter * jnp.dot(qc, state.astype(BF16), preferred_element_type=F32))
        num = nd[:, :ML_HEAD_DIM]
        den = nd[:, ML_HEAD_DIM:]
        h = num / jnp.maximum(jnp.abs(den), jnp.exp(-m_out))
        wv = (w_col * vext.astype(F32)).astype(BF16)
        st_ref[...] = decay * state + lax.dot_general(kc, wv, (((0,), (0,)), ((), ())), preferred_element_type=F32)
        return h, m_new

    mf_ref[...] = jnp.zeros_like(mf_ref)
    mb_ref[...] = jnp.zeros_like(mb_ref)

    def first_half(j, carry):
        m_f, m_b = carry
        jb = n_chunks - 1 - j
        h_f, m_f = step(j, m_f, mf_ref, False)
        o_ref[0, pl.ds(pl.multiple_of(j * L, L), L), :] = h_f
        h_b, m_b = step(jb, m_b, mb_ref, True)
        o_ref[0, pl.ds(pl.multiple_of(jb * L, L), L), :] = h_b
        return m_f, m_b

    def second_half(j, carry):
        m_f, m_b = carry
        jb = n_chunks - 1 - j
        h_f, m_f = step(j, m_f, mf_ref, False)
        o_ref[0, pl.ds(pl.multiple_of(j * L, L), L), :] += h_f
        h_b, m_b = step(jb, m_b, mb_ref, True)
        o_ref[0, pl.ds(pl.multiple_of(jb * L, L), L), :] += h_b
        return m_f, m_b

    zero = jnp.zeros((1, 1), F32)
    carry = lax.fori_loop(0, n_chunks // 2, first_half, (zero, zero))
    lax.fori_loop(n_chunks // 2, n_chunks, second_half, carry)


def _mlstm(q, k, v, g_row, g_col, batch, seq):
    n_chunks = seq // ML_CHUNK
    assert n_chunks % 2 == 0
    q3, k3, v3 = (x.reshape(batch, seq, D_ML) for x in (q, k, v))
    seq_blk = pl.BlockSpec((1, seq, ML_HEAD_DIM), lambda b, j: (b, 0, j))
    out = pl.pallas_call(
        functools.partial(_mlstm_kernel, n_chunks=n_chunks),
        grid=(batch, ML_HEADS),
        in_specs=[seq_blk, seq_blk, seq_blk,
                  pl.BlockSpec((4 * ML_HEADS, seq), lambda b, j: (0, b)),
                  pl.BlockSpec((seq, LANES), lambda b, j: (b, 0))],
        out_specs=seq_blk,
        out_shape=jax.ShapeDtypeStruct((batch, seq, D_ML), F32),
        scratch_shapes=[pltpu.VMEM((ML_HEAD_DIM, 2 * ML_HEAD_DIM), F32)] * 2,
        compiler_params=pltpu.CompilerParams(dimension_semantics=("arbitrary", "arbitrary"),
                                             vmem_limit_bytes=VMEM_LIMIT),
        name="mlstm",
    )(q3, k3, v3, g_row, g_col)
    return out.reshape(batch * seq, D_ML)


def _mix_out_kernel(h_ref, yna_ref, hml_ref, oml_ref, gml_ref, wo_ref, gmoe_ref, wrh_ref, wrl_ref, br_ref, tri_ref,
                    h1_ref, t_ref, info_ref, cnt_ref):
    i = pl.program_id(0)
    tm = h_ref.shape[0]

    @pl.when(i == 0)
    def _():
        cnt_ref[...] = jnp.zeros_like(cnt_ref)

    hml = hml_ref[...]
    parts = [_rms_scale(hml[:, d * ML_HEAD_DIM:(d + 1) * ML_HEAD_DIM]) for d in range(ML_HEADS)]
    yml = _sigmoid(oml_ref[...].astype(F32)) * (jnp.concatenate(parts, axis=1) * gml_ref[...])
    mix = (jnp.dot(yna_ref[...], wo_ref[0:D_NA, :], preferred_element_type=F32)
           + jnp.dot(yml.astype(BF16), wo_ref[D_NA:D_NA + D_ML, :], preferred_element_type=F32))
    h1 = h_ref[...] + mix
    h1_ref[...] = h1
    t = _rms_scale(h1) * gmoe_ref[...]
    t_ref[...] = t

    t_hi = t.astype(BF16)
    t_lo = (t - t_hi.astype(F32)).astype(BF16)
    logits = (jnp.dot(t_hi, wrh_ref[...], preferred_element_type=F32)
              + jnp.dot(t_lo, wrh_ref[...], preferred_element_type=F32)
              + jnp.dot(t_hi, wrl_ref[...], preferred_element_type=F32)) + br_ref[...]
    lane = lax.broadcasted_iota(jnp.int32, (tm, LANES), 1)
    big = jnp.int32(LANES)

    def top(x):
        mx = jnp.max(x, axis=1, keepdims=True)
        idx = jnp.min(jnp.where(x == mx, lane, big), axis=1, keepdims=True)
        return mx, idx

    lg = jnp.where(lane < N_GROUPS, logits, -jnp.inf)
    g_max, g_idx = top(lg)
    pg_top = 1.0 / jnp.sum(jnp.exp(lg - g_max), axis=1, keepdims=True)
    lo = EXPERT_LANE0 + g_idx * EXPERTS_PER_GROUP
    le = jnp.where((lane >= lo) & (lane < lo + EXPERTS_PER_GROUP), logits, -jnp.inf)
    e1_max, i1 = top(le)
    e_sum = jnp.sum(jnp.exp(le - e1_max), axis=1, keepdims=True)
    e2_max, i2 = top(jnp.where(lane == i1, -jnp.inf, le))
    p1 = 1.0 / e_sum
    p2 = jnp.exp(e2_max - e1_max) / e_sum
    w1 = pg_top * p1 / (p1 + p2)
    w2 = pg_top * p2 / (p1 + p2)
    e1 = i1 - EXPERT_LANE0
    e2 = i2 - EXPERT_LANE0

    oh1 = (lane == e1).astype(F32)
    oh2 = (lane == e2).astype(F32)
    oh = oh1 + oh2
    before = jnp.dot(tri_ref[...], oh.astype(BF16), preferred_element_type=F32) + cnt_ref[...]
    r1 = jnp.sum(oh1 * before, axis=1, keepdims=True)
    r2 = jnp.sum(oh2 * before, axis=1, keepdims=True)
    cnt_ref[...] = cnt_ref[...] + jnp.sum(oh, axis=0, keepdims=True)
    cols = [e1.astype(F32), e2.astype(F32), w1, w2, r1, r2]
    info = jnp.zeros((tm, LANES), F32)
    for j, cval in enumerate(cols):
        info = jnp.where(lane == j, cval, info)
    info_ref[...] = info


def _mix_out(h, yna, hml, oml, g_ml, w_out, g_moe, w_rg, b_rg, w_re, b_re):
    t = h.shape[0]
    tm = ROW_TILE
    wr = jnp.zeros((D_MODEL, LANES), F32)
    wr = wr.at[:, :N_GROUPS].set(w_rg).at[:, EXPERT_LANE0:EXPERT_LANE0 + N_EXPERTS].set(w_re)
    wrh = wr.astype(BF16)
    wrl = (wr - wrh.astype(F32)).astype(BF16)
    br = jnp.zeros((1, LANES), F32)
    br = br.at[0, :N_GROUPS].set(b_rg).at[0, EXPERT_LANE0:EXPERT_LANE0 + N_EXPERTS].set(b_re)
    tri = jnp.asarray(np.tril(np.ones((tm, tm), np.float32), -1), BF16)
    full = lambda shape: pl.BlockSpec(shape, lambda i: (0,) * len(shape))
    row_blk = lambda w: pl.BlockSpec((tm, w), lambda i: (i, 0))
    return pl.pallas_call(
        _mix_out_kernel,
        grid=(t // tm,),
        in_specs=[row_blk(D_MODEL), row_blk(D_NA), row_blk(D_ML), row_blk(D_ML), full((1, D_ML)),
                  full((D_MODEL, D_MODEL)), full((1, D_MODEL)), full((D_MODEL, LANES)), full((D_MODEL, LANES)),
                  full((1, LANES)), full((tm, tm))],
        out_specs=[row_blk(D_MODEL), row_blk(D_MODEL), row_blk(LANES), full((1, LANES))],
        out_shape=[jax.ShapeDtypeStruct((t, D_MODEL), F32), jax.ShapeDtypeStruct((t, D_MODEL), F32),
                   jax.ShapeDtypeStruct((t, LANES), F32), jax.ShapeDtypeStruct((1, LANES), F32)],
        compiler_params=pltpu.CompilerParams(dimension_semantics=("arbitrary",), vmem_limit_bytes=VMEM_LIMIT),
        name="mix_out",
    )(h, yna, hml, oml, g_ml.reshape(1, D_ML), w_out.astype(BF16), g_moe.reshape(1, D_MODEL), wrh, wrl, br, tri)


def _dispatch_kernel(dest_ref, t_ref, xs_in_ref, xs_ref, sem):
    del xs_in_ref
    tm = t_ref.shape[0]

    def row_copy(j, d):
        return pltpu.make_async_copy(t_ref.at[pl.ds(j, 1)], xs_ref.at[pl.ds(d, 1)], sem)

    def issue(j, carry):
        row_copy(j, dest_ref[0, 0, j]).start()
        row_copy(j, dest_ref[0, 0, tm + j]).start()
        return carry

    lax.fori_loop(0, tm, issue, 0)
    for _ in range(2):
        pltpu.make_async_copy(t_ref, xs_ref.at[pl.ds(0, tm)], sem).wait()


def _dispatch(t, dest, n_rows):
    tt = t.shape[0]
    tm = ROW_TILE
    n_tiles = tt // tm
    dest3 = jnp.concatenate([dest[:, 0].reshape(n_tiles, 1, tm), dest[:, 1].reshape(n_tiles, 1, tm)], axis=2)
    xs0 = jnp.zeros((n_rows, D_MODEL), F32)
    return pl.pallas_call(
        _dispatch_kernel,
        grid=(n_tiles,),
        in_specs=[pl.BlockSpec((1, 1, 2 * tm), lambda i: (i, 0, 0), memory_space=pltpu.SMEM),
                  pl.BlockSpec((tm, D_MODEL), lambda i: (i, 0)),
                  pl.BlockSpec(memory_space=pl.ANY)],
        out_specs=pl.BlockSpec(memory_space=pl.ANY),
        out_shape=jax.ShapeDtypeStruct((n_rows, D_MODEL), F32),
        scratch_shapes=[pltpu.SemaphoreType.DMA],
        input_output_aliases={2: 0},
        compiler_params=pltpu.CompilerParams(dimension_semantics=("arbitrary",), vmem_limit_bytes=VMEM_LIMIT),
        name="dispatch",
    )(dest3, t, xs0)


def _ffn_kernel(be_ref, nu_ref, x_ref, wg_ref, wu_ref, wd_ref, y_ref):
    del be_ref

    @pl.when(pl.program_id(0) < nu_ref[0])
    def _():
        x = x_ref[...].astype(BF16)
        g = jnp.dot(x, wg_ref[0], preferred_element_type=F32)
        u = jnp.dot(x, wu_ref[0], preferred_element_type=F32)
        hid = (g * _sigmoid(g) * u).astype(BF16)
        y_ref[...] = jnp.dot(hid, wd_ref[0], preferred_element_type=F32)

    @pl.when(pl.program_id(0) >= nu_ref[0])
    def _():
        y_ref[...] = jnp.zeros_like(y_ref)


def _expert_ffn(xs, block_expert, n_used, w_gate, w_up, w_down):
    n_rows = xs.shape[0]
    nb = n_rows // MOE_BLOCK
    blk = lambda i, be, nu: (jnp.minimum(i, nu[0] - 1), 0)
    wsel = lambda i, be, nu: (be[jnp.minimum(i, nu[0] - 1)], 0, 0)
    return pl.pallas_call(
        _ffn_kernel,
        grid_spec=pltpu.PrefetchScalarGridSpec(
            num_scalar_prefetch=2,
            grid=(nb,),
            in_specs=[pl.BlockSpec((MOE_BLOCK, D_MODEL), blk),
                      pl.BlockSpec((1, D_MODEL, D_EXPERT), wsel),
                      pl.BlockSpec((1, D_MODEL, D_EXPERT), wsel),
                      pl.BlockSpec((1, D_EXPERT, D_MODEL), wsel)],
            out_specs=pl.BlockSpec((MOE_BLOCK, D_MODEL), lambda i, be, nu: (i, 0)),
        ),
        out_shape=jax.ShapeDtypeStruct((n_rows, D_MODEL), F32),
        compiler_params=pltpu.CompilerParams(dimension_semantics=("arbitrary",), vmem_limit_bytes=VMEM_LIMIT),
        name="expert_ffn",
    )(block_expert, n_used, xs, w_gate.astype(BF16), w_up.astype(BF16), w_down.astype(BF16))


def _combine_kernel(dest_ref, h1_ref, info_ref, p_ref, gple_ref, wple_ref, wpg_ref, gfin_ref, yb_ref,
                    o_ref, buf, sem, *, final):
    tm = h1_ref.shape[0]

    def row_copy(d, slot):
        return pltpu.make_async_copy(yb_ref.at[pl.ds(d, 1)], buf.at[pl.ds(slot, 1)], sem)

    def issue(j, carry):
        row_copy(dest_ref[0, 0, j], j).start()
        row_copy(dest_ref[0, 0, tm + j], tm + j).start()
        return carry

    lax.fori_loop(0, tm, issue, 0)
    pltpu.make_async_copy(yb_ref.at[pl.ds(0, 2 * tm)], buf, sem).wait()

    info = info_ref[...]
    moe = info[:, 2:3] * buf[0:tm, :] + info[:, 3:4] * buf[tm:2 * tm, :]
    h2 = h1_ref[...] + moe
    a = (_rms_scale(h2) * gple_ref[...]).astype(BF16)
    gate = _sigmoid(jnp.dot(a, wpg_ref[...], preferred_element_type=F32))
    emb = jnp.dot(p_ref[...].astype(BF16), wple_ref[...], preferred_element_type=F32)
    h3 = h2 + emb * gate
    if final:
        h3 = _rms_scale(h3) * gfin_ref[...]
    o_ref[...] = h3


def _combine(h1, yb, dest, info, p_i, g_ple, w_ple, w_pg, g_final, final):
    tt = h1.shape[0]
    tm = ROW_TILE
    n_tiles = tt // tm
    dest3 = jnp.concatenate([dest[:, 0].reshape(n_tiles, 1, tm), dest[:, 1].reshape(n_tiles, 1, tm)], axis=2)
    full = lambda shape: pl.BlockSpec(shape, lambda i: (0,) * len(shape))
    row_blk = lambda w: pl.BlockSpec((tm, w), lambda i: (i, 0))
    return pl.pallas_call(
        functools.partial(_combine_kernel, final=final),
        grid=(n_tiles,),
        in_specs=[pl.BlockSpec((1, 1, 2 * tm), lambda i: (i, 0, 0), memory_space=pltpu.SMEM),
                  row_blk(D_MODEL), row_blk(LANES), row_blk(D_PLE), full((1, D_MODEL)),
                  full((D_PLE, D_MODEL)), full((D_MODEL, D_MODEL)), full((1, D_MODEL)),
                  pl.BlockSpec(memory_space=pl.ANY)],
        out_specs=row_blk(D_MODEL),
        out_shape=jax.ShapeDtypeStruct((tt, D_MODEL), F32),
        scratch_shapes=[pltpu.VMEM((2 * tm, D_MODEL), F32), pltpu.SemaphoreType.DMA],
        compiler_params=pltpu.CompilerParams(dimension_semantics=("arbitrary",), vmem_limit_bytes=VMEM_LIMIT),
        name="combine",
    )(dest3, h1, info, p_i, g_ple.reshape(1, D_MODEL), w_ple.astype(BF16), w_pg.astype(BF16),
      g_final.reshape(1, D_MODEL), yb)


def _routing_tables(info, counts_f, n_blocks):
    e = info[:, 0:2].astype(jnp.int32)
    rank = info[:, 4:6].astype(jnp.int32)
    counts = counts_f[0, :N_EXPERTS].astype(jnp.int32)
    padded = ((counts + MOE_BLOCK - 1) // MOE_BLOCK) * MOE_BLOCK
    pad_end = jnp.cumsum(padded)
    pad_start = pad_end - padded
    ids = jnp.arange(N_EXPERTS, dtype=jnp.int32)
    dest = jnp.sum(jnp.where(e[..., None] == ids, pad_start, 0), axis=-1) + rank
    n_used = (pad_end[-1] // MOE_BLOCK).astype(jnp.int32).reshape(1)
    starts = jnp.arange(n_blocks, dtype=jnp.int32) * MOE_BLOCK
    block_expert = jnp.minimum(jnp.sum((pad_end[None, :] <= starts[:, None]).astype(jnp.int32), axis=1), N_EXPERTS - 1)
    return dest, block_expert, n_used


def kernel(x, p, w_in, b_gate, conv_w, conv_b, rpb, g_na, g_ml, w_out, g_mix, g_moe, w_route_group, b_route_group, w_route_expert, b_route_expert, w_exp_gate, w_exp_up, w_exp_down, g_ple, w_ple, w_ple_gate, g_final):
    batch, seq, _ = x.shape
    depth = w_in.shape[0]
    tt = batch * seq
    assert seq % ROW_TILE == 0 and seq % GRID_W == 0
    n_blocks = (2 * tt) // MOE_BLOCK + N_EXPERTS
    h = x.reshape(tt, D_MODEL)
    for i in range(depth):
        qna, kna, vna, qml, kml, vml, oml, g_col, g_row = _in_proj(h, seq, g_mix[i], w_in[i], b_gate[i], conv_w[i], conv_b[i])
        yna = _na_attention(qna, kna, vna, _na_bias_table(rpb[i], seq // GRID_W), g_na[i], batch, seq)
        hml = _mlstm(qml, kml, vml, g_row, g_col, batch, seq)
        h1, t, info, counts = _mix_out(h, yna, hml, oml, g_ml[i], w_out[i], g_moe[i],
                                       w_route_group[i], b_route_group[i], w_route_expert[i], b_route_expert[i])
        dest, block_expert, n_used = _routing_tables(info, counts, n_blocks)
        xs = _dispatch(t, dest, n_blocks * MOE_BLOCK)
        yb = _expert_ffn(xs, block_expert, n_used, w_exp_gate[i], w_exp_up[i], w_exp_down[i])
        h = _combine(h1, yb, dest, info, p[i].reshape(tt, D_PLE), g_ple[i], w_ple[i], w_ple_gate[i], g_final,
                     final=(i == depth - 1))
    return h.reshape(batch, seq, D_MODEL)
```

```python
import functools

import numpy as np
import jax
import jax.numpy as jnp
from jax import lax
from jax.experimental import pallas as pl
from jax.experimental.pallas import tpu as pltpu

F32 = jnp.float32
BF16 = jnp.bfloat16

D_MODEL = 1024
GRID_W = 64
D_NA = 512
NA_HEADS = 8
NA_HEAD_DIM = 64
NA_WIN_H = 8
NA_WIN_W = 16
D_ML = 512
ML_HEADS = 4
ML_HEAD_DIM = 128
ML_CONV = 5
ML_CHUNK = 128
N_GROUPS = 4
EXPERTS_PER_GROUP = 8
N_EXPERTS = 32
D_EXPERT = 512
D_PLE = 256
EPS = 1e-6
NEG_INF = -1e30

LANES = 128
SUBLANES = 8
ROW_TILE = 512
MOE_BLOCK = 256
PAIRS_PER_GROUP = EXPERTS_PER_GROUP * (EXPERTS_PER_GROUP - 1) // 2
N_CLASSES = N_GROUPS * PAIRS_PER_GROUP
ROUTED_WIDTH = D_MODEL + LANES
INFO_W_LO = D_MODEL + 2
INFO_W_HI = D_MODEL + 3
HALO = SUBLANES
EXPERT_LANE0 = 8
GATE_LANES_PER_DIR = 3 * ML_HEADS
VMEM_LIMIT = 56 * 1024 * 1024


def _sigmoid(x):
    return 1.0 / (1.0 + jnp.exp(-x))


def _log_sigmoid(x):
    return jnp.minimum(x, 0.0) - jnp.log1p(jnp.exp(-jnp.abs(x)))


def _rms_scale(x):
    return x * lax.rsqrt(jnp.mean(x * x, axis=-1, keepdims=True) + EPS)


def _scan_chunk(x, axis, reverse, op, identity):
    n = x.shape[axis]
    idx = lax.broadcasted_iota(jnp.int32, x.shape, axis)
    sh = 1
    while sh < n:
        if reverse:
            x = op(x, jnp.where(idx < n - sh, pltpu.roll(x, n - sh, axis), identity))
        else:
            x = op(x, jnp.where(idx >= sh, pltpu.roll(x, sh, axis), identity))
        sh *= 2
    return x


def _in_proj_kernel(hp_ref, h_ref, hn_ref, g_ref, wna_ref, wqk_ref, wvo_ref, wgc_ref, wgr_ref,
                    bgc_ref, bgr_ref, cw_ref, cb_ref,
                    qna_ref, kna_ref, vna_ref, qml_ref, kt_ref, vml_ref, oml_ref, gc_ref, gr_ref,
                    zbuf, *, tiles_per_seq):
    i = pl.program_id(0)
    tm = h_ref.shape[0]
    g = g_ref[...]

    def norm(x):
        return (_rms_scale(x) * g).astype(BF16)

    a = norm(h_ref[...])
    qna_ref[...] = (jnp.dot(a, wna_ref[:, 0:D_NA], preferred_element_type=F32) * (NA_HEAD_DIM ** -0.5)).astype(BF16)
    kna_ref[...] = jnp.dot(a, wna_ref[:, D_NA:2 * D_NA], preferred_element_type=F32).astype(BF16)
    vna_ref[...] = jnp.dot(a, wna_ref[:, 2 * D_NA:3 * D_NA], preferred_element_type=F32).astype(BF16)
    vml_ref[...] = jnp.dot(a, wvo_ref[:, 0:D_ML], preferred_element_type=F32).astype(BF16)
    oml_ref[...] = jnp.dot(a, wvo_ref[:, D_ML:2 * D_ML], preferred_element_type=F32).astype(BF16)

    pos = i % tiles_per_seq
    zp = jnp.dot(norm(hp_ref[...]), wqk_ref[...], preferred_element_type=F32)
    zn = jnp.dot(norm(hn_ref[...]), wqk_ref[...], preferred_element_type=F32)
    zbuf[0:HALO, :] = jnp.where(pos == 0, 0.0, zp)
    zbuf[HALO:HALO + tm, :] = jnp.dot(a, wqk_ref[...], preferred_element_type=F32)
    zbuf[HALO + tm:2 * HALO + tm, :] = jnp.where(pos == tiles_per_seq - 1, 0.0, zn)
    cstep = 256
    for c0 in range(0, 2 * D_ML, cstep):
        acc = cb_ref[:, c0:c0 + cstep]
        for j in range(ML_CONV):
            r0 = HALO - ML_CONV // 2 + j
            acc = acc + zbuf[r0:r0 + tm, c0:c0 + cstep] * cw_ref[j:j + 1, c0:c0 + cstep]
        y = acc * _sigmoid(acc)
        if c0 < D_ML:
            qml_ref[:, c0:c0 + cstep] = (y * (ML_HEAD_DIM ** -0.5)).astype(BF16)
        else:
            kt_ref[c0 - D_ML:c0 - D_ML + cstep, :] = y.T.astype(BF16)

    gi = jnp.dot(a, wgc_ref[:, 0:LANES], preferred_element_type=F32) + bgc_ref[:, 0:LANES]
    gf = jnp.dot(a, wgc_ref[:, LANES:2 * LANES], preferred_element_type=F32) + bgc_ref[:, LANES:2 * LANES]
    lane = lax.broadcasted_iota(jnp.int32, (ML_CHUNK, LANES), 1)
    is_fw = lane < GATE_LANES_PER_DIR
    qsel = jnp.where(is_fw, lane, lane - GATE_LANES_PER_DIR)
    for c in range(tm // ML_CHUNK):
        rows_c = slice(c * ML_CHUNK, (c + 1) * ML_CHUNK)
        lf = _log_sigmoid(gf[rows_c, :])
        b = jnp.where(is_fw, _scan_chunk(lf, 0, False, jnp.add, 0.0), _scan_chunk(lf, 0, True, jnp.add, 0.0))
        r = gi[rows_c, :] - b
        cmax = jnp.where(is_fw, _scan_chunk(r, 0, False, jnp.maximum, -jnp.inf),
                         _scan_chunk(r, 0, True, jnp.maximum, -jnp.inf))
        gc_ref[rows_c, :] = jnp.where(qsel < ML_HEADS, r, jnp.where(qsel < 2 * ML_HEADS, cmax, b))
    nt = (((1,), (1,)), ((), ()))
    gi_r = lax.dot_general(wgr_ref[0:2 * ML_HEADS, :], a, nt, preferred_element_type=F32) + bgr_ref[0:2 * ML_HEADS, :]
    gf_r = lax.dot_general(wgr_ref[2 * ML_HEADS:4 * ML_HEADS, :], a, nt, preferred_element_type=F32) + bgr_ref[2 * ML_HEADS:4 * ML_HEADS, :]
    row_fw = lax.broadcasted_iota(jnp.int32, (2 * ML_HEADS, ML_CHUNK), 0) < ML_HEADS
    for c in range(tm // ML_CHUNK):
        cols_c = slice(c * ML_CHUNK, (c + 1) * ML_CHUNK)
        lf = _log_sigmoid(gf_r[:, cols_c])
        b = jnp.where(row_fw, _scan_chunk(lf, 1, False, jnp.add, 0.0), _scan_chunk(lf, 1, True, jnp.add, 0.0))
        gr_ref[:, cols_c] = gi_r[:, cols_c] - b


def _in_proj(h, seq, g_mix, w_in, b_gate, conv_w, conv_b):
    t = h.shape[0]
    tm = ROW_TILE
    n_tiles = t // tm
    tiles_per_seq = seq // tm
    nh = 4 * ML_HEADS
    o_qk = 3 * D_NA
    o_v = o_qk + 2 * D_ML
    o_g = o_v + 2 * D_ML
    wna = w_in[:, :o_qk].astype(BF16)
    wqk = w_in[:, o_qk:o_v].astype(BF16)
    wvo = w_in[:, o_v:o_g].astype(BF16)
    wg = w_in[:, o_g:o_g + nh]
    col_lane = np.arange(2 * GATE_LANES_PER_DIR)
    col_d, col_h = col_lane // GATE_LANES_PER_DIR, col_lane % ML_HEADS
    src_i_col = col_d * 2 * ML_HEADS + col_h
    row_id = np.arange(2 * ML_HEADS)
    src_i_row = (row_id // ML_HEADS) * 2 * ML_HEADS + row_id % ML_HEADS
    pad_lanes = lambda x: jnp.pad(x, ((0, 0), (0, LANES - x.shape[1])))
    wgc = jnp.concatenate([pad_lanes(wg[:, src_i_col]), pad_lanes(wg[:, src_i_col + ML_HEADS])], axis=1).astype(BF16)
    bgc = jnp.concatenate([pad_lanes(b_gate[None, src_i_col]), pad_lanes(b_gate[None, src_i_col + ML_HEADS])], axis=1)
    wgr = jnp.concatenate([wg[:, src_i_row], wg[:, src_i_row + ML_HEADS]], axis=1).T.astype(BF16)
    bgr = jnp.concatenate([b_gate[src_i_row], b_gate[src_i_row + ML_HEADS]]).reshape(nh, 1)
    cw = jnp.pad(conv_w, ((0, SUBLANES - ML_CONV), (0, 0)))
    cb = conv_b.reshape(1, 2 * D_ML)
    full = lambda shape: pl.BlockSpec(shape, lambda i: (0,) * len(shape))
    row_blk = lambda w: pl.BlockSpec((tm, w), lambda i: (i, 0))
    hb = tm // HALO
    out_bf = jax.ShapeDtypeStruct((t, D_NA), BF16)
    return pl.pallas_call(
        functools.partial(_in_proj_kernel, tiles_per_seq=tiles_per_seq),
        grid=(n_tiles,),
        in_specs=[
            pl.BlockSpec((HALO, D_MODEL), lambda i: (jnp.maximum(i * hb - 1, 0), 0)),
            row_blk(D_MODEL),
            pl.BlockSpec((HALO, D_MODEL), lambda i: (jnp.minimum((i + 1) * hb, t // HALO - 1), 0)),
            full((1, D_MODEL)), full((D_MODEL, o_qk)), full((D_MODEL, 2 * D_ML)), full((D_MODEL, 2 * D_ML)),
            full((D_MODEL, 2 * LANES)), full((nh, D_MODEL)), full((1, 2 * LANES)), full((nh, 1)),
            full((SUBLANES, 2 * D_ML)), full((1, 2 * D_ML)),
        ],
        out_specs=[row_blk(D_NA)] * 4 + [pl.BlockSpec((D_ML, tm), lambda i: (0, i))] + [row_blk(D_NA)] * 2
        + [row_blk(LANES), pl.BlockSpec((2 * ML_HEADS, tm), lambda i: (0, i))],
        out_shape=[out_bf] * 4 + [jax.ShapeDtypeStruct((D_ML, t), BF16)] + [out_bf] * 2
        + [jax.ShapeDtypeStruct((t, LANES), F32), jax.ShapeDtypeStruct((2 * ML_HEADS, t), F32)],
        scratch_shapes=[pltpu.VMEM((tm + 2 * HALO, 2 * D_ML), F32)],
        compiler_params=pltpu.CompilerParams(dimension_semantics=("arbitrary",), vmem_limit_bytes=VMEM_LIMIT),
        name="in_proj",
    )(h, h, h, g_mix.reshape(1, D_MODEL), wna, wqk, wvo, wgc, wgr, bgc, bgr, cw, cb)


def _na_bias_table(rpb, rows):
    kh = min(NA_WIN_H, rows)
    o = np.arange(kh)[:, None]
    kr = np.arange(kh)[None, :]
    dr = kr - o + (NA_WIN_H - 1)
    qc = np.arange(GRID_W)[:, None]
    kc = np.arange(GRID_W)[None, :]
    cs = np.clip(qc - NA_WIN_W // 2, 0, GRID_W - NA_WIN_W)
    inw = (kc >= cs) & (kc < cs + NA_WIN_W)
    dc = np.clip(kc - qc, 1 - NA_WIN_W, NA_WIN_W - 1) + (NA_WIN_W - 1)
    sel_r = jnp.asarray(dr[:, :, None] == np.arange(2 * NA_WIN_H - 1), F32)
    sel_c = jnp.asarray(dc[:, :, None] == np.arange(2 * NA_WIN_W - 1), F32)
    rows_sel = jnp.einsum("hrc,okr->hokc", rpb.astype(F32), sel_r, precision=lax.Precision.HIGHEST)
    b = jnp.einsum("hokc,qlc->hoqkl", rows_sel, sel_c, precision=lax.Precision.HIGHEST)
    b = jnp.where(inw[None, None, :, None, :], b, NEG_INF)
    return b.reshape(rpb.shape[0], kh, GRID_W, kh * GRID_W)


NA_HEADS_PER_BLOCK = 4
NA_BAND_ROWS = 16
NA_UNROLL = 2


def _na_kernel(q_ref, k_ref, v_ref, bias_ref, g_ref, o_ref, *, rows, kh, band):
    nkeys = kh * GRID_W
    hpb = NA_HEADS_PER_BLOCK
    m_rows = hpb * GRID_W
    width = hpb * NA_HEAD_DIM
    band_i = pl.program_id(2)
    lane_head = lax.shift_right_logical(lax.broadcasted_iota(jnp.int32, (m_rows, width), 1), 6)
    row_head = lax.shift_right_logical(lax.broadcasted_iota(jnp.int32, (m_rows, width), 0), 6)
    own = lane_head == row_head
    g = g_ref[...]

    def one_row(rl):
        r = band_i * band + rl
        rs = jnp.clip(r - kh // 2, 0, rows - kh)
        off = r - rs
        q = q_ref[0, pl.ds(pl.multiple_of(rl * GRID_W, GRID_W), GRID_W), :]
        kb = k_ref[0, pl.ds(pl.multiple_of(rs * GRID_W, GRID_W), nkeys), :]
        vb = v_ref[0, pl.ds(pl.multiple_of(rs * GRID_W, GRID_W), nkeys), :]
        qs = jnp.where(own, jnp.concatenate([q] * hpb, axis=0), jnp.zeros((m_rows, width), BF16))
        s = lax.dot_general(qs, kb, (((1,), (1,)), ((), ())), preferred_element_type=F32)
        s = s + bias_ref[:, off].reshape(m_rows, nkeys)
        m = jnp.max(s, axis=-1, keepdims=True)
        p = jnp.exp(s - m)
        l = jnp.sum(p, axis=-1, keepdims=True)
        pv = jnp.dot(p.astype(BF16), vb, preferred_element_type=F32)
        oh = jnp.where(own, pv * (1.0 / l), 0.0)
        ms = jnp.sum(oh * oh, axis=-1, keepdims=True) * (1.0 / NA_HEAD_DIM)
        yn = oh * lax.rsqrt(ms + EPS)
        y = yn[0:GRID_W]
        for hh in range(1, hpb):
            y = y + yn[hh * GRID_W:(hh + 1) * GRID_W]
        o_ref[0, pl.ds(pl.multiple_of(rl * GRID_W, GRID_W), GRID_W), :] = (y * g).astype(BF16)

    def body(it, carry):
        for u in range(NA_UNROLL):
            one_row(it * NA_UNROLL + u)
        return carry

    lax.fori_loop(0, band // NA_UNROLL, body, 0)


def _na_attention(q, k, v, bias, g_na, batch, seq):
    rows = seq // GRID_W
    kh = min(NA_WIN_H, rows)
    hpb = NA_HEADS_PER_BLOCK
    width = hpb * NA_HEAD_DIM
    band = min(NA_BAND_ROWS, rows)
    assert rows % band == 0 and band % NA_UNROLL == 0
    q3, k3, v3 = (x.reshape(batch, seq, D_NA) for x in (q, k, v))
    seq_blk = pl.BlockSpec((1, seq, width), lambda b, j, r: (b, 0, j))
    band_blk = pl.BlockSpec((1, band * GRID_W, width), lambda b, j, r: (b, r, j))
    out = pl.pallas_call(
        functools.partial(_na_kernel, rows=rows, kh=kh, band=band),
        grid=(batch, D_NA // width, rows // band),
        in_specs=[band_blk, seq_blk, seq_blk,
                  pl.BlockSpec((hpb, kh, GRID_W, kh * GRID_W), lambda b, j, r: (j, 0, 0, 0)),
                  pl.BlockSpec((1, width), lambda b, j, r: (0, j))],
        out_specs=band_blk,
        out_shape=jax.ShapeDtypeStruct((batch, seq, D_NA), BF16),
        compiler_params=pltpu.CompilerParams(dimension_semantics=("arbitrary",) * 3,
                                             vmem_limit_bytes=VMEM_LIMIT),
        name="na_attn",
    )(q3, k3, v3, bias, g_na.reshape(1, D_NA))
    return out.reshape(batch * seq, D_NA)


def _mlstm_kernel(q_ref, kt_ref, v_ref, gr_ref, gc_ref, o_ref, mf_ref, mb_ref, *, n_chunks):
    L = ML_CHUNK
    hd = pl.program_id(1)
    lane = lax.broadcasted_iota(jnp.int32, (L, LANES), 1)
    grow_i = lax.broadcasted_iota(jnp.int32, (2 * ML_HEADS, L), 0)
    row_i = lax.broadcasted_iota(jnp.int32, (L, L), 0)
    col_i = lax.broadcasted_iota(jnp.int32, (L, L), 1)
    ones = jnp.ones((L, ML_HEAD_DIM), BF16)

    def step(c, m_prev, st_ref, reverse):
        off = pl.multiple_of(c * L, L)
        d = 1 if reverse else 0
        qc = q_ref[0, pl.ds(off, L), :]
        kt = kt_ref[:, pl.ds(off, L)]
        vc = v_ref[0, pl.ds(off, L), :]
        gcol = gc_ref[pl.ds(off, L), :]
        grow = gr_ref[:, pl.ds(off, L)]

        def col(qn):
            sel = lane == d * GATE_LANES_PER_DIR + qn * ML_HEADS + hd
            return jnp.sum(jnp.where(sel, gcol, 0.0), axis=1, keepdims=True)

        r_col, cmax_col, b_col = col(0), col(1), col(2)
        r_row = jnp.sum(jnp.where(grow_i == d * ML_HEADS + hd, grow, 0.0), axis=0, keepdims=True)
        end = 0 if reverse else L - 1
        mm_end = jnp.maximum(m_prev, cmax_col[end:end + 1, :])
        m_new = b_col[end:end + 1, :] + mm_end
        mm_col = jnp.maximum(m_prev, cmax_col)
        causal = (col_i >= row_i) if reverse else (col_i <= row_i)
        dmat = jnp.exp(jnp.where(causal, r_row - mm_col, -jnp.inf))
        s = jnp.dot(qc, kt, preferred_element_type=F32) * dmat
        w_inter = jnp.exp(m_prev - mm_col)
        floor = jnp.exp(-(b_col + mm_col))
        w_col = jnp.exp(r_col - mm_end)
        decay = jnp.exp(m_prev - mm_end)
        vext = jnp.concatenate([vc, ones], axis=1)
        state = st_ref[...]
        nd = (jnp.dot(s.astype(BF16), vext, preferred_element_type=F32)
              + w_inter * jnp.dot(qc, state.astype(BF16), preferred_element_type=F32))
        h = nd[:, :ML_HEAD_DIM] / jnp.maximum(jnp.abs(nd[:, ML_HEAD_DIM:]), floor)
        wv = jnp.concatenate([(w_col * vc.astype(F32)).astype(BF16),
                              jnp.broadcast_to(w_col, (L, ML_HEAD_DIM)).astype(BF16)], axis=1)
        st_ref[...] = decay * state + jnp.dot(kt, wv, preferred_element_type=F32)
        return h, m_new

    mf_ref[...] = jnp.zeros_like(mf_ref)
    mb_ref[...] = jnp.zeros_like(mb_ref)

    def first_half(j, carry):
        m_f, m_b = carry
        jb = n_chunks - 1 - j
        h_f, m_f = step(j, m_f, mf_ref, False)
        o_ref[0, pl.ds(pl.multiple_of(j * L, L), L), :] = h_f
        h_b, m_b = step(jb, m_b, mb_ref, True)
        o_ref[0, pl.ds(pl.multiple_of(jb * L, L), L), :] = h_b
        return m_f, m_b

    def second_half(j, carry):
        m_f, m_b = carry
        jb = n_chunks - 1 - j
        h_f, m_f = step(j, m_f, mf_ref, False)
        o_ref[0, pl.ds(pl.multiple_of(j * L, L), L), :] += h_f
        h_b, m_b = step(jb, m_b, mb_ref, True)
        o_ref[0, pl.ds(pl.multiple_of(jb * L, L), L), :] += h_b
        return m_f, m_b

    zero = jnp.zeros((1, 1), F32)
    carry = lax.fori_loop(0, n_chunks // 2, first_half, (zero, zero))
    lax.fori_loop(n_chunks // 2, n_chunks, second_half, carry)


def _mlstm(q, kt, v, g_row, g_col, batch, seq):
    n_chunks = seq // ML_CHUNK
    assert n_chunks % 2 == 0
    q3, v3 = (x.reshape(batch, seq, D_ML) for x in (q, v))
    seq_blk = pl.BlockSpec((1, seq, ML_HEAD_DIM), lambda b, j: (b, 0, j))
    out = pl.pallas_call(
        functools.partial(_mlstm_kernel, n_chunks=n_chunks),
        grid=(batch, ML_HEADS),
        in_specs=[seq_blk, pl.BlockSpec((ML_HEAD_DIM, seq), lambda b, j: (j, b)), seq_blk,
                  pl.BlockSpec((2 * ML_HEADS, seq), lambda b, j: (0, b)),
                  pl.BlockSpec((seq, LANES), lambda b, j: (b, 0))],
        out_specs=seq_blk,
        out_shape=jax.ShapeDtypeStruct((batch, seq, D_ML), F32),
        scratch_shapes=[pltpu.VMEM((ML_HEAD_DIM, 2 * ML_HEAD_DIM), F32)] * 2,
        compiler_params=pltpu.CompilerParams(dimension_semantics=("arbitrary", "arbitrary"),
                                             vmem_limit_bytes=VMEM_LIMIT),
        name="mlstm",
    )(q3, kt, v3, g_row, g_col)
    return out.reshape(batch * seq, D_ML)


def _mix_out_kernel(h_ref, yna_ref, hml_ref, oml_ref, gml_ref, wo_ref, gmoe_ref, wrh_ref, wrl_ref, br_ref, tri_ref,
                    h1_ref, t_ref, cnt_ref):
    i = pl.program_id(0)
    tm = h_ref.shape[0]

    @pl.when(i == 0)
    def _():
        cnt_ref[...] = jnp.zeros_like(cnt_ref)

    hml = hml_ref[...]
    parts = [_rms_scale(hml[:, d * ML_HEAD_DIM:(d + 1) * ML_HEAD_DIM]) for d in range(ML_HEADS)]
    yml = _sigmoid(oml_ref[...].astype(F32)) * (jnp.concatenate(parts, axis=1) * gml_ref[...])
    mix = (jnp.dot(yna_ref[...], wo_ref[0:D_NA, :], preferred_element_type=F32)
           + jnp.dot(yml.astype(BF16), wo_ref[D_NA:D_NA + D_ML, :], preferred_element_type=F32))
    h1 = h_ref[...] + mix
    h1_ref[...] = h1
    t = _rms_scale(h1) * gmoe_ref[...]
    t_ref[:, 0:D_MODEL] = t

    t_hi = t.astype(BF16)
    t_lo = (t - t_hi.astype(F32)).astype(BF16)
    logits = (jnp.dot(t_hi, wrh_ref[...], preferred_element_type=F32)
              + jnp.dot(t_lo, wrh_ref[...], preferred_element_type=F32)
              + jnp.dot(t_hi, wrl_ref[...], preferred_element_type=F32)) + br_ref[...]
    lane = lax.broadcasted_iota(jnp.int32, (tm, LANES), 1)
    big = jnp.int32(LANES)

    def top(x):
        mx = jnp.max(x, axis=1, keepdims=True)
        idx = jnp.min(jnp.where(x == mx, lane, big), axis=1, keepdims=True)
        return mx, idx

    lg = jnp.where(lane < N_GROUPS, logits, -jnp.inf)
    g_max, g_idx = top(lg)
    pg_top = 1.0 / jnp.sum(jnp.exp(lg - g_max), axis=1, keepdims=True)
    lo = EXPERT_LANE0 + g_idx * EXPERTS_PER_GROUP
    le = jnp.where((lane >= lo) & (lane < lo + EXPERTS_PER_GROUP), logits, -jnp.inf)
    e1_max, i1 = top(le)
    e_sum = jnp.sum(jnp.exp(le - e1_max), axis=1, keepdims=True)
    e2_max, i2 = top(jnp.where(lane == i1, -jnp.inf, le))
    p1 = 1.0 / e_sum
    p2 = jnp.exp(e2_max - e1_max) / e_sum
    w1 = pg_top * p1 / (p1 + p2)
    w2 = pg_top * p2 / (p1 + p2)
    e1 = i1 - EXPERT_LANE0
    e2 = i2 - EXPERT_LANE0

    la = jnp.minimum(e1, e2) - g_idx * EXPERTS_PER_GROUP
    lb = jnp.maximum(e1, e2) - g_idx * EXPERTS_PER_GROUP
    cls = g_idx * PAIRS_PER_GROUP + lax.shift_right_logical(la * (2 * EXPERTS_PER_GROUP - 1 - la), 1) + (lb - la - 1)
    w_lo = jnp.where(e1 < e2, w1, w2)
    w_hi = jnp.where(e1 < e2, w2, w1)
    oh = (lane == cls).astype(F32)
    before = jnp.dot(tri_ref[...], oh.astype(BF16), preferred_element_type=F32) + cnt_ref[...]
    rank = jnp.sum(oh * before, axis=1, keepdims=True)
    cnt_ref[...] = cnt_ref[...] + jnp.sum(oh, axis=0, keepdims=True)
    info = jnp.zeros((tm, LANES), F32)
    for j, cval in enumerate([cls.astype(F32), rank, w_lo, w_hi]):
        info = jnp.where(lane == j, cval, info)
    t_ref[:, D_MODEL:D_MODEL + LANES] = info


def _mix_out(h, yna, hml, oml, g_ml, w_out, g_moe, w_rg, b_rg, w_re, b_re):
    t = h.shape[0]
    tm = ROW_TILE
    wr = jnp.zeros((D_MODEL, LANES), F32)
    wr = wr.at[:, :N_GROUPS].set(w_rg).at[:, EXPERT_LANE0:EXPERT_LANE0 + N_EXPERTS].set(w_re)
    wrh = wr.astype(BF16)
    wrl = (wr - wrh.astype(F32)).astype(BF16)
    br = jnp.zeros((1, LANES), F32)
    br = br.at[0, :N_GROUPS].set(b_rg).at[0, EXPERT_LANE0:EXPERT_LANE0 + N_EXPERTS].set(b_re)
    tri = jnp.asarray(np.tril(np.ones((tm, tm), np.float32), -1), BF16)
    full = lambda shape: pl.BlockSpec(shape, lambda i: (0,) * len(shape))
    row_blk = lambda w: pl.BlockSpec((tm, w), lambda i: (i, 0))
    return pl.pallas_call(
        _mix_out_kernel,
        grid=(t // tm,),
        in_specs=[row_blk(D_MODEL), row_blk(D_NA), row_blk(D_ML), row_blk(D_ML), full((1, D_ML)),
                  full((D_MODEL, D_MODEL)), full((1, D_MODEL)), full((D_MODEL, LANES)), full((D_MODEL, LANES)),
                  full((1, LANES)), full((tm, tm))],
        out_specs=[row_blk(D_MODEL), row_blk(ROUTED_WIDTH), full((1, LANES))],
        out_shape=[jax.ShapeDtypeStruct((t, D_MODEL), F32), jax.ShapeDtypeStruct((t, ROUTED_WIDTH), F32),
                   jax.ShapeDtypeStruct((1, LANES), F32)],
        compiler_params=pltpu.CompilerParams(dimension_semantics=("arbitrary",), vmem_limit_bytes=VMEM_LIMIT),
        name="mix_out",
    )(h, yna, hml, oml, g_ml.reshape(1, D_ML), w_out.astype(BF16), g_moe.reshape(1, D_MODEL), wrh, wrl, br, tri)


def _dispatch_kernel(zflag_ref, dest_ref, t_ref, xs_ref, zbuf, sem, zsem, *, n_blocks):
    tm = t_ref.shape[0]

    @pl.when(pl.program_id(0) == 0)
    def _():
        zbuf[...] = jnp.zeros_like(zbuf)

        def zero_copy(b):
            return pltpu.make_async_copy(zbuf, xs_ref.at[pl.ds(pl.multiple_of(b * MOE_BLOCK, MOE_BLOCK), MOE_BLOCK)], zsem)

        def zissue(b, carry):
            @pl.when(zflag_ref[b] != 0)
            def _():
                zero_copy(b).start()
            return carry

        def zwait(b, carry):
            @pl.when(zflag_ref[b] != 0)
            def _():
                zero_copy(b).wait()
            return carry

        lax.fori_loop(0, n_blocks, zissue, 0)
        lax.fori_loop(0, n_blocks, zwait, 0)

    def issue(j, carry):
        pltpu.make_async_copy(t_ref.at[pl.ds(j, 1)], xs_ref.at[pl.ds(dest_ref[0, 0, j], 1)], sem).start()
        return carry

    lax.fori_loop(0, tm, issue, 0)
    pltpu.make_async_copy(t_ref, xs_ref.at[pl.ds(0, tm)], sem).wait()


def _dispatch(t, dest, zflag, n_blocks):
    tt = t.shape[0]
    tm = ROW_TILE
    n_tiles = tt // tm
    return pl.pallas_call(
        functools.partial(_dispatch_kernel, n_blocks=n_blocks),
        grid_spec=pltpu.PrefetchScalarGridSpec(
            num_scalar_prefetch=1,
            grid=(n_tiles,),
            in_specs=[pl.BlockSpec((1, 1, tm), lambda i, zf: (i, 0, 0), memory_space=pltpu.SMEM),
                      pl.BlockSpec((tm, ROUTED_WIDTH), lambda i, zf: (i, 0))],
            out_specs=pl.BlockSpec(memory_space=pl.ANY),
            scratch_shapes=[pltpu.VMEM((MOE_BLOCK, ROUTED_WIDTH), F32), pltpu.SemaphoreType.DMA,
                            pltpu.SemaphoreType.DMA],
        ),
        out_shape=jax.ShapeDtypeStruct((n_blocks * MOE_BLOCK, ROUTED_WIDTH), F32),
        compiler_params=pltpu.CompilerParams(dimension_semantics=("arbitrary",), vmem_limit_bytes=VMEM_LIMIT),
        name="dispatch",
    )(zflag, dest.reshape(n_tiles, 1, tm), t)


def _ffn_kernel(ea_ref, eb_ref, nu_ref, x_ref, wga_ref, wua_ref, wda_ref, wgb_ref, wub_ref, wdb_ref, y_ref):
    del ea_ref, eb_ref

    def swiglu(x, wg_ref, wu_ref, wd_ref):
        g = jnp.dot(x, wg_ref[0], preferred_element_type=F32)
        u = jnp.dot(x, wu_ref[0], preferred_element_type=F32)
        hid = (g * _sigmoid(g) * u).astype(BF16)
        return jnp.dot(hid, wd_ref[0], preferred_element_type=F32)

    @pl.when(pl.program_id(0) < nu_ref[0])
    def _():
        x = x_ref[:, 0:D_MODEL].astype(BF16)
        y_ref[...] = (x_ref[:, INFO_W_LO:INFO_W_LO + 1] * swiglu(x, wga_ref, wua_ref, wda_ref)
                      + x_ref[:, INFO_W_HI:INFO_W_HI + 1] * swiglu(x, wgb_ref, wub_ref, wdb_ref))

    @pl.when(pl.program_id(0) >= nu_ref[0])
    def _():
        y_ref[...] = jnp.zeros_like(y_ref)


def _expert_ffn(xs, block_ea, block_eb, n_used, w_gate, w_up, w_down):
    n_rows = xs.shape[0]
    nb = n_rows // MOE_BLOCK
    wg, wu, wd = w_gate.astype(BF16), w_up.astype(BF16), w_down.astype(BF16)
    blk = lambda i, ea, eb, nu: (jnp.minimum(i, nu[0] - 1), 0)
    sel_a = lambda i, ea, eb, nu: (ea[jnp.minimum(i, nu[0] - 1)], 0, 0)
    sel_b = lambda i, ea, eb, nu: (eb[jnp.minimum(i, nu[0] - 1)], 0, 0)
    up_spec = lambda sel: pl.BlockSpec((1, D_MODEL, D_EXPERT), sel)
    down_spec = lambda sel: pl.BlockSpec((1, D_EXPERT, D_MODEL), sel)
    return pl.pallas_call(
        _ffn_kernel,
        grid_spec=pltpu.PrefetchScalarGridSpec(
            num_scalar_prefetch=3,
            grid=(nb,),
            in_specs=[pl.BlockSpec((MOE_BLOCK, ROUTED_WIDTH), blk),
                      up_spec(sel_a), up_spec(sel_a), down_spec(sel_a),
                      up_spec(sel_b), up_spec(sel_b), down_spec(sel_b)],
            out_specs=pl.BlockSpec((MOE_BLOCK, D_MODEL), lambda i, ea, eb, nu: (i, 0)),
        ),
        out_shape=jax.ShapeDtypeStruct((n_rows, D_MODEL), F32),
        compiler_params=pltpu.CompilerParams(dimension_semantics=("arbitrary",), vmem_limit_bytes=VMEM_LIMIT),
        name="expert_ffn",
    )(block_ea, block_eb, n_used, xs, wg, wu, wd, wg, wu, wd)


def _combine_kernel(dcur_ref, dnxt_ref, h1_ref, p_ref, gple_ref, wple_ref, wpg_ref, gfin_ref, ys_ref,
                    o_ref, buf, sem, *, final, n_tiles):
    i = pl.program_id(0)
    tm = h1_ref.shape[0]
    slot = i % 2

    def issue_tile(d_ref, s):
        def issue(j, carry):
            pltpu.make_async_copy(ys_ref.at[pl.ds(d_ref[0, 0, j], 1)], buf.at[s, pl.ds(j, 1)], sem.at[s]).start()
            return carry

        lax.fori_loop(0, tm, issue, 0)

    @pl.when(i == 0)
    def _():
        issue_tile(dcur_ref, 0)

    @pl.when(i + 1 < n_tiles)
    def _():
        issue_tile(dnxt_ref, 1 - slot)

    pltpu.make_async_copy(ys_ref.at[pl.ds(0, tm)], buf.at[slot], sem.at[slot]).wait()

    h2 = h1_ref[...] + buf[slot]
    a = (_rms_scale(h2) * gple_ref[...]).astype(BF16)
    gate = _sigmoid(jnp.dot(a, wpg_ref[...], preferred_element_type=F32))
    emb = jnp.dot(p_ref[...].astype(BF16), wple_ref[...], preferred_element_type=F32)
    h3 = h2 + emb * gate
    if final:
        h3 = _rms_scale(h3) * gfin_ref[...]
    o_ref[...] = h3


def _combine(h1, ys, dest, p_i, g_ple, w_ple, w_pg, g_final, final):
    tt = h1.shape[0]
    tm = ROW_TILE
    n_tiles = tt // tm
    dest3 = dest.reshape(n_tiles, 1, tm)
    full = lambda shape: pl.BlockSpec(shape, lambda i: (0,) * len(shape))
    row_blk = lambda w: pl.BlockSpec((tm, w), lambda i: (i, 0))
    return pl.pallas_call(
        functools.partial(_combine_kernel, final=final, n_tiles=n_tiles),
        grid=(n_tiles,),
        in_specs=[pl.BlockSpec((1, 1, tm), lambda i: (i, 0, 0), memory_space=pltpu.SMEM),
                  pl.BlockSpec((1, 1, tm), lambda i: (jnp.minimum(i + 1, n_tiles - 1), 0, 0), memory_space=pltpu.SMEM),
                  row_blk(D_MODEL), row_blk(D_PLE), full((1, D_MODEL)),
                  full((D_PLE, D_MODEL)), full((D_MODEL, D_MODEL)), full((1, D_MODEL)),
                  pl.BlockSpec(memory_space=pl.ANY)],
        out_specs=row_blk(D_MODEL),
        out_shape=jax.ShapeDtypeStruct((tt, D_MODEL), F32),
        scratch_shapes=[pltpu.VMEM((2, tm, D_MODEL), F32), pltpu.SemaphoreType.DMA((2,))],
        compiler_params=pltpu.CompilerParams(dimension_semantics=("arbitrary",), vmem_limit_bytes=VMEM_LIMIT),
        name="combine",
    )(dest3, dest3, h1, p_i, g_ple.reshape(1, D_MODEL), w_ple.astype(BF16), w_pg.astype(BF16),
      g_final.reshape(1, D_MODEL), ys)


def _class_experts():
    lo, hi = [], []
    for g in range(N_GROUPS):
        for a in range(EXPERTS_PER_GROUP):
            for b in range(a + 1, EXPERTS_PER_GROUP):
                lo.append(g * EXPERTS_PER_GROUP + a)
                hi.append(g * EXPERTS_PER_GROUP + b)
    return np.asarray(lo, np.int32), np.asarray(hi, np.int32)


def _routing_tables(routed, counts_f, n_blocks):
    cls = routed[:, D_MODEL].astype(jnp.int32)
    rank = routed[:, D_MODEL + 1].astype(jnp.int32)
    counts = counts_f[0, :N_CLASSES].astype(jnp.int32)
    padded = ((counts + MOE_BLOCK - 1) // MOE_BLOCK) * MOE_BLOCK
    pad_end = jnp.cumsum(padded)
    pad_start = pad_end - padded
    ids = jnp.arange(N_CLASSES, dtype=jnp.int32)
    dest = jnp.sum(jnp.where(cls[:, None] == ids, pad_start, 0), axis=-1) + rank
    n_used = (pad_end[-1] // MOE_BLOCK).astype(jnp.int32).reshape(1)
    starts = jnp.arange(n_blocks, dtype=jnp.int32) * MOE_BLOCK
    block_cls = jnp.minimum(jnp.sum((pad_end[None, :] <= starts[:, None]).astype(jnp.int32), axis=1), N_CLASSES - 1)
    pick = lambda table: jnp.sum(jnp.where(block_cls[:, None] == ids, table, 0), axis=-1)
    cls_lo, cls_hi = _class_experts()
    filled = jnp.clip(pick(counts) - (starts - pick(pad_start)), 0, MOE_BLOCK)
    zflag = (filled < MOE_BLOCK).astype(jnp.int32)
    return dest, pick(jnp.asarray(cls_lo)), pick(jnp.asarray(cls_hi)), n_used, zflag


def kernel(x, p, w_in, b_gate, conv_w, conv_b, rpb, g_na, g_ml, w_out, g_mix, g_moe, w_route_group, b_route_group, w_route_expert, b_route_expert, w_exp_gate, w_exp_up, w_exp_down, g_ple, w_ple, w_ple_gate, g_final):
    batch, seq, _ = x.shape
    depth = w_in.shape[0]
    tt = batch * seq
    assert seq % ROW_TILE == 0 and seq % GRID_W == 0
    n_blocks = tt // MOE_BLOCK + N_CLASSES
    h = x.reshape(tt, D_MODEL)
    for i in range(depth):
        qna, kna, vna, qml, kt_ml, vml, oml, g_col, g_row = _in_proj(h, seq, g_mix[i], w_in[i], b_gate[i], conv_w[i], conv_b[i])
        yna = _na_attention(qna, kna, vna, _na_bias_table(rpb[i], seq // GRID_W), g_na[i], batch, seq)
        hml = _mlstm(qml, kt_ml, vml, g_row, g_col, batch, seq)
        h1, routed, counts = _mix_out(h, yna, hml, oml, g_ml[i], w_out[i], g_moe[i],
                                      w_route_group[i], b_route_group[i], w_route_expert[i], b_route_expert[i])
        dest, block_ea, block_eb, n_used, zflag = _routing_tables(routed, counts, n_blocks)
        xs = _dispatch(routed, dest, zflag, n_blocks)
        ys = _expert_ffn(xs, block_ea, block_eb, n_used, w_exp_gate[i], w_exp_up[i], w_exp_down[i])
        h = _combine(h1, ys, dest, p[i].reshape(tt, D_PLE), g_ple[i], w_ple[i], w_ple_gate[i], g_final,
                     final=(i == depth - 1))
    return h.reshape(batch, seq, D_MODEL)
```

```python
import functools

import numpy as np
import jax
import jax.numpy as jnp
from jax import lax
from jax.experimental import pallas as pl
from jax.experimental.pallas import tpu as pltpu

F32 = jnp.float32
BF16 = jnp.bfloat16

D_MODEL = 1024
GRID_W = 64
D_NA = 512
NA_HEADS = 8
NA_HEAD_DIM = 64
NA_WIN_H = 8
NA_WIN_W = 16
D_ML = 512
ML_HEADS = 4
ML_HEAD_DIM = 128
ML_CONV = 5
ML_CHUNK = 128
N_GROUPS = 4
EXPERTS_PER_GROUP = 8
N_EXPERTS = 32
D_EXPERT = 512
D_PLE = 256
EPS = 1e-6
NEG_INF = -1e30

LANES = 128
SUBLANES = 8
ROW_TILE = 512
MOE_BLOCK = 256
PAIRS_PER_GROUP = EXPERTS_PER_GROUP * (EXPERTS_PER_GROUP - 1) // 2
N_CLASSES = N_GROUPS * PAIRS_PER_GROUP
ROUTED_WIDTH = D_MODEL + LANES
INFO_W_LO = D_MODEL + 2
INFO_W_HI = D_MODEL + 3
HALO = SUBLANES
EXPERT_LANE0 = 8
GATE_LANES_PER_DIR = 3 * ML_HEADS
VMEM_LIMIT = 56 * 1024 * 1024


def _params(n_axes, flags=None):
    return pltpu.CompilerParams(dimension_semantics=("arbitrary",) * n_axes, vmem_limit_bytes=VMEM_LIMIT, flags=flags)


def _sigmoid(x):
    return 1.0 / (1.0 + jnp.exp(-x))


def _log_sigmoid(x):
    return jnp.minimum(x, 0.0) - jnp.log1p(jnp.exp(-jnp.abs(x)))


def _rms_scale(x):
    return x * lax.rsqrt(jnp.mean(x * x, axis=-1, keepdims=True) + EPS)


def _scan_chunk(x, axis, reverse, op, identity):
    n = x.shape[axis]
    idx = lax.broadcasted_iota(jnp.int32, x.shape, axis)
    sh = 1
    while sh < n:
        if reverse:
            x = op(x, jnp.where(idx < n - sh, pltpu.roll(x, n - sh, axis), identity))
        else:
            x = op(x, jnp.where(idx >= sh, pltpu.roll(x, sh, axis), identity))
        sh *= 2
    return x


def _in_proj_kernel(hp_ref, h_ref, hn_ref, g_ref, wna_ref, wqk_ref, wvo_ref, wgr_ref, bgr_ref, cw_ref, cb_ref,
                    qna_ref, kna_ref, vna_ref, qml_ref, kt_ref, vml_ref, oml_ref, gc_ref, gr_ref,
                    zbuf, *, tiles_per_seq):
    i = pl.program_id(0)
    tm = h_ref.shape[0]
    g = g_ref[...]

    def norm(x):
        return (_rms_scale(x) * g).astype(BF16)

    a = norm(h_ref[...])

    pos = i % tiles_per_seq
    zp = jnp.dot(norm(hp_ref[...]), wqk_ref[...], preferred_element_type=F32)
    zn = jnp.dot(norm(hn_ref[...]), wqk_ref[...], preferred_element_type=F32)
    zbuf[0:HALO, :] = jnp.where(pos == 0, 0.0, zp)
    zbuf[HALO:HALO + tm, :] = jnp.dot(a, wqk_ref[...], preferred_element_type=F32)
    zbuf[HALO + tm:2 * HALO + tm, :] = jnp.where(pos == tiles_per_seq - 1, 0.0, zn)

    ng = 2 * GATE_LANES_PER_DIR
    nt = (((1,), (1,)), ((), ()))
    gi = lax.dot_general(wgr_ref[0:ng, :], a, nt, preferred_element_type=F32) + bgr_ref[0:ng, :]
    gf = lax.dot_general(wgr_ref[ng:2 * ng, :], a, nt, preferred_element_type=F32) + bgr_ref[ng:2 * ng, :]
    row = lax.broadcasted_iota(jnp.int32, (ng, ML_CHUNK), 0)
    is_fw = row < GATE_LANES_PER_DIR
    qsel = jnp.where(is_fw, row, row - GATE_LANES_PER_DIR)
    pad_rows = jnp.zeros((ML_CHUNK - ng, ML_CHUNK), F32)

    def gates(c):
        cols_c = slice(c * ML_CHUNK, (c + 1) * ML_CHUNK)
        lf = _log_sigmoid(gf[:, cols_c])
        b = jnp.where(is_fw, _scan_chunk(lf, 1, False, jnp.add, 0.0), _scan_chunk(lf, 1, True, jnp.add, 0.0))
        r = gi[:, cols_c] - b
        cmax = jnp.where(is_fw, _scan_chunk(r, 1, False, jnp.maximum, -jnp.inf),
                         _scan_chunk(r, 1, True, jnp.maximum, -jnp.inf))
        packed = jnp.where(qsel < ML_HEADS, r, jnp.where(qsel < 2 * ML_HEADS, cmax, b))
        gc_ref[cols_c, :] = jnp.concatenate([packed, pad_rows], axis=0).T
        gr_ref[:, cols_c] = jnp.concatenate([packed[0:ML_HEADS], packed[GATE_LANES_PER_DIR:GATE_LANES_PER_DIR + ML_HEADS]], axis=0)

    def conv(c0, cstep=256):
        acc = cb_ref[:, c0:c0 + cstep]
        for j in range(ML_CONV):
            r0 = HALO - ML_CONV // 2 + j
            acc = acc + zbuf[r0:r0 + tm, c0:c0 + cstep] * cw_ref[j:j + 1, c0:c0 + cstep]
        y = acc * _sigmoid(acc)
        if c0 < D_ML:
            qml_ref[:, c0:c0 + cstep] = (y * (ML_HEAD_DIM ** -0.5)).astype(BF16)
        else:
            kt_ref[c0 - D_ML:c0 - D_ML + cstep, :] = y.T.astype(BF16)

    def proj(w_ref, c0, scale=None):
        z = jnp.dot(a, w_ref[:, c0:c0 + D_NA], preferred_element_type=F32)
        return (z if scale is None else z * scale).astype(BF16)

    conv(0)
    qna_ref[...] = proj(wna_ref, 0, NA_HEAD_DIM ** -0.5)
    conv(256)
    kna_ref[...] = proj(wna_ref, D_NA)
    conv(512)
    vna_ref[...] = proj(wna_ref, 2 * D_NA)
    conv(768)
    vml_ref[...] = proj(wvo_ref, 0)
    for c in range(tm // ML_CHUNK):
        gates(c)
    oml_ref[...] = proj(wvo_ref, D_ML)


def _in_proj(h, seq, g_mix, w_in, b_gate, conv_w, conv_b):
    t = h.shape[0]
    tm = ROW_TILE
    n_tiles = t // tm
    tiles_per_seq = seq // tm
    nh = 4 * ML_HEADS
    o_qk = 3 * D_NA
    o_v = o_qk + 2 * D_ML
    o_g = o_v + 2 * D_ML
    wna = w_in[:, :o_qk].astype(BF16)
    wqk = w_in[:, o_qk:o_v].astype(BF16)
    wvo = w_in[:, o_v:o_g].astype(BF16)
    wg = w_in[:, o_g:o_g + nh]
    ng = 2 * GATE_LANES_PER_DIR
    gate_id = np.arange(ng)
    src_i = (gate_id // GATE_LANES_PER_DIR) * 2 * ML_HEADS + gate_id % ML_HEADS
    wgr = jnp.concatenate([wg[:, src_i], wg[:, src_i + ML_HEADS]], axis=1).T.astype(BF16)
    bgr = jnp.concatenate([b_gate[src_i], b_gate[src_i + ML_HEADS]]).reshape(2 * ng, 1)
    cw = jnp.pad(conv_w, ((0, SUBLANES - ML_CONV), (0, 0)))
    cb = conv_b.reshape(1, 2 * D_ML)
    full = lambda shape: pl.BlockSpec(shape, lambda i: (0,) * len(shape))
    row_blk = lambda w: pl.BlockSpec((tm, w), lambda i: (i, 0))
    hb = tm // HALO
    out_bf = jax.ShapeDtypeStruct((t, D_NA), BF16)
    return pl.pallas_call(
        functools.partial(_in_proj_kernel, tiles_per_seq=tiles_per_seq),
        grid=(n_tiles,),
        in_specs=[
            pl.BlockSpec((HALO, D_MODEL), lambda i: (jnp.maximum(i * hb - 1, 0), 0)),
            row_blk(D_MODEL),
            pl.BlockSpec((HALO, D_MODEL), lambda i: (jnp.minimum((i + 1) * hb, t // HALO - 1), 0)),
            full((1, D_MODEL)), full((D_MODEL, o_qk)), full((D_MODEL, 2 * D_ML)), full((D_MODEL, 2 * D_ML)),
            full((2 * ng, D_MODEL)), full((2 * ng, 1)),
            full((SUBLANES, 2 * D_ML)), full((1, 2 * D_ML)),
        ],
        out_specs=[row_blk(D_NA)] * 4 + [pl.BlockSpec((D_ML, tm), lambda i: (0, i))] + [row_blk(D_NA)] * 2
        + [row_blk(LANES), pl.BlockSpec((2 * ML_HEADS, tm), lambda i: (0, i))],
        out_shape=[out_bf] * 4 + [jax.ShapeDtypeStruct((D_ML, t), BF16)] + [out_bf] * 2
        + [jax.ShapeDtypeStruct((t, LANES), F32), jax.ShapeDtypeStruct((2 * ML_HEADS, t), F32)],
        scratch_shapes=[pltpu.VMEM((tm + 2 * HALO, 2 * D_ML), F32)],
        compiler_params=_params(1),
        name="in_proj",
    )(h, h, h, g_mix.reshape(1, D_MODEL), wna, wqk, wvo, wgr, bgr, cw, cb)


def _na_bias_table(rpb, rows):
    kh = min(NA_WIN_H, rows)
    o = np.arange(kh)[:, None]
    kr = np.arange(kh)[None, :]
    dr = kr - o + (NA_WIN_H - 1)
    qc = np.arange(GRID_W)[:, None]
    kc = np.arange(GRID_W)[None, :]
    cs = np.clip(qc - NA_WIN_W // 2, 0, GRID_W - NA_WIN_W)
    inw = (kc >= cs) & (kc < cs + NA_WIN_W)
    dc = np.clip(kc - qc, 1 - NA_WIN_W, NA_WIN_W - 1) + (NA_WIN_W - 1)
    sel_r = jnp.asarray(dr[:, :, None] == np.arange(2 * NA_WIN_H - 1), F32)
    sel_c = jnp.asarray(dc[:, :, None] == np.arange(2 * NA_WIN_W - 1), F32)
    rows_sel = jnp.einsum("hrc,okr->hokc", rpb.astype(F32), sel_r, precision=lax.Precision.HIGHEST)
    b = jnp.einsum("hokc,qlc->hoqkl", rows_sel, sel_c, precision=lax.Precision.HIGHEST)
    b = jnp.where(inw[None, None, :, None, :], b, NEG_INF)
    return b.reshape(rpb.shape[0], kh, GRID_W, kh * GRID_W)


NA_HEADS_PER_BLOCK = 4
NA_BAND_ROWS = 16
NA_UNROLL = 8


def _na_kernel(q_ref, k_ref, v_ref, bias_ref, g_ref, o_ref, *, rows, kh, band):
    nkeys = kh * GRID_W
    hpb = NA_HEADS_PER_BLOCK
    m_rows = hpb * GRID_W
    width = hpb * NA_HEAD_DIM
    band_i = pl.program_id(2)
    lane_head = lax.shift_right_logical(lax.broadcasted_iota(jnp.int32, (m_rows, width), 1), 6)
    row_head = lax.shift_right_logical(lax.broadcasted_iota(jnp.int32, (m_rows, width), 0), 6)
    own = lane_head == row_head
    g = g_ref[...]

    def one_row(rl):
        r = band_i * band + rl
        rs = jnp.clip(r - kh // 2, 0, rows - kh)
        off = r - rs
        q = q_ref[0, pl.ds(pl.multiple_of(rl * GRID_W, GRID_W), GRID_W), :]
        kb = k_ref[0, pl.ds(pl.multiple_of(rs * GRID_W, GRID_W), nkeys), :]
        vb = v_ref[0, pl.ds(pl.multiple_of(rs * GRID_W, GRID_W), nkeys), :]
        qs = jnp.where(own, jnp.concatenate([q] * hpb, axis=0), jnp.zeros((m_rows, width), BF16))
        s = lax.dot_general(qs, kb, (((1,), (1,)), ((), ())), preferred_element_type=F32)
        s = s + bias_ref[:, off].reshape(m_rows, nkeys)
        m = jnp.max(s, axis=-1, keepdims=True)
        p = jnp.exp(s - m)
        l = jnp.sum(p, axis=-1, keepdims=True)
        pv = jnp.dot(p.astype(BF16), vb, preferred_element_type=F32)
        oh = jnp.where(own, pv * (1.0 / l), 0.0)
        ms = jnp.sum(oh * oh, axis=-1, keepdims=True) * (1.0 / NA_HEAD_DIM)
        yn = oh * lax.rsqrt(ms + EPS)
        y = yn[0:GRID_W]
        for hh in range(1, hpb):
            y = y + yn[hh * GRID_W:(hh + 1) * GRID_W]
        o_ref[0, pl.ds(pl.multiple_of(rl * GRID_W, GRID_W), GRID_W), :] = (y * g).astype(BF16)

    def body(it, carry):
        for u in range(NA_UNROLL):
            one_row(it * NA_UNROLL + u)
        return carry

    lax.fori_loop(0, band // NA_UNROLL, body, 0)


def _na_attention(q, k, v, bias, g_na, batch, seq):
    rows = seq // GRID_W
    kh = min(NA_WIN_H, rows)
    hpb = NA_HEADS_PER_BLOCK
    width = hpb * NA_HEAD_DIM
    band = min(NA_BAND_ROWS, rows)
    assert rows % band == 0 and band % NA_UNROLL == 0
    q3, k3, v3 = (x.reshape(batch, seq, D_NA) for x in (q, k, v))
    seq_blk = pl.BlockSpec((1, seq, width), lambda b, j, r: (b, 0, j))
    band_blk = pl.BlockSpec((1, band * GRID_W, width), lambda b, j, r: (b, r, j))
    out = pl.pallas_call(
        functools.partial(_na_kernel, rows=rows, kh=kh, band=band),
        grid=(batch, D_NA // width, rows // band),
        in_specs=[band_blk, seq_blk, seq_blk,
                  pl.BlockSpec((hpb, kh, GRID_W, kh * GRID_W), lambda b, j, r: (j, 0, 0, 0)),
                  pl.BlockSpec((1, width), lambda b, j, r: (0, j))],
        out_specs=band_blk,
        out_shape=jax.ShapeDtypeStruct((batch, seq, D_NA), BF16),
        compiler_params=_params(3),
        name="na_attn",
    )(q3, k3, v3, bias, g_na.reshape(1, D_NA))
    return out.reshape(batch * seq, D_NA)


ML_UNROLL = 4


def _mlstm_kernel(q_ref, kt_ref, v_ref, gr_ref, gc_ref, o_ref, mf_ref, mb_ref, *, n_chunks):
    L = ML_CHUNK
    hd = pl.program_id(1)
    lane = lax.broadcasted_iota(jnp.int32, (L, LANES), 1)
    grow_i = lax.broadcasted_iota(jnp.int32, (2 * ML_HEADS, L), 0)
    row_i = lax.broadcasted_iota(jnp.int32, (L, L), 0)
    col_i = lax.broadcasted_iota(jnp.int32, (L, L), 1)
    ones = jnp.ones((L, ML_HEAD_DIM), BF16)

    def step(c, m_prev, st_ref, reverse):
        off = pl.multiple_of(c * L, L)
        d = 1 if reverse else 0
        qc = q_ref[0, pl.ds(off, L), :]
        kt = kt_ref[:, pl.ds(off, L)]
        vc = v_ref[0, pl.ds(off, L), :]
        gcol = gc_ref[pl.ds(off, L), :]
        grow = gr_ref[:, pl.ds(off, L)]

        def col(qn):
            sel = lane == d * GATE_LANES_PER_DIR + qn * ML_HEADS + hd
            return jnp.sum(jnp.where(sel, gcol, 0.0), axis=1, keepdims=True)

        r_col, cmax_col, b_col = col(0), col(1), col(2)
        r_row = jnp.sum(jnp.where(grow_i == d * ML_HEADS + hd, grow, 0.0), axis=0, keepdims=True)
        end = 0 if reverse else L - 1
        mm_end = jnp.maximum(m_prev, cmax_col[end:end + 1, :])
        m_new = b_col[end:end + 1, :] + mm_end
        mm_col = jnp.maximum(m_prev, cmax_col)
        causal = (col_i >= row_i) if reverse else (col_i <= row_i)
        dmat = jnp.exp(jnp.where(causal, r_row - mm_col, -jnp.inf))
        s = jnp.dot(qc, kt, preferred_element_type=F32) * dmat
        w_inter = jnp.exp(m_prev - mm_col)
        floor = jnp.exp(-(b_col + mm_col))
        w_col = jnp.exp(r_col - mm_end)
        decay = jnp.exp(m_prev - mm_end)
        vext = jnp.concatenate([vc, ones], axis=1)
        state = st_ref[...]
        nd = (jnp.dot(s.astype(BF16), vext, preferred_element_type=F32)
              + w_inter * jnp.dot(qc, state.astype(BF16), preferred_element_type=F32))
        h = nd[:, :ML_HEAD_DIM] / jnp.maximum(jnp.abs(nd[:, ML_HEAD_DIM:]), floor)
        wv = jnp.concatenate([(w_col * vc.astype(F32)).astype(BF16),
                              jnp.broadcast_to(w_col, (L, ML_HEAD_DIM)).astype(BF16)], axis=1)
        st_ref[...] = decay * state + jnp.dot(kt, wv, preferred_element_type=F32)
        return h, m_new

    mf_ref[...] = jnp.zeros_like(mf_ref)
    mb_ref[...] = jnp.zeros_like(mb_ref)

    def sweep(accumulate):
        def body(it, carry):
            m_f, m_b = carry
            for u in range(ML_UNROLL):
                j = it * ML_UNROLL + u
                jb = n_chunks - 1 - j
                h_f, m_f = step(j, m_f, mf_ref, False)
                h_b, m_b = step(jb, m_b, mb_ref, True)
                rows_f = pl.ds(pl.multiple_of(j * L, L), L)
                rows_b = pl.ds(pl.multiple_of(jb * L, L), L)
                if accumulate:
                    o_ref[0, rows_f, :] += h_f
                    o_ref[0, rows_b, :] += h_b
                else:
                    o_ref[0, rows_f, :] = h_f
                    o_ref[0, rows_b, :] = h_b
            return m_f, m_b
        return body

    half = n_chunks // (2 * ML_UNROLL)
    zero = jnp.zeros((1, 1), F32)
    carry = lax.fori_loop(0, half, sweep(False), (zero, zero))
    lax.fori_loop(half, 2 * half, sweep(True), carry)


def _mlstm(q, kt, v, g_row, g_col, batch, seq):
    n_chunks = seq // ML_CHUNK
    assert n_chunks % (2 * ML_UNROLL) == 0
    q3, v3 = (x.reshape(batch, seq, D_ML) for x in (q, v))
    seq_blk = pl.BlockSpec((1, seq, ML_HEAD_DIM), lambda b, j: (b, 0, j))
    out = pl.pallas_call(
        functools.partial(_mlstm_kernel, n_chunks=n_chunks),
        grid=(batch, ML_HEADS),
        in_specs=[seq_blk, pl.BlockSpec((ML_HEAD_DIM, seq), lambda b, j: (j, b)), seq_blk,
                  pl.BlockSpec((2 * ML_HEADS, seq), lambda b, j: (0, b)),
                  pl.BlockSpec((seq, LANES), lambda b, j: (b, 0))],
        out_specs=seq_blk,
        out_shape=jax.ShapeDtypeStruct((batch, seq, D_ML), F32),
        scratch_shapes=[pltpu.VMEM((ML_HEAD_DIM, 2 * ML_HEAD_DIM), F32)] * 2,
        compiler_params=_params(2),
        name="mlstm",
    )(q3, kt, v3, g_row, g_col)
    return out.reshape(batch * seq, D_ML)


def _mix_out_kernel(h_ref, yna_ref, hml_ref, oml_ref, gml_ref, wo_ref, gmoe_ref, wrh_ref, wrl_ref, br_ref, tri_ref,
                    h1_ref, t_ref, cnt_ref):
    i = pl.program_id(0)
    tm = h_ref.shape[0]

    @pl.when(i == 0)
    def _():
        cnt_ref[...] = jnp.zeros_like(cnt_ref)

    hml = hml_ref[...]
    parts = [_rms_scale(hml[:, d * ML_HEAD_DIM:(d + 1) * ML_HEAD_DIM]) for d in range(ML_HEADS)]
    yml = _sigmoid(oml_ref[...].astype(F32)) * (jnp.concatenate(parts, axis=1) * gml_ref[...])
    mix = (jnp.dot(yna_ref[...], wo_ref[0:D_NA, :], preferred_element_type=F32)
           + jnp.dot(yml.astype(BF16), wo_ref[D_NA:D_NA + D_ML, :], preferred_element_type=F32))
    h1 = h_ref[...] + mix
    h1_ref[...] = h1
    t = _rms_scale(h1) * gmoe_ref[...]
    t_ref[:, 0:D_MODEL] = t

    t_hi = t.astype(BF16)
    t_lo = (t - t_hi.astype(F32)).astype(BF16)
    logits = (jnp.dot(t_hi, wrh_ref[...], preferred_element_type=F32)
              + jnp.dot(t_lo, wrh_ref[...], preferred_element_type=F32)
              + jnp.dot(t_hi, wrl_ref[...], preferred_element_type=F32)) + br_ref[...]
    lane = lax.broadcasted_iota(jnp.int32, (tm, LANES), 1)
    big = jnp.int32(LANES)

    def top(x):
        mx = jnp.max(x, axis=1, keepdims=True)
        idx = jnp.min(jnp.where(x == mx, lane, big), axis=1, keepdims=True)
        return mx, idx

    lg = jnp.where(lane < N_GROUPS, logits, -jnp.inf)
    g_max, g_idx = top(lg)
    pg_top = 1.0 / jnp.sum(jnp.exp(lg - g_max), axis=1, keepdims=True)
    lo = EXPERT_LANE0 + g_idx * EXPERTS_PER_GROUP
    le = jnp.where((lane >= lo) & (lane < lo + EXPERTS_PER_GROUP), logits, -jnp.inf)
    e1_max, i1 = top(le)
    e_sum = jnp.sum(jnp.exp(le - e1_max), axis=1, keepdims=True)
    e2_max, i2 = top(jnp.where(lane == i1, -jnp.inf, le))
    p1 = 1.0 / e_sum
    p2 = jnp.exp(e2_max - e1_max) / e_sum
    w1 = pg_top * p1 / (p1 + p2)
    w2 = pg_top * p2 / (p1 + p2)
    e1 = i1 - EXPERT_LANE0
    e2 = i2 - EXPERT_LANE0

    la = jnp.minimum(e1, e2) - g_idx * EXPERTS_PER_GROUP
    lb = jnp.maximum(e1, e2) - g_idx * EXPERTS_PER_GROUP
    cls = g_idx * PAIRS_PER_GROUP + lax.shift_right_logical(la * (2 * EXPERTS_PER_GROUP - 1 - la), 1) + (lb - la - 1)
    w_lo = jnp.where(e1 < e2, w1, w2)
    w_hi = jnp.where(e1 < e2, w2, w1)
    oh = (lane == cls).astype(F32)
    before = jnp.dot(tri_ref[...], oh.astype(BF16), preferred_element_type=F32) + cnt_ref[...]
    rank = jnp.sum(oh * before, axis=1, keepdims=True)
    cnt_ref[...] = cnt_ref[...] + jnp.sum(oh, axis=0, keepdims=True)
    info = jnp.zeros((tm, LANES), F32)
    for j, cval in enumerate([cls.astype(F32), rank, w_lo, w_hi]):
        info = jnp.where(lane == j, cval, info)
    t_ref[:, D_MODEL:D_MODEL + LANES] = info


def _mix_out(h, yna, hml, oml, g_ml, w_out, g_moe, w_rg, b_rg, w_re, b_re):
    t = h.shape[0]
    tm = ROW_TILE
    wr = jnp.zeros((D_MODEL, LANES), F32)
    wr = wr.at[:, :N_GROUPS].set(w_rg).at[:, EXPERT_LANE0:EXPERT_LANE0 + N_EXPERTS].set(w_re)
    wrh = wr.astype(BF16)
    wrl = (wr - wrh.astype(F32)).astype(BF16)
    br = jnp.zeros((1, LANES), F32)
    br = br.at[0, :N_GROUPS].set(b_rg).at[0, EXPERT_LANE0:EXPERT_LANE0 + N_EXPERTS].set(b_re)
    tri = jnp.asarray(np.tril(np.ones((tm, tm), np.float32), -1), BF16)
    full = lambda shape: pl.BlockSpec(shape, lambda i: (0,) * len(shape))
    row_blk = lambda w: pl.BlockSpec((tm, w), lambda i: (i, 0))
    return pl.pallas_call(
        _mix_out_kernel,
        grid=(t // tm,),
        in_specs=[row_blk(D_MODEL), row_blk(D_NA), row_blk(D_ML), row_blk(D_ML), full((1, D_ML)),
                  full((D_MODEL, D_MODEL)), full((1, D_MODEL)), full((D_MODEL, LANES)), full((D_MODEL, LANES)),
                  full((1, LANES)), full((tm, tm))],
        out_specs=[row_blk(D_MODEL), row_blk(ROUTED_WIDTH), full((1, LANES))],
        out_shape=[jax.ShapeDtypeStruct((t, D_MODEL), F32), jax.ShapeDtypeStruct((t, ROUTED_WIDTH), F32),
                   jax.ShapeDtypeStruct((1, LANES), F32)],
        compiler_params=_params(1),
        name="mix_out",
    )(h, yna, hml, oml, g_ml.reshape(1, D_ML), w_out.astype(BF16), g_moe.reshape(1, D_MODEL), wrh, wrl, br, tri)


def _dispatch_kernel(zflag_ref, dest_ref, t_ref, xs_ref, zbuf, sem, zsem, *, n_blocks):
    tm = t_ref.shape[0]

    @pl.when(pl.program_id(0) == 0)
    def _():
        zbuf[...] = jnp.zeros_like(zbuf)

        def zero_copy(b):
            return pltpu.make_async_copy(zbuf, xs_ref.at[pl.ds(pl.multiple_of(b * MOE_BLOCK, MOE_BLOCK), MOE_BLOCK)], zsem)

        def zissue(b, carry):
            @pl.when(zflag_ref[b] != 0)
            def _():
                zero_copy(b).start()
            return carry

        def zwait(b, carry):
            @pl.when(zflag_ref[b] != 0)
            def _():
                zero_copy(b).wait()
            return carry

        lax.fori_loop(0, n_blocks, zissue, 0)
        lax.fori_loop(0, n_blocks, zwait, 0)

    def issue(j, carry):
        pltpu.make_async_copy(t_ref.at[pl.ds(j, 1)], xs_ref.at[pl.ds(dest_ref[0, 0, j], 1)], sem).start()
        return carry

    lax.fori_loop(0, tm, issue, 0)
    pltpu.make_async_copy(t_ref, xs_ref.at[pl.ds(0, tm)], sem).wait()


def _dispatch(t, dest, zflag, n_blocks):
    tt = t.shape[0]
    tm = ROW_TILE
    n_tiles = tt // tm
    return pl.pallas_call(
        functools.partial(_dispatch_kernel, n_blocks=n_blocks),
        grid_spec=pltpu.PrefetchScalarGridSpec(
            num_scalar_prefetch=1,
            grid=(n_tiles,),
            in_specs=[pl.BlockSpec((1, 1, tm), lambda i, zf: (i, 0, 0), memory_space=pltpu.SMEM),
                      pl.BlockSpec((tm, ROUTED_WIDTH), lambda i, zf: (i, 0))],
            out_specs=pl.BlockSpec(memory_space=pl.ANY),
            scratch_shapes=[pltpu.VMEM((MOE_BLOCK, ROUTED_WIDTH), F32), pltpu.SemaphoreType.DMA,
                            pltpu.SemaphoreType.DMA],
        ),
        out_shape=jax.ShapeDtypeStruct((n_blocks * MOE_BLOCK, ROUTED_WIDTH), F32),
        compiler_params=_params(1),
        name="dispatch",
    )(zflag, dest.reshape(n_tiles, 1, tm), t)


def _ffn_kernel(ea_ref, eb_ref, nu_ref, x_ref, wga_ref, wua_ref, wda_ref, wgb_ref, wub_ref, wdb_ref, y_ref):
    del ea_ref, eb_ref

    def swiglu(x, wg_ref, wu_ref, wd_ref):
        g = jnp.dot(x, wg_ref[0].astype(BF16), preferred_element_type=F32)
        u = jnp.dot(x, wu_ref[0].astype(BF16), preferred_element_type=F32)
        hid = (g * _sigmoid(g) * u).astype(BF16)
        return jnp.dot(hid, wd_ref[0].astype(BF16), preferred_element_type=F32)

    @pl.when(pl.program_id(0) < nu_ref[0])
    def _():
        x = x_ref[:, 0:D_MODEL].astype(BF16)
        y_ref[...] = (x_ref[:, INFO_W_LO:INFO_W_LO + 1] * swiglu(x, wga_ref, wua_ref, wda_ref)
                      + x_ref[:, INFO_W_HI:INFO_W_HI + 1] * swiglu(x, wgb_ref, wub_ref, wdb_ref))

    @pl.when(pl.program_id(0) >= nu_ref[0])
    def _():
        y_ref[...] = jnp.zeros_like(y_ref)


def _expert_ffn(xs, block_ea, block_eb, n_used, w_gate, w_up, w_down):
    n_rows = xs.shape[0]
    nb = n_rows // MOE_BLOCK
    wg, wu, wd = w_gate, w_up, w_down
    blk = lambda i, ea, eb, nu: (jnp.minimum(i, nu[0] - 1), 0)
    sel_a = lambda i, ea, eb, nu: (ea[jnp.minimum(i, nu[0] - 1)], 0, 0)
    sel_b = lambda i, ea, eb, nu: (eb[jnp.minimum(i, nu[0] - 1)], 0, 0)
    up_spec = lambda sel: pl.BlockSpec((1, D_MODEL, D_EXPERT), sel)
    down_spec = lambda sel: pl.BlockSpec((1, D_EXPERT, D_MODEL), sel)
    return pl.pallas_call(
        _ffn_kernel,
        grid_spec=pltpu.PrefetchScalarGridSpec(
            num_scalar_prefetch=3,
            grid=(nb,),
            in_specs=[pl.BlockSpec((MOE_BLOCK, ROUTED_WIDTH), blk),
                      up_spec(sel_a), up_spec(sel_a), down_spec(sel_a),
                      up_spec(sel_b), up_spec(sel_b), down_spec(sel_b)],
            out_specs=pl.BlockSpec((MOE_BLOCK, D_MODEL), lambda i, ea, eb, nu: (i, 0)),
        ),
        out_shape=jax.ShapeDtypeStruct((n_rows, D_MODEL), F32),
        compiler_params=_params(1),
        name="expert_ffn",
    )(block_ea, block_eb, n_used, xs, wg, wu, wd, wg, wu, wd)


def _combine_kernel(dcur_ref, dnxt_ref, h1_ref, p_ref, gple_ref, wple_ref, wpg_ref, gfin_ref, ys_ref,
                    o_ref, buf, sem, *, final, n_tiles):
    i = pl.program_id(0)
    tm = h1_ref.shape[0]
    slot = i % 2

    def issue_tile(d_ref, s):
        def issue(j, carry):
            pltpu.make_async_copy(ys_ref.at[pl.ds(d_ref[0, 0, j], 1)], buf.at[s, pl.ds(j, 1)], sem.at[s]).start()
            return carry

        lax.fori_loop(0, tm, issue, 0)

    @pl.when(i == 0)
    def _():
        issue_tile(dcur_ref, 0)

    @pl.when(i + 1 < n_tiles)
    def _():
        issue_tile(dnxt_ref, 1 - slot)

    pltpu.make_async_copy(ys_ref.at[pl.ds(0, tm)], buf.at[slot], sem.at[slot]).wait()

    h2 = h1_ref[...] + buf[slot]
    a = (_rms_scale(h2) * gple_ref[...]).astype(BF16)
    gate = _sigmoid(jnp.dot(a, wpg_ref[...], preferred_element_type=F32))
    emb = jnp.dot(p_ref[...].astype(BF16), wple_ref[...], preferred_element_type=F32)
    h3 = h2 + emb * gate
    if final:
        h3 = _rms_scale(h3) * gfin_ref[...]
    o_ref[...] = h3


def _combine(h1, ys, dest, p_i, g_ple, w_ple, w_pg, g_final, final):
    tt = h1.shape[0]
    tm = ROW_TILE
    n_tiles = tt // tm
    dest3 = dest.reshape(n_tiles, 1, tm)
    full = lambda shape: pl.BlockSpec(shape, lambda i: (0,) * len(shape))
    row_blk = lambda w: pl.BlockSpec((tm, w), lambda i: (i, 0))
    return pl.pallas_call(
        functools.partial(_combine_kernel, final=final, n_tiles=n_tiles),
        grid=(n_tiles,),
        in_specs=[pl.BlockSpec((1, 1, tm), lambda i: (i, 0, 0), memory_space=pltpu.SMEM),
                  pl.BlockSpec((1, 1, tm), lambda i: (jnp.minimum(i + 1, n_tiles - 1), 0, 0), memory_space=pltpu.SMEM),
                  row_blk(D_MODEL), row_blk(D_PLE), full((1, D_MODEL)),
                  full((D_PLE, D_MODEL)), full((D_MODEL, D_MODEL)), full((1, D_MODEL)),
                  pl.BlockSpec(memory_space=pl.ANY)],
        out_specs=row_blk(D_MODEL),
        out_shape=jax.ShapeDtypeStruct((tt, D_MODEL), F32),
        scratch_shapes=[pltpu.VMEM((2, tm, D_MODEL), F32), pltpu.SemaphoreType.DMA((2,))],
        compiler_params=_params(1),
        name="combine",
    )(dest3, dest3, h1, p_i, g_ple.reshape(1, D_MODEL), w_ple.astype(BF16), w_pg.astype(BF16),
      g_final.reshape(1, D_MODEL), ys)


def _class_experts():
    lo, hi = [], []
    for g in range(N_GROUPS):
        for a in range(EXPERTS_PER_GROUP):
            for b in range(a + 1, EXPERTS_PER_GROUP):
                lo.append(g * EXPERTS_PER_GROUP + a)
                hi.append(g * EXPERTS_PER_GROUP + b)
    return np.asarray(lo, np.int32), np.asarray(hi, np.int32)


def _routing_tables(routed, counts_f, n_blocks):
    cls = routed[:, D_MODEL].astype(jnp.int32)
    rank = routed[:, D_MODEL + 1].astype(jnp.int32)
    counts = counts_f[0, :N_CLASSES].astype(jnp.int32)
    padded = ((counts + MOE_BLOCK - 1) // MOE_BLOCK) * MOE_BLOCK
    pad_end = jnp.cumsum(padded)
    pad_start = pad_end - padded
    ids = jnp.arange(N_CLASSES, dtype=jnp.int32)
    dest = jnp.sum(jnp.where(cls[:, None] == ids, pad_start, 0), axis=-1) + rank
    n_used = (pad_end[-1] // MOE_BLOCK).astype(jnp.int32).reshape(1)
    starts = jnp.arange(n_blocks, dtype=jnp.int32) * MOE_BLOCK
    block_cls = jnp.minimum(jnp.sum((pad_end[None, :] <= starts[:, None]).astype(jnp.int32), axis=1), N_CLASSES - 1)
    pick = lambda table: jnp.sum(jnp.where(block_cls[:, None] == ids, table, 0), axis=-1)
    cls_lo, cls_hi = _class_experts()
    filled = jnp.clip(pick(counts) - (starts - pick(pad_start)), 0, MOE_BLOCK)
    zflag = (filled < MOE_BLOCK).astype(jnp.int32)
    return dest, pick(jnp.asarray(cls_lo)), pick(jnp.asarray(cls_hi)), n_used, zflag


def kernel(x, p, w_in, b_gate, conv_w, conv_b, rpb, g_na, g_ml, w_out, g_mix, g_moe, w_route_group, b_route_group, w_route_expert, b_route_expert, w_exp_gate, w_exp_up, w_exp_down, g_ple, w_ple, w_ple_gate, g_final):
    batch, seq, _ = x.shape
    depth = w_in.shape[0]
    tt = batch * seq
    assert seq % ROW_TILE == 0 and seq % GRID_W == 0
    n_blocks = tt // MOE_BLOCK + N_CLASSES
    h = x.reshape(tt, D_MODEL)
    for i in range(depth):
        qna, kna, vna, qml, kt_ml, vml, oml, g_col, g_row = _in_proj(h, seq, g_mix[i], w_in[i], b_gate[i], conv_w[i], conv_b[i])
        yna = _na_attention(qna, kna, vna, _na_bias_table(rpb[i], seq // GRID_W), g_na[i], batch, seq)
        hml = _mlstm(qml, kt_ml, vml, g_row, g_col, batch, seq)
        h1, routed, counts = _mix_out(h, yna, hml, oml, g_ml[i], w_out[i], g_moe[i],
                                      w_route_group[i], b_route_group[i], w_route_expert[i], b_route_expert[i])
        dest, block_ea, block_eb, n_used, zflag = _routing_tables(routed, counts, n_blocks)
        xs = _dispatch(routed, dest, zflag, n_blocks)
        ys = _expert_ffn(xs, block_ea, block_eb, n_used, w_exp_gate[i], w_exp_up[i], w_exp_down[i])
        h = _combine(h1, ys, dest, p[i].reshape(tt, D_PLE), g_ple[i], w_ple[i], w_ple_gate[i], g_final,
                     final=(i == depth - 1))
    return h.reshape(batch, seq, D_MODEL)
```

```python
import functools

import numpy as np
import jax
import jax.numpy as jnp
from jax import lax
from jax.experimental import pallas as pl
from jax.experimental.pallas import tpu as pltpu

F32 = jnp.float32
BF16 = jnp.bfloat16

D_MODEL = 1024
GRID_W = 64
D_NA = 512
NA_HEADS = 8
NA_HEAD_DIM = 64
NA_WIN_H = 8
NA_WIN_W = 16
D_ML = 512
ML_HEADS = 4
ML_HEAD_DIM = 128
ML_CONV = 5
ML_CHUNK = 128
N_GROUPS = 4
EXPERTS_PER_GROUP = 8
N_EXPERTS = 32
D_EXPERT = 512
D_PLE = 256
EPS = 1e-6
NEG_INF = -1e30

LANES = 128
SUBLANES = 8
ROW_TILE = 512
MOE_BLOCK = 256
PAIRS_PER_GROUP = EXPERTS_PER_GROUP * (EXPERTS_PER_GROUP - 1) // 2
N_CLASSES = N_GROUPS * PAIRS_PER_GROUP
ROUTED_WIDTH = D_MODEL + LANES
INFO_W_LO = D_MODEL + 2
INFO_W_HI = D_MODEL + 3
HALO = SUBLANES
ISSUE_UNROLL = 8
EXPERT_LANE0 = 8
GATE_LANES_PER_DIR = 3 * ML_HEADS
VMEM_LIMIT = 56 * 1024 * 1024


def _params(n_axes, flags=None):
    return pltpu.CompilerParams(dimension_semantics=("arbitrary",) * n_axes, vmem_limit_bytes=VMEM_LIMIT, flags=flags)


def _sigmoid(x):
    return 1.0 / (1.0 + jnp.exp(-x))


def _log_sigmoid(x):
    return jnp.minimum(x, 0.0) - jnp.log1p(jnp.exp(-jnp.abs(x)))


def _rms_scale(x):
    return x * lax.rsqrt(jnp.mean(x * x, axis=-1, keepdims=True) + EPS)


def _scan_chunk(x, axis, reverse, op, identity):
    n = x.shape[axis]
    idx = lax.broadcasted_iota(jnp.int32, x.shape, axis)
    sh = 1
    while sh < n:
        if reverse:
            x = op(x, jnp.where(idx < n - sh, pltpu.roll(x, n - sh, axis), identity))
        else:
            x = op(x, jnp.where(idx >= sh, pltpu.roll(x, sh, axis), identity))
        sh *= 2
    return x


def _in_proj_kernel(hp_ref, h_ref, hn_ref, g_ref, wna_ref, wqk_ref, wvo_ref, wgr_ref, bgr_ref, cw_ref, cb_ref,
                    qna_ref, kna_ref, vna_ref, qml_ref, kt_ref, vml_ref, oml_ref, gc_ref, gr_ref,
                    zbuf, *, tiles_per_seq):
    i = pl.program_id(0)
    tm = h_ref.shape[0]
    g = g_ref[...]

    def norm(x):
        return (_rms_scale(x) * g).astype(BF16)

    a = norm(h_ref[...])

    pos = i % tiles_per_seq
    zp = jnp.dot(norm(hp_ref[...]), wqk_ref[...], preferred_element_type=F32)
    zn = jnp.dot(norm(hn_ref[...]), wqk_ref[...], preferred_element_type=F32)
    zbuf[0:HALO, :] = jnp.where(pos == 0, 0.0, zp)
    zbuf[HALO:HALO + tm, :] = jnp.dot(a, wqk_ref[...], preferred_element_type=F32)
    zbuf[HALO + tm:2 * HALO + tm, :] = jnp.where(pos == tiles_per_seq - 1, 0.0, zn)

    ng = 2 * GATE_LANES_PER_DIR
    nt = (((1,), (1,)), ((), ()))
    gi = lax.dot_general(wgr_ref[0:ng, :], a, nt, preferred_element_type=F32) + bgr_ref[0:ng, :]
    gf = lax.dot_general(wgr_ref[ng:2 * ng, :], a, nt, preferred_element_type=F32) + bgr_ref[ng:2 * ng, :]
    row = lax.broadcasted_iota(jnp.int32, (ng, ML_CHUNK), 0)
    is_fw = row < GATE_LANES_PER_DIR
    qsel = jnp.where(is_fw, row, row - GATE_LANES_PER_DIR)
    pad_rows = jnp.zeros((ML_CHUNK - ng, ML_CHUNK), F32)

    def gates(c):
        cols_c = slice(c * ML_CHUNK, (c + 1) * ML_CHUNK)
        lf = _log_sigmoid(gf[:, cols_c])
        b = jnp.where(is_fw, _scan_chunk(lf, 1, False, jnp.add, 0.0), _scan_chunk(lf, 1, True, jnp.add, 0.0))
        r = gi[:, cols_c] - b
        cmax = jnp.where(is_fw, _scan_chunk(r, 1, False, jnp.maximum, -jnp.inf),
                         _scan_chunk(r, 1, True, jnp.maximum, -jnp.inf))
        packed = jnp.where(qsel < ML_HEADS, r, jnp.where(qsel < 2 * ML_HEADS, cmax, b))
        gc_ref[cols_c, :] = jnp.concatenate([packed, pad_rows], axis=0).T
        gr_ref[:, cols_c] = jnp.concatenate([packed[0:ML_HEADS], packed[GATE_LANES_PER_DIR:GATE_LANES_PER_DIR + ML_HEADS]], axis=0)

    def conv(c0, cstep=256):
        acc = cb_ref[:, c0:c0 + cstep]
        for j in range(ML_CONV):
            r0 = HALO - ML_CONV // 2 + j
            acc = acc + zbuf[r0:r0 + tm, c0:c0 + cstep] * cw_ref[j:j + 1, c0:c0 + cstep]
        y = acc * _sigmoid(acc)
        if c0 < D_ML:
            qml_ref[:, c0:c0 + cstep] = (y * (ML_HEAD_DIM ** -0.5)).astype(BF16)
        else:
            kt_ref[c0 - D_ML:c0 - D_ML + cstep, :] = y.T.astype(BF16)

    def proj(w_ref, c0, scale=None):
        z = jnp.dot(a, w_ref[:, c0:c0 + D_NA], preferred_element_type=F32)
        return (z if scale is None else z * scale).astype(BF16)

    conv(0)
    qna_ref[...] = proj(wna_ref, 0, NA_HEAD_DIM ** -0.5)
    conv(256)
    kna_ref[...] = proj(wna_ref, D_NA)
    conv(512)
    vna_ref[...] = proj(wna_ref, 2 * D_NA)
    conv(768)
    vml_ref[...] = proj(wvo_ref, 0)
    for c in range(tm // ML_CHUNK):
        gates(c)
    oml_ref[...] = proj(wvo_ref, D_ML)


def _in_proj(h, seq, g_mix, w_in, b_gate, conv_w, conv_b):
    t = h.shape[0]
    tm = ROW_TILE
    n_tiles = t // tm
    tiles_per_seq = seq // tm
    nh = 4 * ML_HEADS
    o_qk = 3 * D_NA
    o_v = o_qk + 2 * D_ML
    o_g = o_v + 2 * D_ML
    wna = w_in[:, :o_qk].astype(BF16)
    wqk = w_in[:, o_qk:o_v].astype(BF16)
    wvo = w_in[:, o_v:o_g].astype(BF16)
    wg = w_in[:, o_g:o_g + nh]
    ng = 2 * GATE_LANES_PER_DIR
    gate_id = np.arange(ng)
    src_i = (gate_id // GATE_LANES_PER_DIR) * 2 * ML_HEADS + gate_id % ML_HEADS
    wgr = jnp.concatenate([wg[:, src_i], wg[:, src_i + ML_HEADS]], axis=1).T.astype(BF16)
    bgr = jnp.concatenate([b_gate[src_i], b_gate[src_i + ML_HEADS]]).reshape(2 * ng, 1)
    cw = jnp.pad(conv_w, ((0, SUBLANES - ML_CONV), (0, 0)))
    cb = conv_b.reshape(1, 2 * D_ML)
    full = lambda shape: pl.BlockSpec(shape, lambda i: (0,) * len(shape))
    row_blk = lambda w: pl.BlockSpec((tm, w), lambda i: (i, 0))
    hb = tm // HALO
    out_bf = jax.ShapeDtypeStruct((t, D_NA), BF16)
    return pl.pallas_call(
        functools.partial(_in_proj_kernel, tiles_per_seq=tiles_per_seq),
        grid=(n_tiles,),
        in_specs=[
            pl.BlockSpec((HALO, D_MODEL), lambda i: (jnp.maximum(i * hb - 1, 0), 0)),
            row_blk(D_MODEL),
            pl.BlockSpec((HALO, D_MODEL), lambda i: (jnp.minimum((i + 1) * hb, t // HALO - 1), 0)),
            full((1, D_MODEL)), full((D_MODEL, o_qk)), full((D_MODEL, 2 * D_ML)), full((D_MODEL, 2 * D_ML)),
            full((2 * ng, D_MODEL)), full((2 * ng, 1)),
            full((SUBLANES, 2 * D_ML)), full((1, 2 * D_ML)),
        ],
        out_specs=[row_blk(D_NA)] * 4 + [pl.BlockSpec((D_ML, tm), lambda i: (0, i))] + [row_blk(D_NA)] * 2
        + [row_blk(LANES), pl.BlockSpec((2 * ML_HEADS, tm), lambda i: (0, i))],
        out_shape=[out_bf] * 4 + [jax.ShapeDtypeStruct((D_ML, t), BF16)] + [out_bf] * 2
        + [jax.ShapeDtypeStruct((t, LANES), F32), jax.ShapeDtypeStruct((2 * ML_HEADS, t), F32)],
        scratch_shapes=[pltpu.VMEM((tm + 2 * HALO, 2 * D_ML), F32)],
        compiler_params=_params(1),
        name="in_proj",
    )(h, h, h, g_mix.reshape(1, D_MODEL), wna, wqk, wvo, wgr, bgr, cw, cb)


def _na_bias_table(rpb, rows):
    kh = min(NA_WIN_H, rows)
    o = np.arange(kh)[:, None]
    kr = np.arange(kh)[None, :]
    dr = kr - o + (NA_WIN_H - 1)
    qc = np.arange(GRID_W)[:, None]
    kc = np.arange(GRID_W)[None, :]
    cs = np.clip(qc - NA_WIN_W // 2, 0, GRID_W - NA_WIN_W)
    inw = (kc >= cs) & (kc < cs + NA_WIN_W)
    dc = np.clip(kc - qc, 1 - NA_WIN_W, NA_WIN_W - 1) + (NA_WIN_W - 1)
    sel_r = jnp.asarray(dr[:, :, None] == np.arange(2 * NA_WIN_H - 1), F32)
    sel_c = jnp.asarray(dc[:, :, None] == np.arange(2 * NA_WIN_W - 1), F32)
    rows_sel = jnp.einsum("hrc,okr->hokc", rpb.astype(F32), sel_r, precision=lax.Precision.HIGHEST)
    b = jnp.einsum("hokc,qlc->hoqkl", rows_sel, sel_c, precision=lax.Precision.HIGHEST)
    b = jnp.where(inw[None, None, :, None, :], b, NEG_INF)
    return b.reshape(rpb.shape[0], kh, GRID_W, kh * GRID_W)


NA_HEADS_PER_BLOCK = 4
NA_BAND_ROWS = 16
NA_UNROLL = 8


def _na_kernel(q_ref, k_ref, v_ref, bias_ref, g_ref, o_ref, *, rows, kh, band):
    nkeys = kh * GRID_W
    hpb = NA_HEADS_PER_BLOCK
    m_rows = hpb * GRID_W
    width = hpb * NA_HEAD_DIM
    band_i = pl.program_id(2)
    lane_head = lax.shift_right_logical(lax.broadcasted_iota(jnp.int32, (m_rows, width), 1), 6)
    row_head = lax.shift_right_logical(lax.broadcasted_iota(jnp.int32, (m_rows, width), 0), 6)
    own = lane_head == row_head
    g = g_ref[...]

    def one_row(rl):
        r = band_i * band + rl
        rs = jnp.clip(r - kh // 2, 0, rows - kh)
        off = r - rs
        q = q_ref[0, pl.ds(pl.multiple_of(rl * GRID_W, GRID_W), GRID_W), :]
        kb = k_ref[0, pl.ds(pl.multiple_of(rs * GRID_W, GRID_W), nkeys), :]
        vb = v_ref[0, pl.ds(pl.multiple_of(rs * GRID_W, GRID_W), nkeys), :]
        qs = jnp.where(own, jnp.concatenate([q] * hpb, axis=0), jnp.zeros((m_rows, width), BF16))
        s = lax.dot_general(qs, kb, (((1,), (1,)), ((), ())), preferred_element_type=F32)
        s = s + bias_ref[:, off].reshape(m_rows, nkeys)
        m = jnp.max(s, axis=-1, keepdims=True)
        p = jnp.exp(s - m)
        l = jnp.sum(p, axis=-1, keepdims=True)
        pv = jnp.dot(p.astype(BF16), vb, preferred_element_type=F32)
        oh = jnp.where(own, pv * (1.0 / l), 0.0)
        ms = jnp.sum(oh * oh, axis=-1, keepdims=True) * (1.0 / NA_HEAD_DIM)
        yn = oh * lax.rsqrt(ms + EPS)
        y = yn[0:GRID_W]
        for hh in range(1, hpb):
            y = y + yn[hh * GRID_W:(hh + 1) * GRID_W]
        o_ref[0, pl.ds(pl.multiple_of(rl * GRID_W, GRID_W), GRID_W), :] = (y * g).astype(BF16)

    def body(it, carry):
        for u in range(NA_UNROLL):
            one_row(it * NA_UNROLL + u)
        return carry

    lax.fori_loop(0, band // NA_UNROLL, body, 0)


def _na_attention(q, k, v, bias, g_na, batch, seq):
    rows = seq // GRID_W
    kh = min(NA_WIN_H, rows)
    hpb = NA_HEADS_PER_BLOCK
    width = hpb * NA_HEAD_DIM
    band = min(NA_BAND_ROWS, rows)
    assert rows % band == 0 and band % NA_UNROLL == 0
    q3, k3, v3 = (x.reshape(batch, seq, D_NA) for x in (q, k, v))
    seq_blk = pl.BlockSpec((1, seq, width), lambda b, j, r: (b, 0, j))
    band_blk = pl.BlockSpec((1, band * GRID_W, width), lambda b, j, r: (b, r, j))
    out = pl.pallas_call(
        functools.partial(_na_kernel, rows=rows, kh=kh, band=band),
        grid=(batch, D_NA // width, rows // band),
        in_specs=[band_blk, seq_blk, seq_blk,
                  pl.BlockSpec((hpb, kh, GRID_W, kh * GRID_W), lambda b, j, r: (j, 0, 0, 0)),
                  pl.BlockSpec((1, width), lambda b, j, r: (0, j))],
        out_specs=band_blk,
        out_shape=jax.ShapeDtypeStruct((batch, seq, D_NA), BF16),
        compiler_params=_params(3),
        name="na_attn",
    )(q3, k3, v3, bias, g_na.reshape(1, D_NA))
    return out.reshape(batch * seq, D_NA)


ML_UNROLL = 4


def _mlstm_kernel(q_ref, kt_ref, v_ref, gr_ref, gc_ref, o_ref, mf_ref, mb_ref, *, n_chunks):
    L = ML_CHUNK
    hd = pl.program_id(1)
    lane = lax.broadcasted_iota(jnp.int32, (L, LANES), 1)
    grow_i = lax.broadcasted_iota(jnp.int32, (2 * ML_HEADS, L), 0)
    row_i = lax.broadcasted_iota(jnp.int32, (L, L), 0)
    col_i = lax.broadcasted_iota(jnp.int32, (L, L), 1)
    ones = jnp.ones((L, ML_HEAD_DIM), BF16)

    def step(c, m_prev, st_ref, reverse):
        off = pl.multiple_of(c * L, L)
        d = 1 if reverse else 0
        qc = q_ref[0, pl.ds(off, L), :]
        kt = kt_ref[:, pl.ds(off, L)]
        vc = v_ref[0, pl.ds(off, L), :]
        gcol = gc_ref[pl.ds(off, L), :]
        grow = gr_ref[:, pl.ds(off, L)]

        def col(qn):
            sel = lane == d * GATE_LANES_PER_DIR + qn * ML_HEADS + hd
            return jnp.sum(jnp.where(sel, gcol, 0.0), axis=1, keepdims=True)

        r_col, cmax_col, b_col = col(0), col(1), col(2)
        r_row = jnp.sum(jnp.where(grow_i == d * ML_HEADS + hd, grow, 0.0), axis=0, keepdims=True)
        end = 0 if reverse else L - 1
        mm_end = jnp.maximum(m_prev, cmax_col[end:end + 1, :])
        m_new = b_col[end:end + 1, :] + mm_end
        mm_col = jnp.maximum(m_prev, cmax_col)
        causal = (col_i >= row_i) if reverse else (col_i <= row_i)
        dmat = jnp.exp(jnp.where(causal, r_row - mm_col, -jnp.inf))
        s = jnp.dot(qc, kt, preferred_element_type=F32) * dmat
        w_inter = jnp.exp(m_prev - mm_col)
        floor = jnp.exp(-(b_col + mm_col))
        w_col = jnp.exp(r_col - mm_end)
        decay = jnp.exp(m_prev - mm_end)
        vext = jnp.concatenate([vc, ones], axis=1)
        state = st_ref[...]
        nd = (jnp.dot(s.astype(BF16), vext, preferred_element_type=F32)
              + w_inter * jnp.dot(qc, state.astype(BF16), preferred_element_type=F32))
        h = nd[:, :ML_HEAD_DIM] / jnp.maximum(jnp.abs(nd[:, ML_HEAD_DIM:]), floor)
        wv = jnp.concatenate([(w_col * vc.astype(F32)).astype(BF16),
                              jnp.broadcast_to(w_col, (L, ML_HEAD_DIM)).astype(BF16)], axis=1)
        st_ref[...] = decay * state + jnp.dot(kt, wv, preferred_element_type=F32)
        return h, m_new

    mf_ref[...] = jnp.zeros_like(mf_ref)
    mb_ref[...] = jnp.zeros_like(mb_ref)

    def sweep(accumulate):
        def body(it, carry):
            m_f, m_b = carry
            for u in range(ML_UNROLL):
                j = it * ML_UNROLL + u
                jb = n_chunks - 1 - j
                h_f, m_f = step(j, m_f, mf_ref, False)
                h_b, m_b = step(jb, m_b, mb_ref, True)
                rows_f = pl.ds(pl.multiple_of(j * L, L), L)
                rows_b = pl.ds(pl.multiple_of(jb * L, L), L)
                if accumulate:
                    o_ref[0, rows_f, :] += h_f
                    o_ref[0, rows_b, :] += h_b
                else:
                    o_ref[0, rows_f, :] = h_f
                    o_ref[0, rows_b, :] = h_b
            return m_f, m_b
        return body

    half = n_chunks // (2 * ML_UNROLL)
    zero = jnp.zeros((1, 1), F32)
    carry = lax.fori_loop(0, half, sweep(False), (zero, zero))
    lax.fori_loop(half, 2 * half, sweep(True), carry)


def _mlstm(q, kt, v, g_row, g_col, batch, seq):
    n_chunks = seq // ML_CHUNK
    assert n_chunks % (2 * ML_UNROLL) == 0
    q3, v3 = (x.reshape(batch, seq, D_ML) for x in (q, v))
    seq_blk = pl.BlockSpec((1, seq, ML_HEAD_DIM), lambda b, j: (b, 0, j))
    out = pl.pallas_call(
        functools.partial(_mlstm_kernel, n_chunks=n_chunks),
        grid=(batch, ML_HEADS),
        in_specs=[seq_blk, pl.BlockSpec((ML_HEAD_DIM, seq), lambda b, j: (j, b)), seq_blk,
                  pl.BlockSpec((2 * ML_HEADS, seq), lambda b, j: (0, b)),
                  pl.BlockSpec((seq, LANES), lambda b, j: (b, 0))],
        out_specs=seq_blk,
        out_shape=jax.ShapeDtypeStruct((batch, seq, D_ML), F32),
        scratch_shapes=[pltpu.VMEM((ML_HEAD_DIM, 2 * ML_HEAD_DIM), F32)] * 2,
        compiler_params=_params(2),
        name="mlstm",
    )(q3, kt, v3, g_row, g_col)
    return out.reshape(batch * seq, D_ML)


def _mix_out_kernel(h_ref, yna_ref, hml_ref, oml_ref, gml_ref, wo_ref, gmoe_ref, wrh_ref, wrl_ref, br_ref, tri_ref,
                    h1_ref, t_ref, cnt_ref):
    i = pl.program_id(0)
    tm = h_ref.shape[0]

    @pl.when(i == 0)
    def _():
        cnt_ref[...] = jnp.zeros_like(cnt_ref)

    hml = hml_ref[...]
    parts = [_rms_scale(hml[:, d * ML_HEAD_DIM:(d + 1) * ML_HEAD_DIM]) for d in range(ML_HEADS)]
    yml = _sigmoid(oml_ref[...].astype(F32)) * (jnp.concatenate(parts, axis=1) * gml_ref[...])
    mix = (jnp.dot(yna_ref[...], wo_ref[0:D_NA, :], preferred_element_type=F32)
           + jnp.dot(yml.astype(BF16), wo_ref[D_NA:D_NA + D_ML, :], preferred_element_type=F32))
    h1 = h_ref[...] + mix
    h1_ref[...] = h1
    t = _rms_scale(h1) * gmoe_ref[...]
    t_ref[:, 0:D_MODEL] = t

    t_hi = t.astype(BF16)
    t_lo = (t - t_hi.astype(F32)).astype(BF16)
    logits = (jnp.dot(t_hi, wrh_ref[...], preferred_element_type=F32)
              + jnp.dot(t_lo, wrh_ref[...], preferred_element_type=F32)
              + jnp.dot(t_hi, wrl_ref[...], preferred_element_type=F32)) + br_ref[...]
    lane = lax.broadcasted_iota(jnp.int32, (tm, LANES), 1)
    big = jnp.int32(LANES)

    def top(x):
        mx = jnp.max(x, axis=1, keepdims=True)
        idx = jnp.min(jnp.where(x == mx, lane, big), axis=1, keepdims=True)
        return mx, idx

    lg = jnp.where(lane < N_GROUPS, logits, -jnp.inf)
    g_max, g_idx = top(lg)
    pg_top = 1.0 / jnp.sum(jnp.exp(lg - g_max), axis=1, keepdims=True)
    lo = EXPERT_LANE0 + g_idx * EXPERTS_PER_GROUP
    le = jnp.where((lane >= lo) & (lane < lo + EXPERTS_PER_GROUP), logits, -jnp.inf)
    e1_max, i1 = top(le)
    e_sum = jnp.sum(jnp.exp(le - e1_max), axis=1, keepdims=True)
    e2_max, i2 = top(jnp.where(lane == i1, -jnp.inf, le))
    p1 = 1.0 / e_sum
    p2 = jnp.exp(e2_max - e1_max) / e_sum
    w1 = pg_top * p1 / (p1 + p2)
    w2 = pg_top * p2 / (p1 + p2)
    e1 = i1 - EXPERT_LANE0
    e2 = i2 - EXPERT_LANE0

    la = jnp.minimum(e1, e2) - g_idx * EXPERTS_PER_GROUP
    lb = jnp.maximum(e1, e2) - g_idx * EXPERTS_PER_GROUP
    cls = g_idx * PAIRS_PER_GROUP + lax.shift_right_logical(la * (2 * EXPERTS_PER_GROUP - 1 - la), 1) + (lb - la - 1)
    w_lo = jnp.where(e1 < e2, w1, w2)
    w_hi = jnp.where(e1 < e2, w2, w1)
    oh = (lane == cls).astype(F32)
    before = jnp.dot(tri_ref[...], oh.astype(BF16), preferred_element_type=F32) + cnt_ref[...]
    rank = jnp.sum(oh * before, axis=1, keepdims=True)
    cnt_ref[...] = cnt_ref[...] + jnp.sum(oh, axis=0, keepdims=True)
    info = jnp.zeros((tm, LANES), F32)
    for j, cval in enumerate([cls.astype(F32), rank, w_lo, w_hi]):
        info = jnp.where(lane == j, cval, info)
    t_ref[:, D_MODEL:D_MODEL + LANES] = info


def _mix_out(h, yna, hml, oml, g_ml, w_out, g_moe, w_rg, b_rg, w_re, b_re):
    t = h.shape[0]
    tm = ROW_TILE
    wr = jnp.zeros((D_MODEL, LANES), F32)
    wr = wr.at[:, :N_GROUPS].set(w_rg).at[:, EXPERT_LANE0:EXPERT_LANE0 + N_EXPERTS].set(w_re)
    wrh = wr.astype(BF16)
    wrl = (wr - wrh.astype(F32)).astype(BF16)
    br = jnp.zeros((1, LANES), F32)
    br = br.at[0, :N_GROUPS].set(b_rg).at[0, EXPERT_LANE0:EXPERT_LANE0 + N_EXPERTS].set(b_re)
    tri = jnp.asarray(np.tril(np.ones((tm, tm), np.float32), -1), BF16)
    full = lambda shape: pl.BlockSpec(shape, lambda i: (0,) * len(shape))
    row_blk = lambda w: pl.BlockSpec((tm, w), lambda i: (i, 0))
    return pl.pallas_call(
        _mix_out_kernel,
        grid=(t // tm,),
        in_specs=[row_blk(D_MODEL), row_blk(D_NA), row_blk(D_ML), row_blk(D_ML), full((1, D_ML)),
                  full((D_MODEL, D_MODEL)), full((1, D_MODEL)), full((D_MODEL, LANES)), full((D_MODEL, LANES)),
                  full((1, LANES)), full((tm, tm))],
        out_specs=[row_blk(D_MODEL), row_blk(ROUTED_WIDTH), full((1, LANES))],
        out_shape=[jax.ShapeDtypeStruct((t, D_MODEL), F32), jax.ShapeDtypeStruct((t, ROUTED_WIDTH), F32),
                   jax.ShapeDtypeStruct((1, LANES), F32)],
        compiler_params=_params(1),
        name="mix_out",
    )(h, yna, hml, oml, g_ml.reshape(1, D_ML), w_out.astype(BF16), g_moe.reshape(1, D_MODEL), wrh, wrl, br, tri)


def _dispatch_kernel(zflag_ref, dest_ref, t_ref, xs_ref, zbuf, ring, sem, zsem, *, n_blocks, n_tiles):
    tm = t_ref.shape[0]

    @pl.when(pl.program_id(0) == 0)
    def _():
        zbuf[...] = jnp.zeros_like(zbuf)

        def zero_copy(b):
            return pltpu.make_async_copy(zbuf, xs_ref.at[pl.ds(pl.multiple_of(b * MOE_BLOCK, MOE_BLOCK), MOE_BLOCK)], zsem)

        def zissue(b, carry):
            @pl.when(zflag_ref[b] != 0)
            def _():
                zero_copy(b).start()
            return carry

        def zwait(b, carry):
            @pl.when(zflag_ref[b] != 0)
            def _():
                zero_copy(b).wait()
            return carry

        lax.fori_loop(0, n_blocks, zissue, 0)
        lax.fori_loop(0, n_blocks, zwait, 0)

    i = pl.program_id(0)
    slot = i % 2

    def wait_slot(s):
        pltpu.make_async_copy(ring.at[s], xs_ref.at[pl.ds(0, tm)], sem.at[s]).wait()

    @pl.when(i >= 2)
    def _():
        wait_slot(slot)

    ring[slot] = t_ref[...]

    def issue(it, carry):
        for u in range(ISSUE_UNROLL):
            j = it * ISSUE_UNROLL + u
            pltpu.make_async_copy(ring.at[slot, pl.ds(j, 1)], xs_ref.at[pl.ds(dest_ref[0, 0, j], 1)], sem.at[slot]).start()
        return carry

    lax.fori_loop(0, tm // ISSUE_UNROLL, issue, 0)

    @pl.when(i == n_tiles - 1)
    def _():
        wait_slot(slot)

        @pl.when(i >= 1)
        def _():
            wait_slot(1 - slot)


def _dispatch(t, dest, zflag, n_blocks):
    tt = t.shape[0]
    tm = ROW_TILE
    n_tiles = tt // tm
    return pl.pallas_call(
        functools.partial(_dispatch_kernel, n_blocks=n_blocks, n_tiles=n_tiles),
        grid_spec=pltpu.PrefetchScalarGridSpec(
            num_scalar_prefetch=1,
            grid=(n_tiles,),
            in_specs=[pl.BlockSpec((1, 1, tm), lambda i, zf: (i, 0, 0), memory_space=pltpu.SMEM),
                      pl.BlockSpec((tm, ROUTED_WIDTH), lambda i, zf: (i, 0))],
            out_specs=pl.BlockSpec(memory_space=pl.ANY),
            scratch_shapes=[pltpu.VMEM((MOE_BLOCK, ROUTED_WIDTH), F32), pltpu.VMEM((2, tm, ROUTED_WIDTH), F32),
                            pltpu.SemaphoreType.DMA((2,)), pltpu.SemaphoreType.DMA],
        ),
        out_shape=jax.ShapeDtypeStruct((n_blocks * MOE_BLOCK, ROUTED_WIDTH), F32),
        compiler_params=_params(1),
        name="dispatch",
    )(zflag, dest.reshape(n_tiles, 1, tm), t)


def _ffn_kernel(ea_ref, eb_ref, nu_ref, x_ref, wga_ref, wua_ref, wda_ref, wgb_ref, wub_ref, wdb_ref, y_ref):
    del ea_ref, eb_ref

    def swiglu(x, wg_ref, wu_ref, wd_ref):
        g = jnp.dot(x, wg_ref[0, 0].astype(BF16), preferred_element_type=F32)
        u = jnp.dot(x, wu_ref[0, 0].astype(BF16), preferred_element_type=F32)
        hid = (g * _sigmoid(g) * u).astype(BF16)
        return jnp.dot(hid, wd_ref[0, 0].astype(BF16), preferred_element_type=F32)

    @pl.when(pl.program_id(0) < nu_ref[0])
    def _():
        x = x_ref[:, 0:D_MODEL].astype(BF16)
        y_ref[...] = (x_ref[:, INFO_W_LO:INFO_W_LO + 1] * swiglu(x, wga_ref, wua_ref, wda_ref)
                      + x_ref[:, INFO_W_HI:INFO_W_HI + 1] * swiglu(x, wgb_ref, wub_ref, wdb_ref))

    @pl.when(pl.program_id(0) >= nu_ref[0])
    def _():
        y_ref[...] = jnp.zeros_like(y_ref)


def _expert_ffn(xs, block_ea, block_eb, n_used, w_gate, w_up, w_down, layer):
    n_rows = xs.shape[0]
    nb = n_rows // MOE_BLOCK
    wg, wu, wd = w_gate, w_up, w_down
    blk = lambda i, ea, eb, nu: (jnp.minimum(i, nu[0] - 1), 0)
    sel_a = lambda i, ea, eb, nu: (layer, ea[jnp.minimum(i, nu[0] - 1)], 0, 0)
    sel_b = lambda i, ea, eb, nu: (layer, eb[jnp.minimum(i, nu[0] - 1)], 0, 0)
    up_spec = lambda sel: pl.BlockSpec((1, 1, D_MODEL, D_EXPERT), sel)
    down_spec = lambda sel: pl.BlockSpec((1, 1, D_EXPERT, D_MODEL), sel)
    return pl.pallas_call(
        _ffn_kernel,
        grid_spec=pltpu.PrefetchScalarGridSpec(
            num_scalar_prefetch=3,
            grid=(nb,),
            in_specs=[pl.BlockSpec((MOE_BLOCK, ROUTED_WIDTH), blk),
                      up_spec(sel_a), up_spec(sel_a), down_spec(sel_a),
                      up_spec(sel_b), up_spec(sel_b), down_spec(sel_b)],
            out_specs=pl.BlockSpec((MOE_BLOCK, D_MODEL), lambda i, ea, eb, nu: (i, 0)),
        ),
        out_shape=jax.ShapeDtypeStruct((n_rows, D_MODEL), F32),
        compiler_params=_params(1),
        name="expert_ffn",
    )(block_ea, block_eb, n_used, xs, wg, wu, wd, wg, wu, wd)


def _combine_kernel(dcur_ref, dnxt_ref, h1_ref, p_ref, gple_ref, wple_ref, wpg_ref, gfin_ref, ys_ref,
                    o_ref, buf, sem, *, final, n_tiles):
    i = pl.program_id(0)
    tm = h1_ref.shape[0]
    slot = i % 2

    def issue_tile(d_ref, s):
        def issue(it, carry):
            for u in range(ISSUE_UNROLL):
                j = it * ISSUE_UNROLL + u
                pltpu.make_async_copy(ys_ref.at[pl.ds(d_ref[0, 0, j], 1)], buf.at[s, pl.ds(j, 1)], sem.at[s]).start()
            return carry

        lax.fori_loop(0, tm // ISSUE_UNROLL, issue, 0)

    @pl.when(i == 0)
    def _():
        issue_tile(dcur_ref, 0)

    @pl.when(i + 1 < n_tiles)
    def _():
        issue_tile(dnxt_ref, 1 - slot)

    pltpu.make_async_copy(ys_ref.at[pl.ds(0, tm)], buf.at[slot], sem.at[slot]).wait()

    h2 = h1_ref[...] + buf[slot]
    a = (_rms_scale(h2) * gple_ref[...]).astype(BF16)
    gate = _sigmoid(jnp.dot(a, wpg_ref[...], preferred_element_type=F32))
    emb = jnp.dot(p_ref[0].astype(BF16), wple_ref[...], preferred_element_type=F32)
    h3 = h2 + emb * gate
    if final:
        h3 = _rms_scale(h3) * gfin_ref[...]
    o_ref[...] = h3


def _combine(h1, ys, dest, p_all, layer, g_ple, w_ple, w_pg, g_final, final):
    tt = h1.shape[0]
    tm = ROW_TILE
    n_tiles = tt // tm
    dest3 = dest.reshape(n_tiles, 1, tm)
    full = lambda shape: pl.BlockSpec(shape, lambda i: (0,) * len(shape))
    row_blk = lambda w: pl.BlockSpec((tm, w), lambda i: (i, 0))
    return pl.pallas_call(
        functools.partial(_combine_kernel, final=final, n_tiles=n_tiles),
        grid=(n_tiles,),
        in_specs=[pl.BlockSpec((1, 1, tm), lambda i: (i, 0, 0), memory_space=pltpu.SMEM),
                  pl.BlockSpec((1, 1, tm), lambda i: (jnp.minimum(i + 1, n_tiles - 1), 0, 0), memory_space=pltpu.SMEM),
                  row_blk(D_MODEL), pl.BlockSpec((1, tm, D_PLE), lambda i: (layer, i, 0)), full((1, D_MODEL)),
                  full((D_PLE, D_MODEL)), full((D_MODEL, D_MODEL)), full((1, D_MODEL)),
                  pl.BlockSpec(memory_space=pl.ANY)],
        out_specs=row_blk(D_MODEL),
        out_shape=jax.ShapeDtypeStruct((tt, D_MODEL), F32),
        scratch_shapes=[pltpu.VMEM((2, tm, D_MODEL), F32), pltpu.SemaphoreType.DMA((2,))],
        compiler_params=_params(1),
        name="combine",
    )(dest3, dest3, h1, p_all, g_ple.reshape(1, D_MODEL), w_ple.astype(BF16), w_pg.astype(BF16),
      g_final.reshape(1, D_MODEL), ys)


def _class_experts():
    lo, hi = [], []
    for g in range(N_GROUPS):
        for a in range(EXPERTS_PER_GROUP):
            for b in range(a + 1, EXPERTS_PER_GROUP):
                lo.append(g * EXPERTS_PER_GROUP + a)
                hi.append(g * EXPERTS_PER_GROUP + b)
    return np.asarray(lo, np.int32), np.asarray(hi, np.int32)


def _routing_tables(routed, counts_f, n_blocks):
    cls = routed[:, D_MODEL].astype(jnp.int32)
    rank = routed[:, D_MODEL + 1].astype(jnp.int32)
    counts = counts_f[0, :N_CLASSES].astype(jnp.int32)
    padded = ((counts + MOE_BLOCK - 1) // MOE_BLOCK) * MOE_BLOCK
    pad_end = jnp.cumsum(padded)
    pad_start = pad_end - padded
    ids = jnp.arange(N_CLASSES, dtype=jnp.int32)
    dest = jnp.sum(jnp.where(cls[:, None] == ids, pad_start, 0), axis=-1) + rank
    n_used = (pad_end[-1] // MOE_BLOCK).astype(jnp.int32).reshape(1)
    starts = jnp.arange(n_blocks, dtype=jnp.int32) * MOE_BLOCK
    block_cls = jnp.minimum(jnp.sum((pad_end[None, :] <= starts[:, None]).astype(jnp.int32), axis=1), N_CLASSES - 1)
    pick = lambda table: jnp.sum(jnp.where(block_cls[:, None] == ids, table, 0), axis=-1)
    cls_lo, cls_hi = _class_experts()
    filled = jnp.clip(pick(counts) - (starts - pick(pad_start)), 0, MOE_BLOCK)
    zflag = (filled < MOE_BLOCK).astype(jnp.int32)
    return dest, pick(jnp.asarray(cls_lo)), pick(jnp.asarray(cls_hi)), n_used, zflag


def kernel(x, p, w_in, b_gate, conv_w, conv_b, rpb, g_na, g_ml, w_out, g_mix, g_moe, w_route_group, b_route_group, w_route_expert, b_route_expert, w_exp_gate, w_exp_up, w_exp_down, g_ple, w_ple, w_ple_gate, g_final):
    batch, seq, _ = x.shape
    depth = w_in.shape[0]
    tt = batch * seq
    assert seq % ROW_TILE == 0 and seq % GRID_W == 0
    n_blocks = tt // MOE_BLOCK + N_CLASSES
    h = x.reshape(tt, D_MODEL)
    for i in range(depth):
        qna, kna, vna, qml, kt_ml, vml, oml, g_col, g_row = _in_proj(h, seq, g_mix[i], w_in[i], b_gate[i], conv_w[i], conv_b[i])
        yna = _na_attention(qna, kna, vna, _na_bias_table(rpb[i], seq // GRID_W), g_na[i], batch, seq)
        hml = _mlstm(qml, kt_ml, vml, g_row, g_col, batch, seq)
        h1, routed, counts = _mix_out(h, yna, hml, oml, g_ml[i], w_out[i], g_moe[i],
                                      w_route_group[i], b_route_group[i], w_route_expert[i], b_route_expert[i])
        dest, block_ea, block_eb, n_used, zflag = _routing_tables(routed, counts, n_blocks)
        xs = _dispatch(routed, dest, zflag, n_blocks)
        ys = _expert_ffn(xs, block_ea, block_eb, n_used, w_exp_gate, w_exp_up, w_exp_down, i)
        h = _combine(h1, ys, dest, p.reshape(depth, tt, D_PLE), i, g_ple[i], w_ple[i], w_ple_gate[i], g_final,
                     final=(i == depth - 1))
    return h.reshape(batch, seq, D_MODEL)
```

```python
import functools

import numpy as np
import jax
import jax.numpy as jnp
from jax import lax
from jax.experimental import pallas as pl
from jax.experimental.pallas import tpu as pltpu

F32 = jnp.float32
BF16 = jnp.bfloat16

D_MODEL = 1024
GRID_W = 64
D_NA = 512
NA_HEADS = 8
NA_HEAD_DIM = 64
NA_WIN_H = 8
NA_WIN_W = 16
D_ML = 512
ML_HEADS = 4
ML_HEAD_DIM = 128
ML_CONV = 5
ML_CHUNK = 128
N_GROUPS = 4
EXPERTS_PER_GROUP = 8
N_EXPERTS = 32
D_EXPERT = 512
D_PLE = 256
EPS = 1e-6
NEG_INF = -1e30

LANES = 128
SUBLANES = 8
ROW_TILE = 512
MOE_BLOCK = 256
PAIRS_PER_GROUP = EXPERTS_PER_GROUP * (EXPERTS_PER_GROUP - 1) // 2
N_CLASSES = N_GROUPS * PAIRS_PER_GROUP
ROUTED_WIDTH = D_MODEL + LANES
INFO_W_LO = D_MODEL + 2
INFO_W_HI = D_MODEL + 3
HALO = SUBLANES
ISSUE_UNROLL = 8
EXPERT_LANE0 = 8
GATE_LANES_PER_DIR = 3 * ML_HEADS
VMEM_LIMIT = 56 * 1024 * 1024


def _params(n_axes, flags=None):
    return pltpu.CompilerParams(dimension_semantics=("arbitrary",) * n_axes, vmem_limit_bytes=VMEM_LIMIT, flags=flags)


def _sigmoid(x):
    return 1.0 / (1.0 + jnp.exp(-x))


def _log_sigmoid(x):
    return jnp.minimum(x, 0.0) - jnp.log1p(jnp.exp(-jnp.abs(x)))


def _rms_scale(x):
    return x * lax.rsqrt(jnp.mean(x * x, axis=-1, keepdims=True) + EPS)


def _scan_chunk(x, axis, reverse, op, identity):
    n = x.shape[axis]
    idx = lax.broadcasted_iota(jnp.int32, x.shape, axis)
    sh = 1
    while sh < n:
        if reverse:
            x = op(x, jnp.where(idx < n - sh, pltpu.roll(x, n - sh, axis), identity))
        else:
            x = op(x, jnp.where(idx >= sh, pltpu.roll(x, sh, axis), identity))
        sh *= 2
    return x


def _in_proj_kernel(hp_ref, h_ref, hn_ref, g_ref, wna_ref, wqk_ref, wvo_ref, wgr_ref, bgr_ref, cw_ref, cb_ref,
                    qna_ref, kna_ref, vna_ref, qml_ref, kt_ref, vml_ref, oml_ref, gc_ref, gr_ref,
                    zbuf, *, tiles_per_seq):
    i = pl.program_id(0)
    tm = h_ref.shape[0]
    g = g_ref[...]

    def norm(x):
        return (_rms_scale(x) * g).astype(BF16)

    a = norm(h_ref[...])

    pos = i % tiles_per_seq
    zp = jnp.dot(norm(hp_ref[...]), wqk_ref[...], preferred_element_type=F32)
    zn = jnp.dot(norm(hn_ref[...]), wqk_ref[...], preferred_element_type=F32)
    zbuf[0:HALO, :] = jnp.where(pos == 0, 0.0, zp)
    zbuf[HALO:HALO + tm, :] = jnp.dot(a, wqk_ref[...], preferred_element_type=F32)
    zbuf[HALO + tm:2 * HALO + tm, :] = jnp.where(pos == tiles_per_seq - 1, 0.0, zn)

    ng = 2 * GATE_LANES_PER_DIR
    nt = (((1,), (1,)), ((), ()))
    gi = lax.dot_general(wgr_ref[0:ng, :], a, nt, preferred_element_type=F32) + bgr_ref[0:ng, :]
    gf = lax.dot_general(wgr_ref[ng:2 * ng, :], a, nt, preferred_element_type=F32) + bgr_ref[ng:2 * ng, :]
    row = lax.broadcasted_iota(jnp.int32, (ng, ML_CHUNK), 0)
    is_fw = row < GATE_LANES_PER_DIR
    qsel = jnp.where(is_fw, row, row - GATE_LANES_PER_DIR)
    pad_rows = jnp.zeros((ML_CHUNK - ng, ML_CHUNK), F32)

    def gates(c):
        cols_c = slice(c * ML_CHUNK, (c + 1) * ML_CHUNK)
        lf = _log_sigmoid(gf[:, cols_c])
        b = jnp.where(is_fw, _scan_chunk(lf, 1, False, jnp.add, 0.0), _scan_chunk(lf, 1, True, jnp.add, 0.0))
        r = gi[:, cols_c] - b
        cmax = jnp.where(is_fw, _scan_chunk(r, 1, False, jnp.maximum, -jnp.inf),
                         _scan_chunk(r, 1, True, jnp.maximum, -jnp.inf))
        packed = jnp.where(qsel < ML_HEADS, r, jnp.where(qsel < 2 * ML_HEADS, cmax, b))
        gc_ref[cols_c, :] = jnp.concatenate([packed, pad_rows], axis=0).T
        gr_ref[:, cols_c] = jnp.concatenate([packed[0:ML_HEADS], packed[GATE_LANES_PER_DIR:GATE_LANES_PER_DIR + ML_HEADS]], axis=0)

    def conv(c0, cstep=256):
        acc = cb_ref[:, c0:c0 + cstep]
        for j in range(ML_CONV):
            r0 = HALO - ML_CONV // 2 + j
            acc = acc + zbuf[r0:r0 + tm, c0:c0 + cstep] * cw_ref[j:j + 1, c0:c0 + cstep]
        y = acc * _sigmoid(acc)
        if c0 < D_ML:
            qml_ref[:, c0:c0 + cstep] = (y * (ML_HEAD_DIM ** -0.5)).astype(BF16)
        else:
            kt_ref[c0 - D_ML:c0 - D_ML + cstep, :] = y.T.astype(BF16)

    def proj(w_ref, c0, scale=None):
        z = jnp.dot(a, w_ref[:, c0:c0 + D_NA], preferred_element_type=F32)
        return (z if scale is None else z * scale).astype(BF16)

    conv(0)
    qna_ref[...] = proj(wna_ref, 0, NA_HEAD_DIM ** -0.5)
    conv(256)
    kna_ref[...] = proj(wna_ref, D_NA)
    conv(512)
    vna_ref[...] = proj(wna_ref, 2 * D_NA)
    conv(768)
    vml_ref[...] = proj(wvo_ref, 0)
    for c in range(tm // ML_CHUNK):
        gates(c)
    oml_ref[...] = proj(wvo_ref, D_ML)


def _in_proj(h, seq, g_mix, w_in, b_gate, conv_w, conv_b):
    t = h.shape[0]
    tm = ROW_TILE
    n_tiles = t // tm
    tiles_per_seq = seq // tm
    nh = 4 * ML_HEADS
    o_qk = 3 * D_NA
    o_v = o_qk + 2 * D_ML
    o_g = o_v + 2 * D_ML
    wna = w_in[:, :o_qk].astype(BF16)
    wqk = w_in[:, o_qk:o_v].astype(BF16)
    wvo = w_in[:, o_v:o_g].astype(BF16)
    wg = w_in[:, o_g:o_g + nh]
    ng = 2 * GATE_LANES_PER_DIR
    gate_id = np.arange(ng)
    src_i = (gate_id // GATE_LANES_PER_DIR) * 2 * ML_HEADS + gate_id % ML_HEADS
    wgr = jnp.concatenate([wg[:, src_i], wg[:, src_i + ML_HEADS]], axis=1).T.astype(BF16)
    bgr = jnp.concatenate([b_gate[src_i], b_gate[src_i + ML_HEADS]]).reshape(2 * ng, 1)
    cw = jnp.pad(conv_w, ((0, SUBLANES - ML_CONV), (0, 0)))
    cb = conv_b.reshape(1, 2 * D_ML)
    full = lambda shape: pl.BlockSpec(shape, lambda i: (0,) * len(shape))
    row_blk = lambda w: pl.BlockSpec((tm, w), lambda i: (i, 0))
    hb = tm // HALO
    out_bf = jax.ShapeDtypeStruct((t, D_NA), BF16)
    return pl.pallas_call(
        functools.partial(_in_proj_kernel, tiles_per_seq=tiles_per_seq),
        grid=(n_tiles,),
        in_specs=[
            pl.BlockSpec((HALO, D_MODEL), lambda i: (jnp.maximum(i * hb - 1, 0), 0)),
            row_blk(D_MODEL),
            pl.BlockSpec((HALO, D_MODEL), lambda i: (jnp.minimum((i + 1) * hb, t // HALO - 1), 0)),
            full((1, D_MODEL)), full((D_MODEL, o_qk)), full((D_MODEL, 2 * D_ML)), full((D_MODEL, 2 * D_ML)),
            full((2 * ng, D_MODEL)), full((2 * ng, 1)),
            full((SUBLANES, 2 * D_ML)), full((1, 2 * D_ML)),
        ],
        out_specs=[row_blk(D_NA)] * 4 + [pl.BlockSpec((D_ML, tm), lambda i: (0, i))] + [row_blk(D_NA)] * 2
        + [row_blk(LANES), pl.BlockSpec((2 * ML_HEADS, tm), lambda i: (0, i))],
        out_shape=[out_bf] * 4 + [jax.ShapeDtypeStruct((D_ML, t), BF16)] + [out_bf] * 2
        + [jax.ShapeDtypeStruct((t, LANES), F32), jax.ShapeDtypeStruct((2 * ML_HEADS, t), F32)],
        scratch_shapes=[pltpu.VMEM((tm + 2 * HALO, 2 * D_ML), F32)],
        compiler_params=_params(1),
        name="in_proj",
    )(h, h, h, g_mix.reshape(1, D_MODEL), wna, wqk, wvo, wgr, bgr, cw, cb)


def _na_bias_table(rpb, rows):
    kh = min(NA_WIN_H, rows)
    o = np.arange(kh)[:, None]
    kr = np.arange(kh)[None, :]
    dr = kr - o + (NA_WIN_H - 1)
    qc = np.arange(GRID_W)[:, None]
    kc = np.arange(GRID_W)[None, :]
    cs = np.clip(qc - NA_WIN_W // 2, 0, GRID_W - NA_WIN_W)
    inw = (kc >= cs) & (kc < cs + NA_WIN_W)
    dc = np.clip(kc - qc, 1 - NA_WIN_W, NA_WIN_W - 1) + (NA_WIN_W - 1)
    sel_r = jnp.asarray(dr[:, :, None] == np.arange(2 * NA_WIN_H - 1), F32)
    sel_c = jnp.asarray(dc[:, :, None] == np.arange(2 * NA_WIN_W - 1), F32)
    rows_sel = jnp.einsum("hrc,okr->hokc", rpb.astype(F32), sel_r, precision=lax.Precision.HIGHEST)
    b = jnp.einsum("hokc,qlc->hoqkl", rows_sel, sel_c, precision=lax.Precision.HIGHEST)
    b = jnp.where(inw[None, None, :, None, :], b, NEG_INF)
    return b.reshape(rpb.shape[0], kh, GRID_W, kh * GRID_W)


NA_HEADS_PER_BLOCK = 4
NA_BAND_ROWS = 16
NA_UNROLL = 8


def _na_kernel(q_ref, k_ref, v_ref, bias_ref, g_ref, o_ref, *, rows, kh, band):
    nkeys = kh * GRID_W
    hpb = NA_HEADS_PER_BLOCK
    m_rows = hpb * GRID_W
    width = hpb * NA_HEAD_DIM
    band_i = pl.program_id(2)
    lane_head = lax.shift_right_logical(lax.broadcasted_iota(jnp.int32, (m_rows, width), 1), 6)
    row_head = lax.shift_right_logical(lax.broadcasted_iota(jnp.int32, (m_rows, width), 0), 6)
    own = lane_head == row_head
    g = g_ref[...]

    def one_row(rl):
        r = band_i * band + rl
        rs = jnp.clip(r - kh // 2, 0, rows - kh)
        off = r - rs
        q = q_ref[0, pl.ds(pl.multiple_of(rl * GRID_W, GRID_W), GRID_W), :]
        kb = k_ref[0, pl.ds(pl.multiple_of(rs * GRID_W, GRID_W), nkeys), :]
        vb = v_ref[0, pl.ds(pl.multiple_of(rs * GRID_W, GRID_W), nkeys), :]
        qs = jnp.where(own, jnp.concatenate([q] * hpb, axis=0), jnp.zeros((m_rows, width), BF16))
        s = lax.dot_general(qs, kb, (((1,), (1,)), ((), ())), preferred_element_type=F32)
        s = s + bias_ref[:, off].reshape(m_rows, nkeys)
        m = jnp.max(s, axis=-1, keepdims=True)
        p = jnp.exp(s - m)
        l = jnp.sum(p, axis=-1, keepdims=True)
        pv = jnp.dot(p.astype(BF16), vb, preferred_element_type=F32)
        oh = jnp.where(own, pv * (1.0 / l), 0.0)
        ms = jnp.sum(oh * oh, axis=-1, keepdims=True) * (1.0 / NA_HEAD_DIM)
        yn = oh * lax.rsqrt(ms + EPS)
        y = yn[0:GRID_W]
        for hh in range(1, hpb):
            y = y + yn[hh * GRID_W:(hh + 1) * GRID_W]
        o_ref[0, pl.ds(pl.multiple_of(rl * GRID_W, GRID_W), GRID_W), :] = (y * g).astype(BF16)

    def body(it, carry):
        for u in range(NA_UNROLL):
            one_row(it * NA_UNROLL + u)
        return carry

    lax.fori_loop(0, band // NA_UNROLL, body, 0)


def _na_attention(q, k, v, bias, g_na, batch, seq):
    rows = seq // GRID_W
    kh = min(NA_WIN_H, rows)
    hpb = NA_HEADS_PER_BLOCK
    width = hpb * NA_HEAD_DIM
    band = min(NA_BAND_ROWS, rows)
    assert rows % band == 0 and band % NA_UNROLL == 0
    q3, k3, v3 = (x.reshape(batch, seq, D_NA) for x in (q, k, v))
    seq_blk = pl.BlockSpec((1, seq, width), lambda b, j, r: (b, 0, j))
    band_blk = pl.BlockSpec((1, band * GRID_W, width), lambda b, j, r: (b, r, j))
    out = pl.pallas_call(
        functools.partial(_na_kernel, rows=rows, kh=kh, band=band),
        grid=(batch, D_NA // width, rows // band),
        in_specs=[band_blk, seq_blk, seq_blk,
                  pl.BlockSpec((hpb, kh, GRID_W, kh * GRID_W), lambda b, j, r: (j, 0, 0, 0)),
                  pl.BlockSpec((1, width), lambda b, j, r: (0, j))],
        out_specs=band_blk,
        out_shape=jax.ShapeDtypeStruct((batch, seq, D_NA), BF16),
        compiler_params=_params(3),
        name="na_attn",
    )(q3, k3, v3, bias, g_na.reshape(1, D_NA))
    return out.reshape(batch * seq, D_NA)


ML_UNROLL = 2
ML_HEADS_PER_STEP = 2


def _mlstm_kernel(q_ref, kt_ref, v_ref, gr_ref, gc_ref, o_ref, st_ref, *, n_chunks):
    L = ML_CHUNK
    hps = ML_HEADS_PER_STEP
    lane = lax.broadcasted_iota(jnp.int32, (L, LANES), 1)
    grow_i = lax.broadcasted_iota(jnp.int32, (2 * ML_HEADS, L), 0)
    row_i = lax.broadcasted_iota(jnp.int32, (L, L), 0)
    col_i = lax.broadcasted_iota(jnp.int32, (L, L), 1)
    ones = jnp.ones((L, ML_HEAD_DIM), BF16)

    def step(c, m_prev, hb, reverse):
        off = pl.multiple_of(c * L, L)
        d = 1 if reverse else 0
        hd = pl.program_id(1) * hps + hb
        cols = slice(hb * ML_HEAD_DIM, (hb + 1) * ML_HEAD_DIM)
        qc = q_ref[0, pl.ds(off, L), cols]
        kt = kt_ref[cols, pl.ds(off, L)]
        vc = v_ref[0, pl.ds(off, L), cols]
        gcol = gc_ref[pl.ds(off, L), :]
        grow = gr_ref[:, pl.ds(off, L)]

        def col(qn):
            sel = lane == d * GATE_LANES_PER_DIR + qn * ML_HEADS + hd
            return jnp.sum(jnp.where(sel, gcol, 0.0), axis=1, keepdims=True)

        r_col, cmax_col, b_col = col(0), col(1), col(2)
        r_row = jnp.sum(jnp.where(grow_i == d * ML_HEADS + hd, grow, 0.0), axis=0, keepdims=True)
        end = 0 if reverse else L - 1
        mm_end = jnp.maximum(m_prev, cmax_col[end:end + 1, :])
        m_new = b_col[end:end + 1, :] + mm_end
        mm_col = jnp.maximum(m_prev, cmax_col)
        causal = (col_i >= row_i) if reverse else (col_i <= row_i)
        dmat = jnp.exp(jnp.where(causal, r_row - mm_col, -jnp.inf))
        s = jnp.dot(qc, kt, preferred_element_type=F32) * dmat
        w_inter = jnp.exp(m_prev - mm_col)
        floor = jnp.exp(-(b_col + mm_col))
        w_col = jnp.exp(r_col - mm_end)
        decay = jnp.exp(m_prev - mm_end)
        vext = jnp.concatenate([vc, ones], axis=1)
        state = st_ref[d, hb]
        nd = (jnp.dot(s.astype(BF16), vext, preferred_element_type=F32)
              + w_inter * jnp.dot(qc, state.astype(BF16), preferred_element_type=F32))
        h = nd[:, :ML_HEAD_DIM] / jnp.maximum(jnp.abs(nd[:, ML_HEAD_DIM:]), floor)
        wv = jnp.concatenate([(w_col * vc.astype(F32)).astype(BF16),
                              jnp.broadcast_to(w_col, (L, ML_HEAD_DIM)).astype(BF16)], axis=1)
        st_ref[d, hb] = decay * state + jnp.dot(kt, wv, preferred_element_type=F32)
        return h, m_new

    st_ref[...] = jnp.zeros_like(st_ref)

    def sweep(accumulate):
        def body(it, carry):
            m = list(carry)
            for u in range(ML_UNROLL):
                j = it * ML_UNROLL + u
                jb = n_chunks - 1 - j
                rows_f = pl.ds(pl.multiple_of(j * L, L), L)
                rows_b = pl.ds(pl.multiple_of(jb * L, L), L)
                for hb in range(hps):
                    cols = slice(hb * ML_HEAD_DIM, (hb + 1) * ML_HEAD_DIM)
                    h_f, m[hb] = step(j, m[hb], hb, False)
                    h_b, m[hps + hb] = step(jb, m[hps + hb], hb, True)
                    if accumulate:
                        o_ref[0, rows_f, cols] += h_f
                        o_ref[0, rows_b, cols] += h_b
                    else:
                        o_ref[0, rows_f, cols] = h_f
                        o_ref[0, rows_b, cols] = h_b
            return tuple(m)
        return body

    half = n_chunks // (2 * ML_UNROLL)
    zero = jnp.zeros((1, 1), F32)
    carry = lax.fori_loop(0, half, sweep(False), (zero,) * (2 * hps))
    lax.fori_loop(half, 2 * half, sweep(True), carry)


def _mlstm(q, kt, v, g_row, g_col, batch, seq):
    n_chunks = seq // ML_CHUNK
    assert n_chunks % (2 * ML_UNROLL) == 0
    hps = ML_HEADS_PER_STEP
    width = hps * ML_HEAD_DIM
    q3, v3 = (x.reshape(batch, seq, D_ML) for x in (q, v))
    seq_blk = pl.BlockSpec((1, seq, width), lambda b, j: (b, 0, j))
    once_per_batch = dict(pipeline_mode=pl.Buffered(1))
    out = pl.pallas_call(
        functools.partial(_mlstm_kernel, n_chunks=n_chunks),
        grid=(batch, ML_HEADS // hps),
        in_specs=[seq_blk, pl.BlockSpec((width, seq), lambda b, j: (j, b)), seq_blk,
                  pl.BlockSpec((2 * ML_HEADS, seq), lambda b, j: (0, b), **once_per_batch),
                  pl.BlockSpec((seq, LANES), lambda b, j: (b, 0), **once_per_batch)],
        out_specs=seq_blk,
        out_shape=jax.ShapeDtypeStruct((batch, seq, D_ML), F32),
        scratch_shapes=[pltpu.VMEM((2, hps, ML_HEAD_DIM, 2 * ML_HEAD_DIM), F32)],
        compiler_params=_params(2),
        name="mlstm",
    )(q3, kt, v3, g_row, g_col)
    return out.reshape(batch * seq, D_ML)


def _mix_out_kernel(h_ref, yna_ref, hml_ref, oml_ref, gml_ref, wo_ref, gmoe_ref, wrh_ref, wrl_ref, br_ref, tri_ref,
                    h1_ref, t_ref, cnt_ref):
    i = pl.program_id(0)
    tm = h_ref.shape[0]

    @pl.when(i == 0)
    def _():
        cnt_ref[...] = jnp.zeros_like(cnt_ref)

    hml = hml_ref[...]
    parts = [_rms_scale(hml[:, d * ML_HEAD_DIM:(d + 1) * ML_HEAD_DIM]) for d in range(ML_HEADS)]
    yml = _sigmoid(oml_ref[...].astype(F32)) * (jnp.concatenate(parts, axis=1) * gml_ref[...])
    mix = (jnp.dot(yna_ref[...], wo_ref[0:D_NA, :], preferred_element_type=F32)
           + jnp.dot(yml.astype(BF16), wo_ref[D_NA:D_NA + D_ML, :], preferred_element_type=F32))
    h1 = h_ref[...] + mix
    h1_ref[...] = h1
    t = _rms_scale(h1) * gmoe_ref[...]
    t_ref[:, 0:D_MODEL] = t

    t_hi = t.astype(BF16)
    t_lo = (t - t_hi.astype(F32)).astype(BF16)
    logits = (jnp.dot(t_hi, wrh_ref[...], preferred_element_type=F32)
              + jnp.dot(t_lo, wrh_ref[...], preferred_element_type=F32)
              + jnp.dot(t_hi, wrl_ref[...], preferred_element_type=F32)) + br_ref[...]
    lane = lax.broadcasted_iota(jnp.int32, (tm, LANES), 1)
    big = jnp.int32(LANES)

    def top(x):
        mx = jnp.max(x, axis=1, keepdims=True)
        idx = jnp.min(jnp.where(x == mx, lane, big), axis=1, keepdims=True)
        return mx, idx

    lg = jnp.where(lane < N_GROUPS, logits, -jnp.inf)
    g_max, g_idx = top(lg)
    pg_top = 1.0 / jnp.sum(jnp.exp(lg - g_max), axis=1, keepdims=True)
    lo = EXPERT_LANE0 + g_idx * EXPERTS_PER_GROUP
    le = jnp.where((lane >= lo) & (lane < lo + EXPERTS_PER_GROUP), logits, -jnp.inf)
    e1_max, i1 = top(le)
    e_sum = jnp.sum(jnp.exp(le - e1_max), axis=1, keepdims=True)
    e2_max, i2 = top(jnp.where(lane == i1, -jnp.inf, le))
    p1 = 1.0 / e_sum
    p2 = jnp.exp(e2_max - e1_max) / e_sum
    w1 = pg_top * p1 / (p1 + p2)
    w2 = pg_top * p2 / (p1 + p2)
    e1 = i1 - EXPERT_LANE0
    e2 = i2 - EXPERT_LANE0

    la = jnp.minimum(e1, e2) - g_idx * EXPERTS_PER_GROUP
    lb = jnp.maximum(e1, e2) - g_idx * EXPERTS_PER_GROUP
    cls = g_idx * PAIRS_PER_GROUP + lax.shift_right_logical(la * (2 * EXPERTS_PER_GROUP - 1 - la), 1) + (lb - la - 1)
    w_lo = jnp.where(e1 < e2, w1, w2)
    w_hi = jnp.where(e1 < e2, w2, w1)
    oh = (lane == cls).astype(F32)
    before = jnp.dot(tri_ref[...], oh.astype(BF16), preferred_element_type=F32) + cnt_ref[...]
    rank = jnp.sum(oh * before, axis=1, keepdims=True)
    cnt_ref[...] = cnt_ref[...] + jnp.sum(oh, axis=0, keepdims=True)
    info = jnp.zeros((tm, LANES), F32)
    for j, cval in enumerate([cls.astype(F32), rank, w_lo, w_hi]):
        info = jnp.where(lane == j, cval, info)
    t_ref[:, D_MODEL:D_MODEL + LANES] = info


def _mix_out(h, yna, hml, oml, g_ml, w_out, g_moe, w_rg, b_rg, w_re, b_re):
    t = h.shape[0]
    tm = ROW_TILE
    wr = jnp.zeros((D_MODEL, LANES), F32)
    wr = wr.at[:, :N_GROUPS].set(w_rg).at[:, EXPERT_LANE0:EXPERT_LANE0 + N_EXPERTS].set(w_re)
    wrh = wr.astype(BF16)
    wrl = (wr - wrh.astype(F32)).astype(BF16)
    br = jnp.zeros((1, LANES), F32)
    br = br.at[0, :N_GROUPS].set(b_rg).at[0, EXPERT_LANE0:EXPERT_LANE0 + N_EXPERTS].set(b_re)
    tri = jnp.asarray(np.tril(np.ones((tm, tm), np.float32), -1), BF16)
    full = lambda shape: pl.BlockSpec(shape, lambda i: (0,) * len(shape))
    row_blk = lambda w: pl.BlockSpec((tm, w), lambda i: (i, 0))
    return pl.pallas_call(
        _mix_out_kernel,
        grid=(t // tm,),
        in_specs=[row_blk(D_MODEL), row_blk(D_NA), row_blk(D_ML), row_blk(D_ML), full((1, D_ML)),
                  full((D_MODEL, D_MODEL)), full((1, D_MODEL)), full((D_MODEL, LANES)), full((D_MODEL, LANES)),
                  full((1, LANES)), full((tm, tm))],
        out_specs=[row_blk(D_MODEL), row_blk(ROUTED_WIDTH), full((1, LANES))],
        out_shape=[jax.ShapeDtypeStruct((t, D_MODEL), F32), jax.ShapeDtypeStruct((t, ROUTED_WIDTH), F32),
                   jax.ShapeDtypeStruct((1, LANES), F32)],
        compiler_params=_params(1),
        name="mix_out",
    )(h, yna, hml, oml, g_ml.reshape(1, D_ML), w_out.astype(BF16), g_moe.reshape(1, D_MODEL), wrh, wrl, br, tri)


def _dispatch_kernel(zflag_ref, dest_ref, t_ref, xs_ref, zbuf, ring, sem, zsem, *, n_blocks, n_tiles):
    tm = t_ref.shape[0]

    @pl.when(pl.program_id(0) == 0)
    def _():
        zbuf[...] = jnp.zeros_like(zbuf)

        def zero_copy(b):
            return pltpu.make_async_copy(zbuf, xs_ref.at[pl.ds(pl.multiple_of(b * MOE_BLOCK, MOE_BLOCK), MOE_BLOCK)], zsem)

        def zissue(b, carry):
            @pl.when(zflag_ref[b] != 0)
            def _():
                zero_copy(b).start()
            return carry

        def zwait(b, carry):
            @pl.when(zflag_ref[b] != 0)
            def _():
                zero_copy(b).wait()
            return carry

        lax.fori_loop(0, n_blocks, zissue, 0)
        lax.fori_loop(0, n_blocks, zwait, 0)

    i = pl.program_id(0)
    slot = i % 2

    def wait_slot(s):
        pltpu.make_async_copy(ring.at[s], xs_ref.at[pl.ds(0, tm)], sem.at[s]).wait()

    @pl.when(i >= 2)
    def _():
        wait_slot(slot)

    ring[slot] = t_ref[...]

    def issue(it, carry):
        for u in range(ISSUE_UNROLL):
            j = it * ISSUE_UNROLL + u
            pltpu.make_async_copy(ring.at[slot, pl.ds(j, 1)], xs_ref.at[pl.ds(dest_ref[0, 0, j], 1)], sem.at[slot]).start()
        return carry

    lax.fori_loop(0, tm // ISSUE_UNROLL, issue, 0)

    @pl.when(i == n_tiles - 1)
    def _():
        wait_slot(slot)

        @pl.when(i >= 1)
        def _():
            wait_slot(1 - slot)


def _dispatch(t, dest, zflag, n_blocks):
    tt = t.shape[0]
    tm = ROW_TILE
    n_tiles = tt // tm
    return pl.pallas_call(
        functools.partial(_dispatch_kernel, n_blocks=n_blocks, n_tiles=n_tiles),
        grid_spec=pltpu.PrefetchScalarGridSpec(
            num_scalar_prefetch=1,
            grid=(n_tiles,),
            in_specs=[pl.BlockSpec((1, 1, tm), lambda i, zf: (i, 0, 0), memory_space=pltpu.SMEM),
                      pl.BlockSpec((tm, ROUTED_WIDTH), lambda i, zf: (i, 0))],
            out_specs=pl.BlockSpec(memory_space=pl.ANY),
            scratch_shapes=[pltpu.VMEM((MOE_BLOCK, ROUTED_WIDTH), F32), pltpu.VMEM((2, tm, ROUTED_WIDTH), F32),
                            pltpu.SemaphoreType.DMA((2,)), pltpu.SemaphoreType.DMA],
        ),
        out_shape=jax.ShapeDtypeStruct((n_blocks * MOE_BLOCK, ROUTED_WIDTH), F32),
        compiler_params=_params(1),
        name="dispatch",
    )(zflag, dest.reshape(n_tiles, 1, tm), t)


def _ffn_kernel(ea_ref, eb_ref, nu_ref, x_ref, wg_hbm, wu_hbm, wd_hbm, y_ref,
                wg_s, wu_s, wd_s, stage_g, stage_u, stage_d, sem, *, layer):
    i = pl.program_id(0)
    used = i < nu_ref[0]
    group = lax.shift_right_logical(ea_ref[i], 3)
    prev_group = lax.shift_right_logical(ea_ref[jnp.maximum(i - 1, 0)], 3)
    mats = ((wg_hbm, stage_g, wg_s), (wu_hbm, stage_u, wu_s), (wd_hbm, stage_d, wd_s))

    @pl.when(used & ((i == 0) | (group != prev_group)))
    def _():
        def fetch(m, e):
            w_hbm, stage, _ = mats[m]
            return pltpu.make_async_copy(w_hbm.at[layer, group * EXPERTS_PER_GROUP + e], stage, sem.at[m])

        for m in range(3):
            fetch(m, 0).start()

        def load_expert(e, carry):
            for m in range(3):
                _, stage, w_s = mats[m]
                fetch(m, e).wait()
                w_s[e] = stage[...].astype(BF16)

                @pl.when(e + 1 < EXPERTS_PER_GROUP)
                def _():
                    fetch(m, e + 1).start()
            return carry

        lax.fori_loop(0, EXPERTS_PER_GROUP, load_expert, 0)

    def swiglu(x, e):
        g = jnp.dot(x, wg_s[e], preferred_element_type=F32)
        u = jnp.dot(x, wu_s[e], preferred_element_type=F32)
        hid = (g * _sigmoid(g) * u).astype(BF16)
        return jnp.dot(hid, wd_s[e], preferred_element_type=F32)

    @pl.when(used)
    def _():
        x = x_ref[:, 0:D_MODEL].astype(BF16)
        y_ref[...] = (x_ref[:, INFO_W_LO:INFO_W_LO + 1] * swiglu(x, ea_ref[i] & (EXPERTS_PER_GROUP - 1))
                      + x_ref[:, INFO_W_HI:INFO_W_HI + 1] * swiglu(x, eb_ref[i] & (EXPERTS_PER_GROUP - 1)))

    @pl.when(jnp.logical_not(used))
    def _():
        y_ref[...] = jnp.zeros_like(y_ref)


def _expert_ffn(xs, block_ea, block_eb, n_used, w_gate, w_up, w_down, layer):
    n_rows = xs.shape[0]
    nb = n_rows // MOE_BLOCK
    blk = lambda i, ea, eb, nu: (jnp.minimum(i, nu[0] - 1), 0)
    hbm = pl.BlockSpec(memory_space=pl.ANY)
    return pl.pallas_call(
        functools.partial(_ffn_kernel, layer=layer),
        grid_spec=pltpu.PrefetchScalarGridSpec(
            num_scalar_prefetch=3,
            grid=(nb,),
            in_specs=[pl.BlockSpec((MOE_BLOCK, ROUTED_WIDTH), blk), hbm, hbm, hbm],
            out_specs=pl.BlockSpec((MOE_BLOCK, D_MODEL), lambda i, ea, eb, nu: (i, 0)),
            scratch_shapes=[pltpu.VMEM((EXPERTS_PER_GROUP, D_MODEL, D_EXPERT), BF16),
                            pltpu.VMEM((EXPERTS_PER_GROUP, D_MODEL, D_EXPERT), BF16),
                            pltpu.VMEM((EXPERTS_PER_GROUP, D_EXPERT, D_MODEL), BF16),
                            pltpu.VMEM((D_MODEL, D_EXPERT), F32), pltpu.VMEM((D_MODEL, D_EXPERT), F32),
                            pltpu.VMEM((D_EXPERT, D_MODEL), F32), pltpu.SemaphoreType.DMA((3,))],
        ),
        out_shape=jax.ShapeDtypeStruct((n_rows, D_MODEL), F32),
        compiler_params=_params(1),
        name="expert_ffn",
    )(block_ea, block_eb, n_used, xs, w_gate, w_up, w_down)


def _combine_kernel(dcur_ref, dnxt_ref, h1_ref, p_ref, gple_ref, wple_ref, wpg_ref, gfin_ref, ys_ref,
                    o_ref, buf, sem, *, final, n_tiles):
    i = pl.program_id(0)
    tm = h1_ref.shape[0]
    slot = i % 2

    def issue_tile(d_ref, s):
        def issue(it, carry):
            for u in range(ISSUE_UNROLL):
                j = it * ISSUE_UNROLL + u
                pltpu.make_async_copy(ys_ref.at[pl.ds(d_ref[0, 0, j], 1)], buf.at[s, pl.ds(j, 1)], sem.at[s]).start()
            return carry

        lax.fori_loop(0, tm // ISSUE_UNROLL, issue, 0)

    @pl.when(i == 0)
    def _():
        issue_tile(dcur_ref, 0)

    @pl.when(i + 1 < n_tiles)
    def _():
        issue_tile(dnxt_ref, 1 - slot)

    pltpu.make_async_copy(ys_ref.at[pl.ds(0, tm)], buf.at[slot], sem.at[slot]).wait()

    h2 = h1_ref[...] + buf[slot]
    a = (_rms_scale(h2) * gple_ref[...]).astype(BF16)
    gate = _sigmoid(jnp.dot(a, wpg_ref[...], preferred_element_type=F32))
    emb = jnp.dot(p_ref[0].astype(BF16), wple_ref[...], preferred_element_type=F32)
    h3 = h2 + emb * gate
    if final:
        h3 = _rms_scale(h3) * gfin_ref[...]
    o_ref[...] = h3


def _combine(h1, ys, dest, p_all, layer, g_ple, w_ple, w_pg, g_final, final):
    tt = h1.shape[0]
    tm = ROW_TILE
    n_tiles = tt // tm
    dest3 = dest.reshape(n_tiles, 1, tm)
    full = lambda shape: pl.BlockSpec(shape, lambda i: (0,) * len(shape))
    row_blk = lambda w: pl.BlockSpec((tm, w), lambda i: (i, 0))
    return pl.pallas_call(
        functools.partial(_combine_kernel, final=final, n_tiles=n_tiles),
        grid=(n_tiles,),
        in_specs=[pl.BlockSpec((1, 1, tm), lambda i: (i, 0, 0), memory_space=pltpu.SMEM),
                  pl.BlockSpec((1, 1, tm), lambda i: (jnp.minimum(i + 1, n_tiles - 1), 0, 0), memory_space=pltpu.SMEM),
                  row_blk(D_MODEL), pl.BlockSpec((1, tm, D_PLE), lambda i: (layer, i, 0)), full((1, D_MODEL)),
                  full((D_PLE, D_MODEL)), full((D_MODEL, D_MODEL)), full((1, D_MODEL)),
                  pl.BlockSpec(memory_space=pl.ANY)],
        out_specs=row_blk(D_MODEL),
        out_shape=jax.ShapeDtypeStruct((tt, D_MODEL), F32),
        scratch_shapes=[pltpu.VMEM((2, tm, D_MODEL), F32), pltpu.SemaphoreType.DMA((2,))],
        compiler_params=_params(1),
        name="combine",
    )(dest3, dest3, h1, p_all, g_ple.reshape(1, D_MODEL), w_ple.astype(BF16), w_pg.astype(BF16),
      g_final.reshape(1, D_MODEL), ys)


def _class_experts():
    lo, hi = [], []
    for g in range(N_GROUPS):
        for a in range(EXPERTS_PER_GROUP):
            for b in range(a + 1, EXPERTS_PER_GROUP):
                lo.append(g * EXPERTS_PER_GROUP + a)
                hi.append(g * EXPERTS_PER_GROUP + b)
    return np.asarray(lo, np.int32), np.asarray(hi, np.int32)


def _routing_tables(routed, counts_f, n_blocks):
    cls = routed[:, D_MODEL].astype(jnp.int32)
    rank = routed[:, D_MODEL + 1].astype(jnp.int32)
    counts = counts_f[0, :N_CLASSES].astype(jnp.int32)
    padded = ((counts + MOE_BLOCK - 1) // MOE_BLOCK) * MOE_BLOCK
    pad_end = jnp.cumsum(padded)
    pad_start = pad_end - padded
    ids = jnp.arange(N_CLASSES, dtype=jnp.int32)
    dest = jnp.sum(jnp.where(cls[:, None] == ids, pad_start, 0), axis=-1) + rank
    n_used = (pad_end[-1] // MOE_BLOCK).astype(jnp.int32).reshape(1)
    starts = jnp.arange(n_blocks, dtype=jnp.int32) * MOE_BLOCK
    block_cls = jnp.minimum(jnp.sum((pad_end[None, :] <= starts[:, None]).astype(jnp.int32), axis=1), N_CLASSES - 1)
    pick = lambda table: jnp.sum(jnp.where(block_cls[:, None] == ids, table, 0), axis=-1)
    cls_lo, cls_hi = _class_experts()
    filled = jnp.clip(pick(counts) - (starts - pick(pad_start)), 0, MOE_BLOCK)
    zflag = (filled < MOE_BLOCK).astype(jnp.int32)
    return dest, pick(jnp.asarray(cls_lo)), pick(jnp.asarray(cls_hi)), n_used, zflag


def kernel(x, p, w_in, b_gate, conv_w, conv_b, rpb, g_na, g_ml, w_out, g_mix, g_moe, w_route_group, b_route_group, w_route_expert, b_route_expert, w_exp_gate, w_exp_up, w_exp_down, g_ple, w_ple, w_ple_gate, g_final):
    batch, seq, _ = x.shape
    depth = w_in.shape[0]
    tt = batch * seq
    assert seq % ROW_TILE == 0 and seq % GRID_W == 0
    n_blocks = tt // MOE_BLOCK + N_CLASSES
    h = x.reshape(tt, D_MODEL)
    for i in range(depth):
        qna, kna, vna, qml, kt_ml, vml, oml, g_col, g_row = _in_proj(h, seq, g_mix[i], w_in[i], b_gate[i], conv_w[i], conv_b[i])
        yna = _na_attention(qna, kna, vna, _na_bias_table(rpb[i], seq // GRID_W), g_na[i], batch, seq)
        hml = _mlstm(qml, kt_ml, vml, g_row, g_col, batch, seq)
        h1, routed, counts = _mix_out(h, yna, hml, oml, g_ml[i], w_out[i], g_moe[i],
                                      w_route_group[i], b_route_group[i], w_route_expert[i], b_route_expert[i])
        dest, block_ea, block_eb, n_used, zflag = _routing_tables(routed, counts, n_blocks)
        xs = _dispatch(routed, dest, zflag, n_blocks)
        ys = _expert_ffn(xs, block_ea, block_eb, n_used, w_exp_gate, w_exp_up, w_exp_down, i)
        h = _combine(h1, ys, dest, p.reshape(depth, tt, D_PLE), i, g_ple[i], w_ple[i], w_ple_gate[i], g_final,
                     final=(i == depth - 1))
    return h.reshape(batch, seq, D_MODEL)
```

```python
import functools

import numpy as np
import jax
import jax.numpy as jnp
from jax import lax
from jax.experimental import pallas as pl
from jax.experimental.pallas import tpu as pltpu

F32 = jnp.float32
BF16 = jnp.bfloat16

D_MODEL = 1024
GRID_W = 64
D_NA = 512
NA_HEADS = 8
NA_HEAD_DIM = 64
NA_WIN_H = 8
NA_WIN_W = 16
D_ML = 512
ML_HEADS = 4
ML_HEAD_DIM = 128
ML_CONV = 5
ML_CHUNK = 128
N_GROUPS = 4
EXPERTS_PER_GROUP = 8
N_EXPERTS = 32
D_EXPERT = 512
D_PLE = 256
EPS = 1e-6
NEG_INF = -1e30
LOG2E = 1.4426950408889634

LANES = 128
SUBLANES = 8
ROW_TILE = 512
MOE_BLOCK = 256
PAIRS_PER_GROUP = EXPERTS_PER_GROUP * (EXPERTS_PER_GROUP - 1) // 2
N_CLASSES = N_GROUPS * PAIRS_PER_GROUP
PACKED = D_MODEL // 2
ROUTED_WIDTH = PACKED + LANES
U32 = jnp.uint32
INFO_CLASS = PACKED
INFO_RANK = PACKED + 1
INFO_W_LO = PACKED + 2
INFO_W_HI = PACKED + 3
HALO = SUBLANES
ISSUE_UNROLL = 8
EXPERT_LANE0 = 8
GATE_LANES_PER_DIR = 3 * ML_HEADS
VMEM_LIMIT = 56 * 1024 * 1024


def _params(n_axes, flags=None):
    return pltpu.CompilerParams(dimension_semantics=("arbitrary",) * n_axes, vmem_limit_bytes=VMEM_LIMIT, flags=flags)


def _sigmoid(x):
    return 1.0 / (1.0 + jnp.exp(-x))


def _log_sigmoid(x):
    return jnp.minimum(x, 0.0) - jnp.log1p(jnp.exp(-jnp.abs(x)))


def _rms_scale(x):
    return x * lax.rsqrt(jnp.mean(x * x, axis=-1, keepdims=True) + EPS)


def _scan_chunk(x, axis, reverse, op, identity):
    n = x.shape[axis]
    idx = lax.broadcasted_iota(jnp.int32, x.shape, axis)
    sh = 1
    while sh < n:
        if reverse:
            x = op(x, jnp.where(idx < n - sh, pltpu.roll(x, n - sh, axis), identity))
        else:
            x = op(x, jnp.where(idx >= sh, pltpu.roll(x, sh, axis), identity))
        sh *= 2
    return x


def _in_proj_kernel(hp_ref, h_ref, hn_ref, g_ref, wna_ref, wqk_ref, wvo_ref, wgr_ref, bgr_ref, cw_ref, cb_ref,
                    qna_ref, kna_ref, vna_ref, qml_ref, kt_ref, vml_ref, oml_ref, gc_ref, gr_ref,
                    zbuf, *, tiles_per_seq):
    i = pl.program_id(0)
    tm = h_ref.shape[0]
    g = g_ref[...]

    def norm(x):
        return (_rms_scale(x) * g).astype(BF16)

    a = norm(h_ref[...])

    pos = i % tiles_per_seq
    zp = jnp.dot(norm(hp_ref[...]), wqk_ref[...], preferred_element_type=F32)
    zn = jnp.dot(norm(hn_ref[...]), wqk_ref[...], preferred_element_type=F32)
    zbuf[0:HALO, :] = jnp.where(pos == 0, 0.0, zp)
    zbuf[HALO:HALO + tm, :] = jnp.dot(a, wqk_ref[...], preferred_element_type=F32)
    zbuf[HALO + tm:2 * HALO + tm, :] = jnp.where(pos == tiles_per_seq - 1, 0.0, zn)

    ng = 2 * GATE_LANES_PER_DIR
    nt = (((1,), (1,)), ((), ()))
    gi = lax.dot_general(wgr_ref[0:ng, :], a, nt, preferred_element_type=F32) + bgr_ref[0:ng, :]
    gf = lax.dot_general(wgr_ref[ng:2 * ng, :], a, nt, preferred_element_type=F32) + bgr_ref[ng:2 * ng, :]
    row = lax.broadcasted_iota(jnp.int32, (ng, ML_CHUNK), 0)
    is_fw = row < GATE_LANES_PER_DIR
    qsel = jnp.where(is_fw, row, row - GATE_LANES_PER_DIR)
    pad_rows = jnp.zeros((ML_CHUNK - ng, ML_CHUNK), F32)

    def gates(c):
        cols_c = slice(c * ML_CHUNK, (c + 1) * ML_CHUNK)
        lf = _log_sigmoid(gf[:, cols_c])
        b = jnp.where(is_fw, _scan_chunk(lf, 1, False, jnp.add, 0.0), _scan_chunk(lf, 1, True, jnp.add, 0.0))
        r = gi[:, cols_c] - b
        cmax = jnp.where(is_fw, _scan_chunk(r, 1, False, jnp.maximum, -jnp.inf),
                         _scan_chunk(r, 1, True, jnp.maximum, -jnp.inf))
        packed = jnp.where(qsel < ML_HEADS, r, jnp.where(qsel < 2 * ML_HEADS, cmax, b))
        gc_ref[cols_c, :] = jnp.concatenate([packed, pad_rows], axis=0).T
        gr_ref[:, cols_c] = jnp.concatenate([packed[0:ML_HEADS], packed[GATE_LANES_PER_DIR:GATE_LANES_PER_DIR + ML_HEADS]], axis=0)

    def conv(c0, cstep=256):
        acc = cb_ref[:, c0:c0 + cstep]
        for j in range(ML_CONV):
            r0 = HALO - ML_CONV // 2 + j
            acc = acc + zbuf[r0:r0 + tm, c0:c0 + cstep] * cw_ref[j:j + 1, c0:c0 + cstep]
        y = acc * _sigmoid(acc)
        if c0 < D_ML:
            qml_ref[:, c0:c0 + cstep] = (y * (ML_HEAD_DIM ** -0.5)).astype(BF16)
        else:
            kt_ref[c0 - D_ML:c0 - D_ML + cstep, :] = y.T.astype(BF16)

    def proj(w_ref, c0, scale=None):
        z = jnp.dot(a, w_ref[:, c0:c0 + D_NA], preferred_element_type=F32)
        return (z if scale is None else z * scale).astype(BF16)

    conv(0)
    qna_ref[...] = proj(wna_ref, 0, NA_HEAD_DIM ** -0.5 * LOG2E)
    conv(256)
    kna_ref[...] = proj(wna_ref, D_NA)
    conv(512)
    vna_ref[...] = proj(wna_ref, 2 * D_NA)
    conv(768)
    vml_ref[...] = proj(wvo_ref, 0)
    for c in range(tm // ML_CHUNK):
        gates(c)
    oml_ref[...] = proj(wvo_ref, D_ML)


def _in_proj(h, seq, g_mix, w_in, b_gate, conv_w, conv_b):
    t = h.shape[0]
    tm = ROW_TILE
    n_tiles = t // tm
    tiles_per_seq = seq // tm
    nh = 4 * ML_HEADS
    o_qk = 3 * D_NA
    o_v = o_qk + 2 * D_ML
    o_g = o_v + 2 * D_ML
    wna = w_in[:, :o_qk].astype(BF16)
    wqk = w_in[:, o_qk:o_v].astype(BF16)
    wvo = w_in[:, o_v:o_g].astype(BF16)
    wg = w_in[:, o_g:o_g + nh]
    ng = 2 * GATE_LANES_PER_DIR
    gate_id = np.arange(ng)
    src_i = (gate_id // GATE_LANES_PER_DIR) * 2 * ML_HEADS + gate_id % ML_HEADS
    wgr = jnp.concatenate([wg[:, src_i], wg[:, src_i + ML_HEADS]], axis=1).T.astype(BF16)
    bgr = jnp.concatenate([b_gate[src_i], b_gate[src_i + ML_HEADS]]).reshape(2 * ng, 1)
    cw = jnp.pad(conv_w, ((0, SUBLANES - ML_CONV), (0, 0)))
    cb = conv_b.reshape(1, 2 * D_ML)
    full = lambda shape: pl.BlockSpec(shape, lambda i: (0,) * len(shape))
    row_blk = lambda w: pl.BlockSpec((tm, w), lambda i: (i, 0))
    hb = tm // HALO
    out_bf = jax.ShapeDtypeStruct((t, D_NA), BF16)
    return pl.pallas_call(
        functools.partial(_in_proj_kernel, tiles_per_seq=tiles_per_seq),
        grid=(n_tiles,),
        in_specs=[
            pl.BlockSpec((HALO, D_MODEL), lambda i: (jnp.maximum(i * hb - 1, 0), 0)),
            row_blk(D_MODEL),
            pl.BlockSpec((HALO, D_MODEL), lambda i: (jnp.minimum((i + 1) * hb, t // HALO - 1), 0)),
            full((1, D_MODEL)), full((D_MODEL, o_qk)), full((D_MODEL, 2 * D_ML)), full((D_MODEL, 2 * D_ML)),
            full((2 * ng, D_MODEL)), full((2 * ng, 1)),
            full((SUBLANES, 2 * D_ML)), full((1, 2 * D_ML)),
        ],
        out_specs=[row_blk(D_NA)] * 4 + [pl.BlockSpec((D_ML, tm), lambda i: (0, i))] + [row_blk(D_NA)] * 2
        + [row_blk(LANES), pl.BlockSpec((2 * ML_HEADS, tm), lambda i: (0, i))],
        out_shape=[out_bf] * 4 + [jax.ShapeDtypeStruct((D_ML, t), BF16)] + [out_bf] * 2
        + [jax.ShapeDtypeStruct((t, LANES), F32), jax.ShapeDtypeStruct((2 * ML_HEADS, t), F32)],
        scratch_shapes=[pltpu.VMEM((tm + 2 * HALO, 2 * D_ML), F32)],
        compiler_params=_params(1),
        name="in_proj",
    )(h, h, h, g_mix.reshape(1, D_MODEL), wna, wqk, wvo, wgr, bgr, cw, cb)


def _na_bias_table(rpb, rows):
    kh = min(NA_WIN_H, rows)
    o = np.arange(kh)[:, None]
    kr = np.arange(kh)[None, :]
    dr = kr - o + (NA_WIN_H - 1)
    qc = np.arange(GRID_W)[:, None]
    kc = np.arange(GRID_W)[None, :]
    cs = np.clip(qc - NA_WIN_W // 2, 0, GRID_W - NA_WIN_W)
    inw = (kc >= cs) & (kc < cs + NA_WIN_W)
    dc = np.clip(kc - qc, 1 - NA_WIN_W, NA_WIN_W - 1) + (NA_WIN_W - 1)
    sel_r = jnp.asarray(dr[:, :, None] == np.arange(2 * NA_WIN_H - 1), F32)
    sel_c = jnp.asarray(dc[:, :, None] == np.arange(2 * NA_WIN_W - 1), F32)
    rows_sel = jnp.einsum("hrc,okr->hokc", rpb.astype(F32), sel_r, precision=lax.Precision.HIGHEST)
    b = jnp.einsum("hokc,qlc->hoqkl", rows_sel, sel_c, precision=lax.Precision.HIGHEST)
    b = jnp.where(inw[None, None, :, None, :], b * LOG2E, NEG_INF)
    return b.reshape(rpb.shape[0], kh, GRID_W, kh * GRID_W)


NA_HEADS_PER_BLOCK = 4
NA_BAND_ROWS = 16
NA_UNROLL = 8


def _na_kernel(q_ref, k_ref, v_ref, bias_ref, g_ref, o_ref, *, rows, kh, band):
    nkeys = kh * GRID_W
    hpb = NA_HEADS_PER_BLOCK
    m_rows = hpb * GRID_W
    width = hpb * NA_HEAD_DIM
    band_i = pl.program_id(2)
    lane_head = lax.shift_right_logical(lax.broadcasted_iota(jnp.int32, (m_rows, width), 1), 6)
    row_head = lax.shift_right_logical(lax.broadcasted_iota(jnp.int32, (m_rows, width), 0), 6)
    own = lane_head == row_head
    g = g_ref[...]

    def one_row(rl):
        r = band_i * band + rl
        rs = jnp.clip(r - kh // 2, 0, rows - kh)
        off = r - rs
        q = q_ref[0, pl.ds(pl.multiple_of(rl * GRID_W, GRID_W), GRID_W), :]
        kb = k_ref[0, pl.ds(pl.multiple_of(rs * GRID_W, GRID_W), nkeys), :]
        vb = v_ref[0, pl.ds(pl.multiple_of(rs * GRID_W, GRID_W), nkeys), :]
        qs = jnp.where(own, jnp.concatenate([q] * hpb, axis=0), jnp.zeros((m_rows, width), BF16))
        s = lax.dot_general(qs, kb, (((1,), (1,)), ((), ())), preferred_element_type=F32)
        s = s + bias_ref[:, off].reshape(m_rows, nkeys)
        m = jnp.max(s, axis=-1, keepdims=True)
        p = jnp.exp2(s - m)
        l = jnp.sum(p, axis=-1, keepdims=True)
        pv = jnp.dot(p.astype(BF16), vb, preferred_element_type=F32)
        oh = jnp.where(own, pv * (1.0 / l), 0.0)
        ms = jnp.sum(oh * oh, axis=-1, keepdims=True) * (1.0 / NA_HEAD_DIM)
        yn = oh * lax.rsqrt(ms + EPS)
        y = yn[0:GRID_W]
        for hh in range(1, hpb):
            y = y + yn[hh * GRID_W:(hh + 1) * GRID_W]
        o_ref[0, pl.ds(pl.multiple_of(rl * GRID_W, GRID_W), GRID_W), :] = (y * g).astype(BF16)

    def body(it, carry):
        for u in range(NA_UNROLL):
            one_row(it * NA_UNROLL + u)
        return carry

    lax.fori_loop(0, band // NA_UNROLL, body, 0)


def _na_attention(q, k, v, bias, g_na, batch, seq):
    rows = seq // GRID_W
    kh = min(NA_WIN_H, rows)
    hpb = NA_HEADS_PER_BLOCK
    width = hpb * NA_HEAD_DIM
    band = min(NA_BAND_ROWS, rows)
    assert rows % band == 0 and band % NA_UNROLL == 0
    q3, k3, v3 = (x.reshape(batch, seq, D_NA) for x in (q, k, v))
    seq_blk = pl.BlockSpec((1, seq, width), lambda b, j, r: (b, 0, j))
    band_blk = pl.BlockSpec((1, band * GRID_W, width), lambda b, j, r: (b, r, j))
    out = pl.pallas_call(
        functools.partial(_na_kernel, rows=rows, kh=kh, band=band),
        grid=(batch, D_NA // width, rows // band),
        in_specs=[band_blk, seq_blk, seq_blk,
                  pl.BlockSpec((hpb, kh, GRID_W, kh * GRID_W), lambda b, j, r: (j, 0, 0, 0)),
                  pl.BlockSpec((1, width), lambda b, j, r: (0, j))],
        out_specs=band_blk,
        out_shape=jax.ShapeDtypeStruct((batch, seq, D_NA), BF16),
        compiler_params=_params(3),
        name="na_attn",
    )(q3, k3, v3, bias, g_na.reshape(1, D_NA))
    return out.reshape(batch * seq, D_NA)


ML_UNROLL = 4
ML_HEADS_PER_STEP = 1


def _mlstm_kernel(q_ref, kt_ref, v_ref, gr_ref, gc_ref, o_ref, st_ref, *, n_chunks):
    L = ML_CHUNK
    hps = ML_HEADS_PER_STEP
    lane = lax.broadcasted_iota(jnp.int32, (L, LANES), 1)
    grow_i = lax.broadcasted_iota(jnp.int32, (2 * ML_HEADS, L), 0)
    row_i = lax.broadcasted_iota(jnp.int32, (L, L), 0)
    col_i = lax.broadcasted_iota(jnp.int32, (L, L), 1)
    ones = jnp.ones((L, ML_HEAD_DIM), BF16)

    def step(c, m_prev, hb, reverse):
        off = pl.multiple_of(c * L, L)
        d = 1 if reverse else 0
        hd = pl.program_id(1) * hps + hb
        cols = slice(hb * ML_HEAD_DIM, (hb + 1) * ML_HEAD_DIM)
        qc = q_ref[0, pl.ds(off, L), cols]
        kt = kt_ref[cols, pl.ds(off, L)]
        vc = v_ref[0, pl.ds(off, L), cols]
        gcol = gc_ref[pl.ds(off, L), :]
        grow = gr_ref[:, pl.ds(off, L)]

        def col(qn):
            sel = lane == d * GATE_LANES_PER_DIR + qn * ML_HEADS + hd
            return jnp.sum(jnp.where(sel, gcol, 0.0), axis=1, keepdims=True)

        r_col, cmax_col, b_col = col(0), col(1), col(2)
        r_row = jnp.sum(jnp.where(grow_i == d * ML_HEADS + hd, grow, 0.0), axis=0, keepdims=True)
        end = 0 if reverse else L - 1
        mm_end = jnp.maximum(m_prev, cmax_col[end:end + 1, :])
        m_new = b_col[end:end + 1, :] + mm_end
        mm_col = jnp.maximum(m_prev, cmax_col)
        causal = (col_i >= row_i) if reverse else (col_i <= row_i)
        dmat = jnp.exp(jnp.where(causal, r_row - mm_col, -jnp.inf))
        s = jnp.dot(qc, kt, preferred_element_type=F32) * dmat
        w_inter = jnp.exp(m_prev - mm_col)
        floor = jnp.exp(-(b_col + mm_col))
        w_col = jnp.exp(r_col - mm_end)
        decay = jnp.exp(m_prev - mm_end)
        vext = jnp.concatenate([vc, ones], axis=1)
        state = st_ref[d, hb]
        nd = (jnp.dot(s.astype(BF16), vext, preferred_element_type=F32)
              + w_inter * jnp.dot(qc, state.astype(BF16), preferred_element_type=F32))
        h = nd[:, :ML_HEAD_DIM] / jnp.maximum(jnp.abs(nd[:, ML_HEAD_DIM:]), floor)
        wv = jnp.concatenate([(w_col * vc.astype(F32)).astype(BF16),
                              jnp.broadcast_to(w_col, (L, ML_HEAD_DIM)).astype(BF16)], axis=1)
        st_ref[d, hb] = decay * state + jnp.dot(kt, wv, preferred_element_type=F32)
        return h, m_new

    st_ref[...] = jnp.zeros_like(st_ref)

    def sweep(accumulate):
        def body(it, carry):
            m = list(carry)
            for u in range(ML_UNROLL):
                j = it * ML_UNROLL + u
                jb = n_chunks - 1 - j
                rows_f = pl.ds(pl.multiple_of(j * L, L), L)
                rows_b = pl.ds(pl.multiple_of(jb * L, L), L)
                for hb in range(hps):
                    cols = slice(hb * ML_HEAD_DIM, (hb + 1) * ML_HEAD_DIM)
                    h_f, m[hb] = step(j, m[hb], hb, False)
                    h_b, m[hps + hb] = step(jb, m[hps + hb], hb, True)
                    if accumulate:
                        o_ref[0, rows_f, cols] += h_f
                        o_ref[0, rows_b, cols] += h_b
                    else:
                        o_ref[0, rows_f, cols] = h_f
                        o_ref[0, rows_b, cols] = h_b
            return tuple(m)
        return body

    half = n_chunks // (2 * ML_UNROLL)
    zero = jnp.zeros((1, 1), F32)
    carry = lax.fori_loop(0, half, sweep(False), (zero,) * (2 * hps))
    lax.fori_loop(half, 2 * half, sweep(True), carry)


def _mlstm(q, kt, v, g_row, g_col, batch, seq):
    n_chunks = seq // ML_CHUNK
    assert n_chunks % (2 * ML_UNROLL) == 0
    hps = ML_HEADS_PER_STEP
    width = hps * ML_HEAD_DIM
    q3, v3 = (x.reshape(batch, seq, D_ML) for x in (q, v))
    seq_blk = pl.BlockSpec((1, seq, width), lambda b, j: (b, 0, j))
    once_per_batch = dict(pipeline_mode=pl.Buffered(1))
    out = pl.pallas_call(
        functools.partial(_mlstm_kernel, n_chunks=n_chunks),
        grid=(batch, ML_HEADS // hps),
        in_specs=[seq_blk, pl.BlockSpec((width, seq), lambda b, j: (j, b)), seq_blk,
                  pl.BlockSpec((2 * ML_HEADS, seq), lambda b, j: (0, b), **once_per_batch),
                  pl.BlockSpec((seq, LANES), lambda b, j: (b, 0), **once_per_batch)],
        out_specs=seq_blk,
        out_shape=jax.ShapeDtypeStruct((batch, seq, D_ML), F32),
        scratch_shapes=[pltpu.VMEM((2, hps, ML_HEAD_DIM, 2 * ML_HEAD_DIM), F32)],
        compiler_params=_params(2),
        name="mlstm",
    )(q3, kt, v3, g_row, g_col)
    return out.reshape(batch * seq, D_ML)


def _mix_out_kernel(h_ref, yna_ref, hml_ref, oml_ref, gml_ref, wo_ref, gmoe_ref, wrh_ref, wrl_ref, br_ref, tri_ref,
                    h1_ref, t_ref, cnt_ref):
    i = pl.program_id(0)
    tm = h_ref.shape[0]

    @pl.when(i == 0)
    def _():
        cnt_ref[...] = jnp.zeros_like(cnt_ref)

    hml = hml_ref[...]
    parts = [_rms_scale(hml[:, d * ML_HEAD_DIM:(d + 1) * ML_HEAD_DIM]) for d in range(ML_HEADS)]
    yml = _sigmoid(oml_ref[...].astype(F32)) * (jnp.concatenate(parts, axis=1) * gml_ref[...])
    mix = (jnp.dot(yna_ref[...], wo_ref[0:D_NA, :], preferred_element_type=F32)
           + jnp.dot(yml.astype(BF16), wo_ref[D_NA:D_NA + D_ML, :], preferred_element_type=F32))
    h1 = h_ref[...] + mix
    h1_ref[...] = h1
    t = _rms_scale(h1) * gmoe_ref[...]
    bits = lax.bitcast_convert_type(t.astype(BF16).astype(F32), U32)
    t_ref[:, 0:PACKED] = bits[:, PACKED:] | lax.shift_right_logical(bits[:, :PACKED], U32(16))

    t_hi = t.astype(BF16)
    t_lo = (t - t_hi.astype(F32)).astype(BF16)
    logits = (jnp.dot(t_hi, wrh_ref[...], preferred_element_type=F32)
              + jnp.dot(t_lo, wrh_ref[...], preferred_element_type=F32)
              + jnp.dot(t_hi, wrl_ref[...], preferred_element_type=F32)) + br_ref[...]
    lane = lax.broadcasted_iota(jnp.int32, (tm, LANES), 1)
    big = jnp.int32(LANES)

    def top(x):
        mx = jnp.max(x, axis=1, keepdims=True)
        idx = jnp.min(jnp.where(x == mx, lane, big), axis=1, keepdims=True)
        return mx, idx

    lg = jnp.where(lane < N_GROUPS, logits, -jnp.inf)
    g_max, g_idx = top(lg)
    pg_top = 1.0 / jnp.sum(jnp.exp(lg - g_max), axis=1, keepdims=True)
    lo = EXPERT_LANE0 + g_idx * EXPERTS_PER_GROUP
    le = jnp.where((lane >= lo) & (lane < lo + EXPERTS_PER_GROUP), logits, -jnp.inf)
    e1_max, i1 = top(le)
    e_sum = jnp.sum(jnp.exp(le - e1_max), axis=1, keepdims=True)
    e2_max, i2 = top(jnp.where(lane == i1, -jnp.inf, le))
    p1 = 1.0 / e_sum
    p2 = jnp.exp(e2_max - e1_max) / e_sum
    w1 = pg_top * p1 / (p1 + p2)
    w2 = pg_top * p2 / (p1 + p2)
    e1 = i1 - EXPERT_LANE0
    e2 = i2 - EXPERT_LANE0

    la = jnp.minimum(e1, e2) - g_idx * EXPERTS_PER_GROUP
    lb = jnp.maximum(e1, e2) - g_idx * EXPERTS_PER_GROUP
    cls = g_idx * PAIRS_PER_GROUP + lax.shift_right_logical(la * (2 * EXPERTS_PER_GROUP - 1 - la), 1) + (lb - la - 1)
    w_lo = jnp.where(e1 < e2, w1, w2)
    w_hi = jnp.where(e1 < e2, w2, w1)
    oh = (lane == cls).astype(F32)
    before = jnp.dot(tri_ref[...], oh.astype(BF16), preferred_element_type=F32) + cnt_ref[...]
    rank = jnp.sum(oh * before, axis=1, keepdims=True)
    cnt_ref[...] = cnt_ref[...] + jnp.sum(oh, axis=0, keepdims=True)
    info = jnp.zeros((tm, LANES), F32)
    for j, cval in enumerate([cls.astype(F32), rank, w_lo, w_hi]):
        info = jnp.where(lane == j, cval, info)
    t_ref[:, PACKED:PACKED + LANES] = lax.bitcast_convert_type(info, U32)


def _mix_out(h, yna, hml, oml, g_ml, w_out, g_moe, w_rg, b_rg, w_re, b_re):
    t = h.shape[0]
    tm = ROW_TILE
    wr = jnp.zeros((D_MODEL, LANES), F32)
    wr = wr.at[:, :N_GROUPS].set(w_rg).at[:, EXPERT_LANE0:EXPERT_LANE0 + N_EXPERTS].set(w_re)
    wrh = wr.astype(BF16)
    wrl = (wr - wrh.astype(F32)).astype(BF16)
    br = jnp.zeros((1, LANES), F32)
    br = br.at[0, :N_GROUPS].set(b_rg).at[0, EXPERT_LANE0:EXPERT_LANE0 + N_EXPERTS].set(b_re)
    tri = jnp.asarray(np.tril(np.ones((tm, tm), np.float32), -1), BF16)
    full = lambda shape: pl.BlockSpec(shape, lambda i: (0,) * len(shape))
    row_blk = lambda w: pl.BlockSpec((tm, w), lambda i: (i, 0))
    return pl.pallas_call(
        _mix_out_kernel,
        grid=(t // tm,),
        in_specs=[row_blk(D_MODEL), row_blk(D_NA), row_blk(D_ML), row_blk(D_ML), full((1, D_ML)),
                  full((D_MODEL, D_MODEL)), full((1, D_MODEL)), full((D_MODEL, LANES)), full((D_MODEL, LANES)),
                  full((1, LANES)), full((tm, tm))],
        out_specs=[row_blk(D_MODEL), row_blk(ROUTED_WIDTH), full((1, LANES))],
        out_shape=[jax.ShapeDtypeStruct((t, D_MODEL), F32), jax.ShapeDtypeStruct((t, ROUTED_WIDTH), U32),
                   jax.ShapeDtypeStruct((1, LANES), F32)],
        compiler_params=_params(1),
        name="mix_out",
    )(h, yna, hml, oml, g_ml.reshape(1, D_ML), w_out.astype(BF16), g_moe.reshape(1, D_MODEL), wrh, wrl, br, tri)


def _dispatch_kernel(zflag_ref, dest_ref, t_ref, xs_ref, zbuf, ring, sem, zsem, *, n_blocks, n_tiles):
    tm = t_ref.shape[0]

    @pl.when(pl.program_id(0) == 0)
    def _():
        zbuf[...] = jnp.zeros_like(zbuf)

        def zero_copy(b):
            return pltpu.make_async_copy(zbuf, xs_ref.at[pl.ds(pl.multiple_of(b * MOE_BLOCK, MOE_BLOCK), MOE_BLOCK)], zsem)

        def zissue(b, carry):
            @pl.when(zflag_ref[b] != 0)
            def _():
                zero_copy(b).start()
            return carry

        def zwait(b, carry):
            @pl.when(zflag_ref[b] != 0)
            def _():
                zero_copy(b).wait()
            return carry

        lax.fori_loop(0, n_blocks, zissue, 0)
        lax.fori_loop(0, n_blocks, zwait, 0)

    i = pl.program_id(0)
    slot = i % 2

    def wait_slot(s):
        pltpu.make_async_copy(ring.at[s], xs_ref.at[pl.ds(0, tm)], sem.at[s]).wait()

    @pl.when(i >= 2)
    def _():
        wait_slot(slot)

    ring[slot] = t_ref[...]

    def issue(it, carry):
        for u in range(ISSUE_UNROLL):
            j = it * ISSUE_UNROLL + u
            pltpu.make_async_copy(ring.at[slot, pl.ds(j, 1)], xs_ref.at[pl.ds(dest_ref[0, 0, j], 1)], sem.at[slot]).start()
        return carry

    lax.fori_loop(0, tm // ISSUE_UNROLL, issue, 0)

    @pl.when(i == n_tiles - 1)
    def _():
        wait_slot(slot)

        @pl.when(i >= 1)
        def _():
            wait_slot(1 - slot)


def _dispatch(t, dest, zflag, n_blocks):
    tt = t.shape[0]
    tm = ROW_TILE
    n_tiles = tt // tm
    return pl.pallas_call(
        functools.partial(_dispatch_kernel, n_blocks=n_blocks, n_tiles=n_tiles),
        grid_spec=pltpu.PrefetchScalarGridSpec(
            num_scalar_prefetch=1,
            grid=(n_tiles,),
            in_specs=[pl.BlockSpec((1, 1, tm), lambda i, zf: (i, 0, 0), memory_space=pltpu.SMEM),
                      pl.BlockSpec((tm, ROUTED_WIDTH), lambda i, zf: (i, 0))],
            out_specs=pl.BlockSpec(memory_space=pl.ANY),
            scratch_shapes=[pltpu.VMEM((MOE_BLOCK, ROUTED_WIDTH), U32), pltpu.VMEM((2, tm, ROUTED_WIDTH), U32),
                            pltpu.SemaphoreType.DMA((2,)), pltpu.SemaphoreType.DMA],
        ),
        out_shape=jax.ShapeDtypeStruct((n_blocks * MOE_BLOCK, ROUTED_WIDTH), U32),
        compiler_params=_params(1),
        name="dispatch",
    )(zflag, dest.reshape(n_tiles, 1, tm), t)


def _ffn_kernel(ea_ref, eb_ref, nu_ref, x_ref, wg_hbm, wu_hbm, wd_hbm, y_ref,
                wg_s, wu_s, wd_s, stage_g, stage_u, stage_d, sem, *, layer):
    i = pl.program_id(0)
    used = i < nu_ref[0]
    group = lax.shift_right_logical(ea_ref[i], 3)
    prev_group = lax.shift_right_logical(ea_ref[jnp.maximum(i - 1, 0)], 3)
    mats = ((wg_hbm, stage_g, wg_s), (wu_hbm, stage_u, wu_s), (wd_hbm, stage_d, wd_s))

    @pl.when(used & ((i == 0) | (group != prev_group)))
    def _():
        def fetch(m, e):
            w_hbm, stage, _ = mats[m]
            return pltpu.make_async_copy(w_hbm.at[layer, group * EXPERTS_PER_GROUP + e], stage, sem.at[m])

        for m in range(3):
            fetch(m, 0).start()

        def load_expert(e, carry):
            for m in range(3):
                _, stage, w_s = mats[m]
                fetch(m, e).wait()
                w_s[e] = stage[...].astype(BF16)

                @pl.when(e + 1 < EXPERTS_PER_GROUP)
                def _():
                    fetch(m, e + 1).start()
            return carry

        lax.fori_loop(0, EXPERTS_PER_GROUP, load_expert, 0)

    def swiglu(x, e):
        g = jnp.dot(x, wg_s[e], preferred_element_type=F32)
        u = jnp.dot(x, wu_s[e], preferred_element_type=F32)
        hid = (g * _sigmoid(g) * u).astype(BF16)
        return jnp.dot(hid, wd_s[e], preferred_element_type=F32)

    @pl.when(used)
    def _():
        words = x_ref[:, 0:PACKED]
        as_f32 = lambda u: lax.bitcast_convert_type(u, F32)
        x = jnp.concatenate([as_f32(lax.shift_left(words, U32(16))).astype(BF16),
                             as_f32(words & U32(0xFFFF0000)).astype(BF16)], axis=1)
        y_ref[...] = (as_f32(x_ref[:, INFO_W_LO:INFO_W_LO + 1]) * swiglu(x, ea_ref[i] & (EXPERTS_PER_GROUP - 1))
                      + as_f32(x_ref[:, INFO_W_HI:INFO_W_HI + 1]) * swiglu(x, eb_ref[i] & (EXPERTS_PER_GROUP - 1)))

    @pl.when(jnp.logical_not(used))
    def _():
        y_ref[...] = jnp.zeros_like(y_ref)


def _expert_ffn(xs, block_ea, block_eb, n_used, w_gate, w_up, w_down, layer):
    n_rows = xs.shape[0]
    nb = n_rows // MOE_BLOCK
    blk = lambda i, ea, eb, nu: (jnp.minimum(i, nu[0] - 1), 0)
    hbm = pl.BlockSpec(memory_space=pl.ANY)
    return pl.pallas_call(
        functools.partial(_ffn_kernel, layer=layer),
        grid_spec=pltpu.PrefetchScalarGridSpec(
            num_scalar_prefetch=3,
            grid=(nb,),
            in_specs=[pl.BlockSpec((MOE_BLOCK, ROUTED_WIDTH), blk), hbm, hbm, hbm],
            out_specs=pl.BlockSpec((MOE_BLOCK, D_MODEL), lambda i, ea, eb, nu: (i, 0)),
            scratch_shapes=[pltpu.VMEM((EXPERTS_PER_GROUP, D_MODEL, D_EXPERT), BF16),
                            pltpu.VMEM((EXPERTS_PER_GROUP, D_MODEL, D_EXPERT), BF16),
                            pltpu.VMEM((EXPERTS_PER_GROUP, D_EXPERT, D_MODEL), BF16),
                            pltpu.VMEM((D_MODEL, D_EXPERT), F32), pltpu.VMEM((D_MODEL, D_EXPERT), F32),
                            pltpu.VMEM((D_EXPERT, D_MODEL), F32), pltpu.SemaphoreType.DMA((3,))],
        ),
        out_shape=jax.ShapeDtypeStruct((n_rows, D_MODEL), F32),
        compiler_params=_params(1),
        name="expert_ffn",
    )(block_ea, block_eb, n_used, xs, w_gate, w_up, w_down)


def _combine_kernel(dcur_ref, dnxt_ref, h1_ref, p_ref, gple_ref, wple_ref, wpg_ref, gfin_ref, ys_ref,
                    o_ref, buf, sem, *, final, n_tiles):
    i = pl.program_id(0)
    tm = h1_ref.shape[0]
    slot = i % 2

    def issue_tile(d_ref, s):
        def issue(it, carry):
            for u in range(ISSUE_UNROLL):
                j = it * ISSUE_UNROLL + u
                pltpu.make_async_copy(ys_ref.at[pl.ds(d_ref[0, 0, j], 1)], buf.at[s, pl.ds(j, 1)], sem.at[s]).start()
            return carry

        lax.fori_loop(0, tm // ISSUE_UNROLL, issue, 0)

    @pl.when(i == 0)
    def _():
        issue_tile(dcur_ref, 0)

    @pl.when(i + 1 < n_tiles)
    def _():
        issue_tile(dnxt_ref, 1 - slot)

    pltpu.make_async_copy(ys_ref.at[pl.ds(0, tm)], buf.at[slot], sem.at[slot]).wait()

    h2 = h1_ref[...] + buf[slot]
    a = (_rms_scale(h2) * gple_ref[...]).astype(BF16)
    gate = _sigmoid(jnp.dot(a, wpg_ref[...], preferred_element_type=F32))
    emb = jnp.dot(p_ref[0].astype(BF16), wple_ref[...], preferred_element_type=F32)
    h3 = h2 + emb * gate
    if final:
        h3 = _rms_scale(h3) * gfin_ref[...]
    o_ref[...] = h3


def _combine(h1, ys, dest, p_all, layer, g_ple, w_ple, w_pg, g_final, final):
    tt = h1.shape[0]
    tm = ROW_TILE
    n_tiles = tt // tm
    dest3 = dest.reshape(n_tiles, 1, tm)
    full = lambda shape: pl.BlockSpec(shape, lambda i: (0,) * len(shape))
    row_blk = lambda w: pl.BlockSpec((tm, w), lambda i: (i, 0))
    return pl.pallas_call(
        functools.partial(_combine_kernel, final=final, n_tiles=n_tiles),
        grid=(n_tiles,),
        in_specs=[pl.BlockSpec((1, 1, tm), lambda i: (i, 0, 0), memory_space=pltpu.SMEM),
                  pl.BlockSpec((1, 1, tm), lambda i: (jnp.minimum(i + 1, n_tiles - 1), 0, 0), memory_space=pltpu.SMEM),
                  row_blk(D_MODEL), pl.BlockSpec((1, tm, D_PLE), lambda i: (layer, i, 0)), full((1, D_MODEL)),
                  full((D_PLE, D_MODEL)), full((D_MODEL, D_MODEL)), full((1, D_MODEL)),
                  pl.BlockSpec(memory_space=pl.ANY)],
        out_specs=row_blk(D_MODEL),
        out_shape=jax.ShapeDtypeStruct((tt, D_MODEL), F32),
        scratch_shapes=[pltpu.VMEM((2, tm, D_MODEL), F32), pltpu.SemaphoreType.DMA((2,))],
        compiler_params=_params(1),
        name="combine",
    )(dest3, dest3, h1, p_all, g_ple.reshape(1, D_MODEL), w_ple.astype(BF16), w_pg.astype(BF16),
      g_final.reshape(1, D_MODEL), ys)


def _class_experts():
    lo, hi = [], []
    for g in range(N_GROUPS):
        for a in range(EXPERTS_PER_GROUP):
            for b in range(a + 1, EXPERTS_PER_GROUP):
                lo.append(g * EXPERTS_PER_GROUP + a)
                hi.append(g * EXPERTS_PER_GROUP + b)
    return np.asarray(lo, np.int32), np.asarray(hi, np.int32)


def _routing_tables(routed, counts_f, n_blocks):
    info = lax.bitcast_convert_type(routed[:, INFO_CLASS:INFO_RANK + 1], F32)
    cls = info[:, 0].astype(jnp.int32)
    rank = info[:, 1].astype(jnp.int32)
    counts = counts_f[0, :N_CLASSES].astype(jnp.int32)
    padded = ((counts + MOE_BLOCK - 1) // MOE_BLOCK) * MOE_BLOCK
    pad_end = jnp.cumsum(padded)
    pad_start = pad_end - padded
    ids = jnp.arange(N_CLASSES, dtype=jnp.int32)
    dest = jnp.sum(jnp.where(cls[:, None] == ids, pad_start, 0), axis=-1) + rank
    n_used = (pad_end[-1] // MOE_BLOCK).astype(jnp.int32).reshape(1)
    starts = jnp.arange(n_blocks, dtype=jnp.int32) * MOE_BLOCK
    block_cls = jnp.minimum(jnp.sum((pad_end[None, :] <= starts[:, None]).astype(jnp.int32), axis=1), N_CLASSES - 1)
    pick = lambda table: jnp.sum(jnp.where(block_cls[:, None] == ids, table, 0), axis=-1)
    cls_lo, cls_hi = _class_experts()
    filled = jnp.clip(pick(counts) - (starts - pick(pad_start)), 0, MOE_BLOCK)
    zflag = (filled < MOE_BLOCK).astype(jnp.int32)
    return dest, pick(jnp.asarray(cls_lo)), pick(jnp.asarray(cls_hi)), n_used, zflag


def kernel(x, p, w_in, b_gate, conv_w, conv_b, rpb, g_na, g_ml, w_out, g_mix, g_moe, w_route_group, b_route_group, w_route_expert, b_route_expert, w_exp_gate, w_exp_up, w_exp_down, g_ple, w_ple, w_ple_gate, g_final):
    batch, seq, _ = x.shape
    depth = w_in.shape[0]
    tt = batch * seq
    assert seq % ROW_TILE == 0 and seq % GRID_W == 0
    n_blocks = tt // MOE_BLOCK + N_CLASSES
    h = x.reshape(tt, D_MODEL)
    for i in range(depth):
        qna, kna, vna, qml, kt_ml, vml, oml, g_col, g_row = _in_proj(h, seq, g_mix[i], w_in[i], b_gate[i], conv_w[i], conv_b[i])
        yna = _na_attention(qna, kna, vna, _na_bias_table(rpb[i], seq // GRID_W), g_na[i], batch, seq)
        hml = _mlstm(qml, kt_ml, vml, g_row, g_col, batch, seq)
        h1, routed, counts = _mix_out(h, yna, hml, oml, g_ml[i], w_out[i], g_moe[i],
                                      w_route_group[i], b_route_group[i], w_route_expert[i], b_route_expert[i])
        dest, block_ea, block_eb, n_used, zflag = _routing_tables(routed, counts, n_blocks)
        xs = _dispatch(routed, dest, zflag, n_blocks)
        ys = _expert_ffn(xs, block_ea, block_eb, n_used, w_exp_gate, w_exp_up, w_exp_down, i)
        h = _combine(h1, ys, dest, p.reshape(depth, tt, D_PLE), i, g_ple[i], w_ple[i], w_ple_gate[i], g_final,
                     final=(i == depth - 1))
    return h.reshape(batch, seq, D_MODEL)
```

```python
import functools

import numpy as np
import jax
import jax.numpy as jnp
from jax import lax
from jax.experimental import pallas as pl
from jax.experimental.pallas import tpu as pltpu

F32 = jnp.float32
BF16 = jnp.bfloat16

D_MODEL = 1024
GRID_W = 64
D_NA = 512
NA_HEADS = 8
NA_HEAD_DIM = 64
NA_WIN_H = 8
NA_WIN_W = 16
D_ML = 512
ML_HEADS = 4
ML_HEAD_DIM = 128
ML_CONV = 5
ML_CHUNK = 128
N_GROUPS = 4
EXPERTS_PER_GROUP = 8
N_EXPERTS = 32
D_EXPERT = 512
D_PLE = 256
EPS = 1e-6
NEG_INF = -1e30
LOG2E = 1.4426950408889634

LANES = 128
SUBLANES = 8
ROW_TILE = 512
IN_PROJ_TILE = 512
MOE_BLOCK = 256
PAIRS_PER_GROUP = EXPERTS_PER_GROUP * (EXPERTS_PER_GROUP - 1) // 2
N_CLASSES = N_GROUPS * PAIRS_PER_GROUP
PACKED = D_MODEL // 2
ROUTED_WIDTH = PACKED + LANES
U32 = jnp.uint32
INFO_CLASS = PACKED
INFO_RANK = PACKED + 1
INFO_W_LO = PACKED + 2
INFO_W_HI = PACKED + 3
HALO = SUBLANES
ISSUE_UNROLL = 8
EXPERT_LANE0 = 8
GATE_LANES_PER_DIR = 3 * ML_HEADS
VMEM_LIMIT = 56 * 1024 * 1024


def _params(n_axes, flags=None):
    return pltpu.CompilerParams(dimension_semantics=("arbitrary",) * n_axes, vmem_limit_bytes=VMEM_LIMIT, flags=flags)


def _sigmoid(x):
    return 1.0 / (1.0 + jnp.exp(-x))


def _log_sigmoid(x):
    return jnp.minimum(x, 0.0) - jnp.log1p(jnp.exp(-jnp.abs(x)))


def _rms_scale(x):
    return x * lax.rsqrt(jnp.mean(x * x, axis=-1, keepdims=True) + EPS)


def _scan_chunk(x, axis, reverse, op, identity):
    n = x.shape[axis]
    idx = lax.broadcasted_iota(jnp.int32, x.shape, axis)
    sh = 1
    while sh < n:
        if reverse:
            x = op(x, jnp.where(idx < n - sh, pltpu.roll(x, n - sh, axis), identity))
        else:
            x = op(x, jnp.where(idx >= sh, pltpu.roll(x, sh, axis), identity))
        sh *= 2
    return x


def _in_proj_kernel(hp_ref, h_ref, hn_ref, g_ref, wna_ref, wqk_ref, wvo_ref, wgr_ref, bgr_ref, cw_ref, cb_ref,
                    qna_ref, kna_ref, vna_ref, qml_ref, kt_ref, vml_ref, oml_ref, gc_ref, gr_ref,
                    zbuf, *, tiles_per_seq):
    i = pl.program_id(0)
    tm = h_ref.shape[0]
    g = g_ref[...]

    def norm(x):
        return (_rms_scale(x) * g).astype(BF16)

    a = norm(h_ref[...])

    pos = i % tiles_per_seq
    zp = jnp.dot(norm(hp_ref[...]), wqk_ref[...], preferred_element_type=F32)
    zn = jnp.dot(norm(hn_ref[...]), wqk_ref[...], preferred_element_type=F32)
    zbuf[0:HALO, :] = jnp.where(pos == 0, 0.0, zp)
    zbuf[HALO:HALO + tm, :] = jnp.dot(a, wqk_ref[...], preferred_element_type=F32)
    zbuf[HALO + tm:2 * HALO + tm, :] = jnp.where(pos == tiles_per_seq - 1, 0.0, zn)

    ng = 2 * GATE_LANES_PER_DIR
    nt = (((1,), (1,)), ((), ()))
    gi = lax.dot_general(wgr_ref[0:ng, :], a, nt, preferred_element_type=F32) + bgr_ref[0:ng, :]
    gf = lax.dot_general(wgr_ref[ng:2 * ng, :], a, nt, preferred_element_type=F32) + bgr_ref[ng:2 * ng, :]
    row = lax.broadcasted_iota(jnp.int32, (ng, ML_CHUNK), 0)
    is_fw = row < GATE_LANES_PER_DIR
    qsel = jnp.where(is_fw, row, row - GATE_LANES_PER_DIR)
    pad_rows = jnp.zeros((ML_CHUNK - ng, ML_CHUNK), F32)

    def gates(c):
        cols_c = slice(c * ML_CHUNK, (c + 1) * ML_CHUNK)
        lf = _log_sigmoid(gf[:, cols_c])
        b = jnp.where(is_fw, _scan_chunk(lf, 1, False, jnp.add, 0.0), _scan_chunk(lf, 1, True, jnp.add, 0.0))
        r = gi[:, cols_c] - b
        cmax = jnp.where(is_fw, _scan_chunk(r, 1, False, jnp.maximum, -jnp.inf),
                         _scan_chunk(r, 1, True, jnp.maximum, -jnp.inf))
        packed = jnp.where(qsel < ML_HEADS, r, jnp.where(qsel < 2 * ML_HEADS, cmax, b))
        gc_ref[cols_c, :] = jnp.concatenate([packed, pad_rows], axis=0).T
        gr_ref[:, cols_c] = jnp.concatenate([packed[0:ML_HEADS], packed[GATE_LANES_PER_DIR:GATE_LANES_PER_DIR + ML_HEADS]], axis=0)

    def conv(c0, cstep=256):
        acc = cb_ref[:, c0:c0 + cstep]
        for j in range(ML_CONV):
            r0 = HALO - ML_CONV // 2 + j
            acc = acc + zbuf[r0:r0 + tm, c0:c0 + cstep] * cw_ref[j:j + 1, c0:c0 + cstep]
        y = acc * _sigmoid(acc)
        if c0 < D_ML:
            qml_ref[:, c0:c0 + cstep] = (y * (ML_HEAD_DIM ** -0.5)).astype(BF16)
        else:
            kt_ref[c0 - D_ML:c0 - D_ML + cstep, :] = y.T.astype(BF16)

    def proj(w_ref, c0, scale=None):
        z = jnp.dot(a, w_ref[:, c0:c0 + D_NA], preferred_element_type=F32)
        return (z if scale is None else z * scale).astype(BF16)

    conv(0)
    qna_ref[...] = proj(wna_ref, 0, NA_HEAD_DIM ** -0.5 * LOG2E)
    conv(256)
    kna_ref[...] = proj(wna_ref, D_NA)
    conv(512)
    vna_ref[...] = proj(wna_ref, 2 * D_NA)
    conv(768)
    vml_ref[...] = proj(wvo_ref, 0)
    for c in range(tm // ML_CHUNK):
        gates(c)
    oml_ref[...] = proj(wvo_ref, D_ML)


def _in_proj(h, seq, g_mix, w_in, b_gate, conv_w, conv_b):
    t = h.shape[0]
    tm = IN_PROJ_TILE
    n_tiles = t // tm
    tiles_per_seq = seq // tm
    nh = 4 * ML_HEADS
    o_qk = 3 * D_NA
    o_v = o_qk + 2 * D_ML
    o_g = o_v + 2 * D_ML
    wna = w_in[:, :o_qk].astype(BF16)
    wqk = w_in[:, o_qk:o_v].astype(BF16)
    wvo = w_in[:, o_v:o_g].astype(BF16)
    wg = w_in[:, o_g:o_g + nh]
    ng = 2 * GATE_LANES_PER_DIR
    gate_id = np.arange(ng)
    src_i = (gate_id // GATE_LANES_PER_DIR) * 2 * ML_HEADS + gate_id % ML_HEADS
    wgr = jnp.concatenate([wg[:, src_i], wg[:, src_i + ML_HEADS]], axis=1).T.astype(BF16)
    bgr = jnp.concatenate([b_gate[src_i], b_gate[src_i + ML_HEADS]]).reshape(2 * ng, 1)
    cw = jnp.pad(conv_w, ((0, SUBLANES - ML_CONV), (0, 0)))
    cb = conv_b.reshape(1, 2 * D_ML)
    full = lambda shape: pl.BlockSpec(shape, lambda i: (0,) * len(shape))
    row_blk = lambda w: pl.BlockSpec((tm, w), lambda i: (i, 0))
    hb = tm // HALO
    out_bf = jax.ShapeDtypeStruct((t, D_NA), BF16)
    return pl.pallas_call(
        functools.partial(_in_proj_kernel, tiles_per_seq=tiles_per_seq),
        grid=(n_tiles,),
        in_specs=[
            pl.BlockSpec((HALO, D_MODEL), lambda i: (jnp.maximum(i * hb - 1, 0), 0)),
            row_blk(D_MODEL),
            pl.BlockSpec((HALO, D_MODEL), lambda i: (jnp.minimum((i + 1) * hb, t // HALO - 1), 0)),
            full((1, D_MODEL)), full((D_MODEL, o_qk)), full((D_MODEL, 2 * D_ML)), full((D_MODEL, 2 * D_ML)),
            full((2 * ng, D_MODEL)), full((2 * ng, 1)),
            full((SUBLANES, 2 * D_ML)), full((1, 2 * D_ML)),
        ],
        out_specs=[row_blk(D_NA)] * 4 + [pl.BlockSpec((D_ML, tm), lambda i: (0, i))] + [row_blk(D_NA)] * 2
        + [row_blk(LANES), pl.BlockSpec((2 * ML_HEADS, tm), lambda i: (0, i))],
        out_shape=[out_bf] * 4 + [jax.ShapeDtypeStruct((D_ML, t), BF16)] + [out_bf] * 2
        + [jax.ShapeDtypeStruct((t, LANES), F32), jax.ShapeDtypeStruct((2 * ML_HEADS, t), F32)],
        scratch_shapes=[pltpu.VMEM((tm + 2 * HALO, 2 * D_ML), F32)],
        compiler_params=_params(1),
        name="in_proj",
    )(h, h, h, g_mix.reshape(1, D_MODEL), wna, wqk, wvo, wgr, bgr, cw, cb)


def _na_bias_table(rpb, rows):
    kh = min(NA_WIN_H, rows)
    o = np.arange(kh)[:, None]
    kr = np.arange(kh)[None, :]
    dr = kr - o + (NA_WIN_H - 1)
    qc = np.arange(GRID_W)[:, None]
    kc = np.arange(GRID_W)[None, :]
    cs = np.clip(qc - NA_WIN_W // 2, 0, GRID_W - NA_WIN_W)
    inw = (kc >= cs) & (kc < cs + NA_WIN_W)
    dc = np.clip(kc - qc, 1 - NA_WIN_W, NA_WIN_W - 1) + (NA_WIN_W - 1)
    sel_r = jnp.asarray(dr[:, :, None] == np.arange(2 * NA_WIN_H - 1), F32)
    sel_c = jnp.asarray(dc[:, :, None] == np.arange(2 * NA_WIN_W - 1), F32)
    rows_sel = jnp.einsum("hrc,okr->hokc", rpb.astype(F32), sel_r, precision=lax.Precision.HIGHEST)
    b = jnp.einsum("hokc,qlc->hoqkl", rows_sel, sel_c, precision=lax.Precision.HIGHEST)
    b = jnp.where(inw[None, None, :, None, :], b * LOG2E, NEG_INF)
    return b.reshape(rpb.shape[0], kh, GRID_W, kh * GRID_W)


NA_HEADS_PER_BLOCK = 4
NA_BAND_ROWS = 32
NA_UNROLL = 8


def _na_kernel(q_ref, k_ref, v_ref, bias_ref, g_ref, o_ref, *, rows, kh, band):
    nkeys = kh * GRID_W
    hpb = NA_HEADS_PER_BLOCK
    m_rows = hpb * GRID_W
    width = hpb * NA_HEAD_DIM
    band_i = pl.program_id(2)
    lane_head = lax.shift_right_logical(lax.broadcasted_iota(jnp.int32, (m_rows, width), 1), 6)
    row_head = lax.shift_right_logical(lax.broadcasted_iota(jnp.int32, (m_rows, width), 0), 6)
    own = lane_head == row_head
    g = g_ref[...]

    def one_row(rl):
        r = band_i * band + rl
        rs = jnp.clip(r - kh // 2, 0, rows - kh)
        off = r - rs
        q = q_ref[0, pl.ds(pl.multiple_of(rl * GRID_W, GRID_W), GRID_W), :]
        kb = k_ref[0, pl.ds(pl.multiple_of(rs * GRID_W, GRID_W), nkeys), :]
        vb = v_ref[0, pl.ds(pl.multiple_of(rs * GRID_W, GRID_W), nkeys), :]
        qs = jnp.where(own, jnp.concatenate([q] * hpb, axis=0), jnp.zeros((m_rows, width), BF16))
        s = lax.dot_general(qs, kb, (((1,), (1,)), ((), ())), preferred_element_type=F32)
        s = s + bias_ref[:, off].reshape(m_rows, nkeys)
        m = jnp.max(s, axis=-1, keepdims=True)
        p = jnp.exp2(s - m)
        l = jnp.sum(p, axis=-1, keepdims=True)
        pv = jnp.dot(p.astype(BF16), vb, preferred_element_type=F32)
        oh = jnp.where(own, pv * (1.0 / l), 0.0)
        ms = jnp.sum(oh * oh, axis=-1, keepdims=True) * (1.0 / NA_HEAD_DIM)
        yn = oh * lax.rsqrt(ms + EPS)
        y = yn[0:GRID_W]
        for hh in range(1, hpb):
            y = y + yn[hh * GRID_W:(hh + 1) * GRID_W]
        o_ref[0, pl.ds(pl.multiple_of(rl * GRID_W, GRID_W), GRID_W), :] = (y * g).astype(BF16)

    def body(it, carry):
        for u in range(NA_UNROLL):
            one_row(it * NA_UNROLL + u)
        return carry

    lax.fori_loop(0, band // NA_UNROLL, body, 0)


def _na_attention(q, k, v, bias, g_na, batch, seq):
    rows = seq // GRID_W
    kh = min(NA_WIN_H, rows)
    hpb = NA_HEADS_PER_BLOCK
    width = hpb * NA_HEAD_DIM
    band = min(NA_BAND_ROWS, rows)
    assert rows % band == 0 and band % NA_UNROLL == 0
    q3, k3, v3 = (x.reshape(batch, seq, D_NA) for x in (q, k, v))
    seq_blk = pl.BlockSpec((1, seq, width), lambda b, j, r: (b, 0, j))
    band_blk = pl.BlockSpec((1, band * GRID_W, width), lambda b, j, r: (b, r, j))
    out = pl.pallas_call(
        functools.partial(_na_kernel, rows=rows, kh=kh, band=band),
        grid=(batch, D_NA // width, rows // band),
        in_specs=[band_blk, seq_blk, seq_blk,
                  pl.BlockSpec((hpb, kh, GRID_W, kh * GRID_W), lambda b, j, r: (j, 0, 0, 0)),
                  pl.BlockSpec((1, width), lambda b, j, r: (0, j))],
        out_specs=band_blk,
        out_shape=jax.ShapeDtypeStruct((batch, seq, D_NA), BF16),
        compiler_params=_params(3),
        name="na_attn",
    )(q3, k3, v3, bias, g_na.reshape(1, D_NA))
    return out.reshape(batch * seq, D_NA)


ML_UNROLL = 4
ML_HEADS_PER_STEP = 1


def _mlstm_kernel(q_ref, kt_ref, v_ref, gr_ref, gc_ref, o_ref, st_ref, *, n_chunks):
    L = ML_CHUNK
    hps = ML_HEADS_PER_STEP
    lane = lax.broadcasted_iota(jnp.int32, (L, LANES), 1)
    grow_i = lax.broadcasted_iota(jnp.int32, (2 * ML_HEADS, L), 0)
    row_i = lax.broadcasted_iota(jnp.int32, (L, L), 0)
    col_i = lax.broadcasted_iota(jnp.int32, (L, L), 1)
    ones = jnp.ones((L, ML_HEAD_DIM), BF16)

    def step(c, m_prev, hb, reverse):
        off = pl.multiple_of(c * L, L)
        d = 1 if reverse else 0
        hd = pl.program_id(1) * hps + hb
        cols = slice(hb * ML_HEAD_DIM, (hb + 1) * ML_HEAD_DIM)
        qc = q_ref[0, pl.ds(off, L), cols]
        kt = kt_ref[cols, pl.ds(off, L)]
        vc = v_ref[0, pl.ds(off, L), cols]
        gcol = gc_ref[pl.ds(off, L), :]
        grow = gr_ref[:, pl.ds(off, L)]

        def col(qn):
            sel = lane == d * GATE_LANES_PER_DIR + qn * ML_HEADS + hd
            return jnp.sum(jnp.where(sel, gcol, 0.0), axis=1, keepdims=True)

        r_col, cmax_col, b_col = col(0), col(1), col(2)
        r_row = jnp.sum(jnp.where(grow_i == d * ML_HEADS + hd, grow, 0.0), axis=0, keepdims=True)
        end = 0 if reverse else L - 1
        mm_end = jnp.maximum(m_prev, cmax_col[end:end + 1, :])
        m_new = b_col[end:end + 1, :] + mm_end
        mm_col = jnp.maximum(m_prev, cmax_col)
        causal = (col_i >= row_i) if reverse else (col_i <= row_i)
        dmat = jnp.exp(jnp.where(causal, r_row - mm_col, -jnp.inf))
        s = jnp.dot(qc, kt, preferred_element_type=F32) * dmat
        w_inter = jnp.exp(m_prev - mm_col)
        floor = jnp.exp(-(b_col + mm_col))
        w_col = jnp.exp(r_col - mm_end)
        decay = jnp.exp(m_prev - mm_end)
        vext = jnp.concatenate([vc, ones], axis=1)
        state = st_ref[d, hb]
        nd = (jnp.dot(s.astype(BF16), vext, preferred_element_type=F32)
              + w_inter * jnp.dot(qc, state.astype(BF16), preferred_element_type=F32))
        h = nd[:, :ML_HEAD_DIM] / jnp.maximum(jnp.abs(nd[:, ML_HEAD_DIM:]), floor)
        wv = jnp.concatenate([(w_col * vc.astype(F32)).astype(BF16),
                              jnp.broadcast_to(w_col, (L, ML_HEAD_DIM)).astype(BF16)], axis=1)
        st_ref[d, hb] = decay * state + jnp.dot(kt, wv, preferred_element_type=F32)
        return h, m_new

    st_ref[...] = jnp.zeros_like(st_ref)

    def sweep(accumulate):
        def body(it, carry):
            m = list(carry)
            for u in range(ML_UNROLL):
                j = it * ML_UNROLL + u
                jb = n_chunks - 1 - j
                rows_f = pl.ds(pl.multiple_of(j * L, L), L)
                rows_b = pl.ds(pl.multiple_of(jb * L, L), L)
                for hb in range(hps):
                    cols = slice(hb * ML_HEAD_DIM, (hb + 1) * ML_HEAD_DIM)
                    h_f, m[hb] = step(j, m[hb], hb, False)
                    h_b, m[hps + hb] = step(jb, m[hps + hb], hb, True)
                    if accumulate:
                        o_ref[0, rows_f, cols] += h_f
                        o_ref[0, rows_b, cols] += h_b
                    else:
                        o_ref[0, rows_f, cols] = h_f
                        o_ref[0, rows_b, cols] = h_b
            return tuple(m)
        return body

    half = n_chunks // (2 * ML_UNROLL)
    zero = jnp.zeros((1, 1), F32)
    carry = lax.fori_loop(0, half, sweep(False), (zero,) * (2 * hps))
    lax.fori_loop(half, 2 * half, sweep(True), carry)


def _mlstm(q, kt, v, g_row, g_col, batch, seq):
    n_chunks = seq // ML_CHUNK
    assert n_chunks % (2 * ML_UNROLL) == 0
    hps = ML_HEADS_PER_STEP
    width = hps * ML_HEAD_DIM
    q3, v3 = (x.reshape(batch, seq, D_ML) for x in (q, v))
    seq_blk = pl.BlockSpec((1, seq, width), lambda b, j: (b, 0, j))
    once_per_batch = {}
    out = pl.pallas_call(
        functools.partial(_mlstm_kernel, n_chunks=n_chunks),
        grid=(batch, ML_HEADS // hps),
        in_specs=[seq_blk, pl.BlockSpec((width, seq), lambda b, j: (j, b)), seq_blk,
                  pl.BlockSpec((2 * ML_HEADS, seq), lambda b, j: (0, b), **once_per_batch),
                  pl.BlockSpec((seq, LANES), lambda b, j: (b, 0), **once_per_batch)],
        out_specs=seq_blk,
        out_shape=jax.ShapeDtypeStruct((batch, seq, D_ML), F32),
        scratch_shapes=[pltpu.VMEM((2, hps, ML_HEAD_DIM, 2 * ML_HEAD_DIM), F32)],
        compiler_params=_params(2),
        name="mlstm",
    )(q3, kt, v3, g_row, g_col)
    return out.reshape(batch * seq, D_ML)


def _mix_out_kernel(h_ref, yna_ref, hml_ref, oml_ref, gml_ref, wo_ref, gmoe_ref, wrh_ref, wrl_ref, br_ref, tri_ref,
                    h1_ref, t_ref, cnt_ref):
    i = pl.program_id(0)
    tm = h_ref.shape[0]

    @pl.when(i == 0)
    def _():
        cnt_ref[...] = jnp.zeros_like(cnt_ref)

    hml = hml_ref[...]
    parts = [_rms_scale(hml[:, d * ML_HEAD_DIM:(d + 1) * ML_HEAD_DIM]) for d in range(ML_HEADS)]
    yml = _sigmoid(oml_ref[...].astype(F32)) * (jnp.concatenate(parts, axis=1) * gml_ref[...])
    mix = (jnp.dot(yna_ref[...], wo_ref[0:D_NA, :], preferred_element_type=F32)
           + jnp.dot(yml.astype(BF16), wo_ref[D_NA:D_NA + D_ML, :], preferred_element_type=F32))
    h1 = h_ref[...] + mix
    h1_ref[...] = h1
    t = _rms_scale(h1) * gmoe_ref[...]
    bits = lax.bitcast_convert_type(t.astype(BF16).astype(F32), U32)
    t_ref[:, 0:PACKED] = bits[:, PACKED:] | lax.shift_right_logical(bits[:, :PACKED], U32(16))

    t_hi = t.astype(BF16)
    t_lo = (t - t_hi.astype(F32)).astype(BF16)
    nt = (((1,), (1,)), ((), ()))
    logits = (lax.dot_general(wrh_ref[...], t_hi, nt, preferred_element_type=F32)
              + lax.dot_general(wrh_ref[...], t_lo, nt, preferred_element_type=F32)
              + lax.dot_general(wrl_ref[...], t_hi, nt, preferred_element_type=F32)) + br_ref[...]
    epg = EXPERTS_PER_GROUP
    row8 = lax.broadcasted_iota(jnp.int32, (epg, tm), 0)

    def top(x):
        mx = jnp.max(x, axis=0, keepdims=True)
        idx = jnp.min(jnp.where(x == mx, row8, epg), axis=0, keepdims=True)
        return mx, idx

    lg = jnp.where(row8 < N_GROUPS, logits[0:epg], -jnp.inf)
    g_max, g_idx = top(lg)
    pg_top = 1.0 / jnp.sum(jnp.exp(lg - g_max), axis=0, keepdims=True)
    le = jnp.zeros((epg, tm), F32)
    for grp in range(N_GROUPS):
        le = jnp.where(g_idx == grp, logits[EXPERT_LANE0 + grp * epg:EXPERT_LANE0 + (grp + 1) * epg], le)
    e1_max, i1 = top(le)
    e_sum = jnp.sum(jnp.exp(le - e1_max), axis=0, keepdims=True)
    e2_max, i2 = top(jnp.where(row8 == i1, -jnp.inf, le))
    p1 = 1.0 / e_sum
    p2 = jnp.exp(e2_max - e1_max) / e_sum
    w1 = pg_top * p1 / (p1 + p2)
    w2 = pg_top * p2 / (p1 + p2)

    la = jnp.minimum(i1, i2)
    lb = jnp.maximum(i1, i2)
    cls = g_idx * PAIRS_PER_GROUP + lax.shift_right_logical(la * (2 * epg - 1 - la), 1) + (lb - la - 1)
    w_lo = jnp.where(i1 < i2, w1, w2)
    w_hi = jnp.where(i1 < i2, w2, w1)
    oh = (lax.broadcasted_iota(jnp.int32, (LANES, tm), 0) == cls).astype(F32)
    before = jnp.dot(oh.astype(BF16), tri_ref[...], preferred_element_type=F32) + cnt_ref[:, 0:1]
    rank = jnp.sum(oh * before, axis=0, keepdims=True)
    cnt_ref[...] = cnt_ref[...] + jnp.sum(oh, axis=1, keepdims=True)
    info = jnp.concatenate([cls.astype(F32), rank, w_lo, w_hi, jnp.zeros((LANES - 4, tm), F32)], axis=0)
    for c in range(tm // LANES):
        t_ref[c * LANES:(c + 1) * LANES, PACKED:PACKED + LANES] = lax.bitcast_convert_type(
            info[:, c * LANES:(c + 1) * LANES].T, U32)


def _mix_out(h, yna, hml, oml, g_ml, w_out, g_moe, w_rg, b_rg, w_re, b_re):
    t = h.shape[0]
    tm = ROW_TILE
    wr = jnp.zeros((LANES, D_MODEL), F32)
    wr = wr.at[:N_GROUPS].set(w_rg.T).at[EXPERT_LANE0:EXPERT_LANE0 + N_EXPERTS].set(w_re.T)
    wrh = wr.astype(BF16)
    wrl = (wr - wrh.astype(F32)).astype(BF16)
    br = jnp.zeros((LANES, 1), F32)
    br = br.at[:N_GROUPS, 0].set(b_rg).at[EXPERT_LANE0:EXPERT_LANE0 + N_EXPERTS, 0].set(b_re)
    tri = jnp.asarray(np.triu(np.ones((tm, tm), np.float32), 1), BF16)
    full = lambda shape: pl.BlockSpec(shape, lambda i: (0,) * len(shape))
    row_blk = lambda w: pl.BlockSpec((tm, w), lambda i: (i, 0))
    return pl.pallas_call(
        _mix_out_kernel,
        grid=(t // tm,),
        in_specs=[row_blk(D_MODEL), row_blk(D_NA), row_blk(D_ML), row_blk(D_ML), full((1, D_ML)),
                  full((D_MODEL, D_MODEL)), full((1, D_MODEL)), full((LANES, D_MODEL)), full((LANES, D_MODEL)),
                  full((LANES, 1)), full((tm, tm))],
        out_specs=[row_blk(D_MODEL), row_blk(ROUTED_WIDTH), full((LANES, LANES))],
        out_shape=[jax.ShapeDtypeStruct((t, D_MODEL), F32), jax.ShapeDtypeStruct((t, ROUTED_WIDTH), U32),
                   jax.ShapeDtypeStruct((LANES, LANES), F32)],
        compiler_params=_params(1),
        name="mix_out",
    )(h, yna, hml, oml, g_ml.reshape(1, D_ML), w_out.astype(BF16), g_moe.reshape(1, D_MODEL), wrh, wrl, br, tri)


def _dispatch_kernel(zflag_ref, dest_ref, t_ref, xs_ref, zbuf, ring, sem, zsem, *, n_blocks, n_tiles):
    tm = t_ref.shape[0]

    @pl.when(pl.program_id(0) == 0)
    def _():
        zbuf[...] = jnp.zeros_like(zbuf)

        def zero_copy(b):
            return pltpu.make_async_copy(zbuf, xs_ref.at[pl.ds(pl.multiple_of(b * MOE_BLOCK, MOE_BLOCK), MOE_BLOCK)], zsem)

        def zissue(b, carry):
            @pl.when(zflag_ref[b] != 0)
            def _():
                zero_copy(b).start()
            return carry

        def zwait(b, carry):
            @pl.when(zflag_ref[b] != 0)
            def _():
                zero_copy(b).wait()
            return carry

        lax.fori_loop(0, n_blocks, zissue, 0)
        lax.fori_loop(0, n_blocks, zwait, 0)

    i = pl.program_id(0)
    slot = i % 2

    def wait_slot(s):
        pltpu.make_async_copy(ring.at[s], xs_ref.at[pl.ds(0, tm)], sem.at[s]).wait()

    @pl.when(i >= 2)
    def _():
        wait_slot(slot)

    ring[slot] = t_ref[...]

    def issue(it, carry):
        for u in range(ISSUE_UNROLL):
            j = it * ISSUE_UNROLL + u
            pltpu.make_async_copy(ring.at[slot, pl.ds(j, 1)], xs_ref.at[pl.ds(dest_ref[0, 0, j], 1)], sem.at[slot]).start()
        return carry

    lax.fori_loop(0, tm // ISSUE_UNROLL, issue, 0)

    @pl.when(i == n_tiles - 1)
    def _():
        wait_slot(slot)

        @pl.when(i >= 1)
        def _():
            wait_slot(1 - slot)


def _dispatch(t, dest, zflag, n_blocks):
    tt = t.shape[0]
    tm = ROW_TILE
    n_tiles = tt // tm
    return pl.pallas_call(
        functools.partial(_dispatch_kernel, n_blocks=n_blocks, n_tiles=n_tiles),
        grid_spec=pltpu.PrefetchScalarGridSpec(
            num_scalar_prefetch=1,
            grid=(n_tiles,),
            in_specs=[pl.BlockSpec((1, 1, tm), lambda i, zf: (i, 0, 0), memory_space=pltpu.SMEM),
                      pl.BlockSpec((tm, ROUTED_WIDTH), lambda i, zf: (i, 0))],
            out_specs=pl.BlockSpec(memory_space=pl.ANY),
            scratch_shapes=[pltpu.VMEM((MOE_BLOCK, ROUTED_WIDTH), U32), pltpu.VMEM((2, tm, ROUTED_WIDTH), U32),
                            pltpu.SemaphoreType.DMA((2,)), pltpu.SemaphoreType.DMA],
        ),
        out_shape=jax.ShapeDtypeStruct((n_blocks * MOE_BLOCK, ROUTED_WIDTH), U32),
        compiler_params=_params(1),
        name="dispatch",
    )(zflag, dest.reshape(n_tiles, 1, tm), t)


def _ffn_kernel(ea_ref, eb_ref, nu_ref, x_ref, wg_hbm, wu_hbm, wd_hbm, y_ref,
                wg_s, wu_s, wd_s, stage_g, stage_u, stage_d, sem, *, layer):
    i = pl.program_id(0)
    used = i < nu_ref[0]
    group = lax.shift_right_logical(ea_ref[i], 3)
    prev_group = lax.shift_right_logical(ea_ref[jnp.maximum(i - 1, 0)], 3)
    mats = ((wg_hbm, stage_g, wg_s), (wu_hbm, stage_u, wu_s), (wd_hbm, stage_d, wd_s))

    @pl.when(used & ((i == 0) | (group != prev_group)))
    def _():
        def fetch(m, e):
            w_hbm, stage, _ = mats[m]
            return pltpu.make_async_copy(w_hbm.at[layer, group * EXPERTS_PER_GROUP + e], stage, sem.at[m])

        for m in range(3):
            fetch(m, 0).start()

        def load_expert(e, carry):
            for m in range(3):
                _, stage, w_s = mats[m]
                fetch(m, e).wait()
                w_s[e] = stage[...].astype(BF16)

                @pl.when(e + 1 < EXPERTS_PER_GROUP)
                def _():
                    fetch(m, e + 1).start()
            return carry

        lax.fori_loop(0, EXPERTS_PER_GROUP, load_expert, 0)

    def swiglu(x, e):
        g = jnp.dot(x, wg_s[e], preferred_element_type=F32)
        u = jnp.dot(x, wu_s[e], preferred_element_type=F32)
        hid = (g * _sigmoid(g) * u).astype(BF16)
        return jnp.dot(hid, wd_s[e], preferred_element_type=F32)

    @pl.when(used)
    def _():
        words = x_ref[:, 0:PACKED]
        as_f32 = lambda u: lax.bitcast_convert_type(u, F32)
        x = jnp.concatenate([as_f32(lax.shift_left(words, U32(16))).astype(BF16),
                             as_f32(words & U32(0xFFFF0000)).astype(BF16)], axis=1)
        y_ref[...] = (as_f32(x_ref[:, INFO_W_LO:INFO_W_LO + 1]) * swiglu(x, ea_ref[i] & (EXPERTS_PER_GROUP - 1))
                      + as_f32(x_ref[:, INFO_W_HI:INFO_W_HI + 1]) * swiglu(x, eb_ref[i] & (EXPERTS_PER_GROUP - 1)))

    @pl.when(jnp.logical_not(used))
    def _():
        y_ref[...] = jnp.zeros_like(y_ref)


def _expert_ffn(xs, block_ea, block_eb, n_used, w_gate, w_up, w_down, layer):
    n_rows = xs.shape[0]
    nb = n_rows // MOE_BLOCK
    blk = lambda i, ea, eb, nu: (jnp.minimum(i, nu[0] - 1), 0)
    hbm = pl.BlockSpec(memory_space=pl.ANY)
    return pl.pallas_call(
        functools.partial(_ffn_kernel, layer=layer),
        grid_spec=pltpu.PrefetchScalarGridSpec(
            num_scalar_prefetch=3,
            grid=(nb,),
            in_specs=[pl.BlockSpec((MOE_BLOCK, ROUTED_WIDTH), blk), hbm, hbm, hbm],
            out_specs=pl.BlockSpec((MOE_BLOCK, D_MODEL), lambda i, ea, eb, nu: (i, 0)),
            scratch_shapes=[pltpu.VMEM((EXPERTS_PER_GROUP, D_MODEL, D_EXPERT), BF16),
                            pltpu.VMEM((EXPERTS_PER_GROUP, D_MODEL, D_EXPERT), BF16),
                            pltpu.VMEM((EXPERTS_PER_GROUP, D_EXPERT, D_MODEL), BF16),
                            pltpu.VMEM((D_MODEL, D_EXPERT), F32), pltpu.VMEM((D_MODEL, D_EXPERT), F32),
                            pltpu.VMEM((D_EXPERT, D_MODEL), F32), pltpu.SemaphoreType.DMA((3,))],
        ),
        out_shape=jax.ShapeDtypeStruct((n_rows, D_MODEL), F32),
        compiler_params=_params(1),
        name="expert_ffn",
    )(block_ea, block_eb, n_used, xs, w_gate, w_up, w_down)


def _combine_kernel(dcur_ref, dnxt_ref, h1_ref, p_ref, gple_ref, wple_ref, wpg_ref, gfin_ref, ys_ref,
                    o_ref, buf, sem, *, final, n_tiles):
    i = pl.program_id(0)
    tm = h1_ref.shape[0]
    slot = i % 2

    def issue_tile(d_ref, s):
        def issue(it, carry):
            for u in range(ISSUE_UNROLL):
                j = it * ISSUE_UNROLL + u
                pltpu.make_async_copy(ys_ref.at[pl.ds(d_ref[0, 0, j], 1)], buf.at[s, pl.ds(j, 1)], sem.at[s]).start()
            return carry

        lax.fori_loop(0, tm // ISSUE_UNROLL, issue, 0)

    @pl.when(i == 0)
    def _():
        issue_tile(dcur_ref, 0)

    @pl.when(i + 1 < n_tiles)
    def _():
        issue_tile(dnxt_ref, 1 - slot)

    pltpu.make_async_copy(ys_ref.at[pl.ds(0, tm)], buf.at[slot], sem.at[slot]).wait()

    h2 = h1_ref[...] + buf[slot]
    a = (_rms_scale(h2) * gple_ref[...]).astype(BF16)
    gate = _sigmoid(jnp.dot(a, wpg_ref[...], preferred_element_type=F32))
    emb = jnp.dot(p_ref[0].astype(BF16), wple_ref[...], preferred_element_type=F32)
    h3 = h2 + emb * gate
    if final:
        h3 = _rms_scale(h3) * gfin_ref[...]
    o_ref[...] = h3


def _combine(h1, ys, dest, p_all, layer, g_ple, w_ple, w_pg, g_final, final):
    tt = h1.shape[0]
    tm = ROW_TILE
    n_tiles = tt // tm
    dest3 = dest.reshape(n_tiles, 1, tm)
    full = lambda shape: pl.BlockSpec(shape, lambda i: (0,) * len(shape))
    row_blk = lambda w: pl.BlockSpec((tm, w), lambda i: (i, 0))
    return pl.pallas_call(
        functools.partial(_combine_kernel, final=final, n_tiles=n_tiles),
        grid=(n_tiles,),
        in_specs=[pl.BlockSpec((1, 1, tm), lambda i: (i, 0, 0), memory_space=pltpu.SMEM),
                  pl.BlockSpec((1, 1, tm), lambda i: (jnp.minimum(i + 1, n_tiles - 1), 0, 0), memory_space=pltpu.SMEM),
                  row_blk(D_MODEL), pl.BlockSpec((1, tm, D_PLE), lambda i: (layer, i, 0)), full((1, D_MODEL)),
                  full((D_PLE, D_MODEL)), full((D_MODEL, D_MODEL)), full((1, D_MODEL)),
                  pl.BlockSpec(memory_space=pl.ANY)],
        out_specs=row_blk(D_MODEL),
        out_shape=jax.ShapeDtypeStruct((tt, D_MODEL), F32),
        scratch_shapes=[pltpu.VMEM((2, tm, D_MODEL), F32), pltpu.SemaphoreType.DMA((2,))],
        compiler_params=_params(1),
        name="combine",
    )(dest3, dest3, h1, p_all, g_ple.reshape(1, D_MODEL), w_ple.astype(BF16), w_pg.astype(BF16),
      g_final.reshape(1, D_MODEL), ys)


def _class_experts():
    lo, hi = [], []
    for g in range(N_GROUPS):
        for a in range(EXPERTS_PER_GROUP):
            for b in range(a + 1, EXPERTS_PER_GROUP):
                lo.append(g * EXPERTS_PER_GROUP + a)
                hi.append(g * EXPERTS_PER_GROUP + b)
    return np.asarray(lo, np.int32), np.asarray(hi, np.int32)


def _routing_tables(routed, counts_f, n_blocks):
    info = lax.bitcast_convert_type(routed[:, INFO_CLASS:INFO_RANK + 1], F32)
    cls = info[:, 0].astype(jnp.int32)
    rank = info[:, 1].astype(jnp.int32)
    counts = counts_f[:N_CLASSES, 0].astype(jnp.int32)
    padded = ((counts + MOE_BLOCK - 1) // MOE_BLOCK) * MOE_BLOCK
    pad_end = jnp.cumsum(padded)
    pad_start = pad_end - padded
    ids = jnp.arange(N_CLASSES, dtype=jnp.int32)
    dest = jnp.sum(jnp.where(cls[:, None] == ids, pad_start, 0), axis=-1) + rank
    n_used = (pad_end[-1] // MOE_BLOCK).astype(jnp.int32).reshape(1)
    starts = jnp.arange(n_blocks, dtype=jnp.int32) * MOE_BLOCK
    block_cls = jnp.minimum(jnp.sum((pad_end[None, :] <= starts[:, None]).astype(jnp.int32), axis=1), N_CLASSES - 1)
    pick = lambda table: jnp.sum(jnp.where(block_cls[:, None] == ids, table, 0), axis=-1)
    cls_lo, cls_hi = _class_experts()
    filled = jnp.clip(pick(counts) - (starts - pick(pad_start)), 0, MOE_BLOCK)
    zflag = (filled < MOE_BLOCK).astype(jnp.int32)
    return dest, pick(jnp.asarray(cls_lo)), pick(jnp.asarray(cls_hi)), n_used, zflag


def kernel(x, p, w_in, b_gate, conv_w, conv_b, rpb, g_na, g_ml, w_out, g_mix, g_moe, w_route_group, b_route_group, w_route_expert, b_route_expert, w_exp_gate, w_exp_up, w_exp_down, g_ple, w_ple, w_ple_gate, g_final):
    batch, seq, _ = x.shape
    depth = w_in.shape[0]
    tt = batch * seq
    assert seq % ROW_TILE == 0 and seq % GRID_W == 0
    n_blocks = tt // MOE_BLOCK + N_CLASSES
    h = x.reshape(tt, D_MODEL)
    for i in range(depth):
        qna, kna, vna, qml, kt_ml, vml, oml, g_col, g_row = _in_proj(h, seq, g_mix[i], w_in[i], b_gate[i], conv_w[i], conv_b[i])
        yna = _na_attention(qna, kna, vna, _na_bias_table(rpb[i], seq // GRID_W), g_na[i], batch, seq)
        hml = _mlstm(qml, kt_ml, vml, g_row, g_col, batch, seq)
        h1, routed, counts = _mix_out(h, yna, hml, oml, g_ml[i], w_out[i], g_moe[i],
                                      w_route_group[i], b_route_group[i], w_route_expert[i], b_route_expert[i])
        dest, block_ea, block_eb, n_used, zflag = _routing_tables(routed, counts, n_blocks)
        xs = _dispatch(routed, dest, zflag, n_blocks)
        ys = _expert_ffn(xs, block_ea, block_eb, n_used, w_exp_gate, w_exp_up, w_exp_down, i)
        h = _combine(h1, ys, dest, p.reshape(depth, tt, D_PLE), i, g_ple[i], w_ple[i], w_ple_gate[i], g_final,
                     final=(i == depth - 1))
    return h.reshape(batch, seq, D_MODEL)
```

```python
import functools

import numpy as np
import jax
import jax.numpy as jnp
from jax import lax
from jax.experimental import pallas as pl
from jax.experimental.pallas import tpu as pltpu

F32 = jnp.float32
BF16 = jnp.bfloat16

D_MODEL = 1024
GRID_W = 64
D_NA = 512
NA_HEADS = 8
NA_HEAD_DIM = 64
NA_WIN_H = 8
NA_WIN_W = 16
D_ML = 512
ML_HEADS = 4
ML_HEAD_DIM = 128
ML_CONV = 5
ML_CHUNK = 128
N_GROUPS = 4
EXPERTS_PER_GROUP = 8
N_EXPERTS = 32
D_EXPERT = 512
D_PLE = 256
EPS = 1e-6
NEG_INF = -1e30
LOG2E = 1.4426950408889634

LANES = 128
SUBLANES = 8
ROW_TILE = 1024
IN_PROJ_TILE = 1024
MOE_BLOCK = 256
PAIRS_PER_GROUP = EXPERTS_PER_GROUP * (EXPERTS_PER_GROUP - 1) // 2
N_CLASSES = N_GROUPS * PAIRS_PER_GROUP
PACKED = D_MODEL // 2
ROUTED_WIDTH = PACKED + LANES
U32 = jnp.uint32
INFO_CLASS = PACKED
INFO_RANK = PACKED + 1
INFO_W_LO = PACKED + 2
INFO_W_HI = PACKED + 3
HALO = SUBLANES
ISSUE_UNROLL = 8
EXPERT_LANE0 = 8
GATE_LANES_PER_DIR = 3 * ML_HEADS
VMEM_LIMIT = 56 * 1024 * 1024


def _params(n_axes, flags=None):
    return pltpu.CompilerParams(dimension_semantics=("arbitrary",) * n_axes, vmem_limit_bytes=VMEM_LIMIT, flags=flags)


def _sigmoid(x):
    return 1.0 / (1.0 + jnp.exp(-x))


def _log_sigmoid(x):
    return jnp.minimum(x, 0.0) - jnp.log1p(jnp.exp(-jnp.abs(x)))


def _rms_scale(x):
    return x * lax.rsqrt(jnp.mean(x * x, axis=-1, keepdims=True) + EPS)


def _scan_chunk(x, axis, reverse, op, identity):
    n = x.shape[axis]
    idx = lax.broadcasted_iota(jnp.int32, x.shape, axis)
    sh = 1
    while sh < n:
        if reverse:
            x = op(x, jnp.where(idx < n - sh, pltpu.roll(x, n - sh, axis), identity))
        else:
            x = op(x, jnp.where(idx >= sh, pltpu.roll(x, sh, axis), identity))
        sh *= 2
    return x


def _in_proj_kernel(hp_ref, h_ref, hn_ref, g_ref, wna_ref, wqk_ref, wvo_ref, wgr_ref, bgr_ref, cw_ref, cb_ref,
                    qna_ref, kna_ref, vna_ref, qml_ref, kt_ref, vml_ref, oml_ref, gc_ref, gr_ref,
                    zbuf, *, tiles_per_seq):
    i = pl.program_id(0)
    tm = h_ref.shape[0]
    g = g_ref[...]

    def norm(x):
        return (_rms_scale(x) * g).astype(BF16)

    a = norm(h_ref[...])

    pos = i % tiles_per_seq
    zp = jnp.dot(norm(hp_ref[...]), wqk_ref[...], preferred_element_type=F32)
    zn = jnp.dot(norm(hn_ref[...]), wqk_ref[...], preferred_element_type=F32)
    zbuf[0:HALO, :] = jnp.where(pos == 0, 0.0, zp)
    zbuf[HALO:HALO + tm, :] = jnp.dot(a, wqk_ref[...], preferred_element_type=F32)
    zbuf[HALO + tm:2 * HALO + tm, :] = jnp.where(pos == tiles_per_seq - 1, 0.0, zn)

    ng = 2 * GATE_LANES_PER_DIR
    nt = (((1,), (1,)), ((), ()))
    gi = lax.dot_general(wgr_ref[0:ng, :], a, nt, preferred_element_type=F32) + bgr_ref[0:ng, :]
    gf = lax.dot_general(wgr_ref[ng:2 * ng, :], a, nt, preferred_element_type=F32) + bgr_ref[ng:2 * ng, :]
    row = lax.broadcasted_iota(jnp.int32, (ng, ML_CHUNK), 0)
    is_fw = row < GATE_LANES_PER_DIR
    qsel = jnp.where(is_fw, row, row - GATE_LANES_PER_DIR)
    pad_rows = jnp.zeros((ML_CHUNK - ng, ML_CHUNK), F32)

    def gates(c):
        cols_c = slice(c * ML_CHUNK, (c + 1) * ML_CHUNK)
        lf = _log_sigmoid(gf[:, cols_c])
        b = jnp.where(is_fw, _scan_chunk(lf, 1, False, jnp.add, 0.0), _scan_chunk(lf, 1, True, jnp.add, 0.0))
        r = gi[:, cols_c] - b
        cmax = jnp.where(is_fw, _scan_chunk(r, 1, False, jnp.maximum, -jnp.inf),
                         _scan_chunk(r, 1, True, jnp.maximum, -jnp.inf))
        packed = jnp.where(qsel < ML_HEADS, r, jnp.where(qsel < 2 * ML_HEADS, cmax, b))
        gc_ref[cols_c, :] = jnp.concatenate([packed, pad_rows], axis=0).T
        gr_ref[:, cols_c] = jnp.concatenate([packed[0:ML_HEADS], packed[GATE_LANES_PER_DIR:GATE_LANES_PER_DIR + ML_HEADS]], axis=0)

    def conv(c0, cstep=256):
        acc = cb_ref[:, c0:c0 + cstep]
        for j in range(ML_CONV):
            r0 = HALO - ML_CONV // 2 + j
            acc = acc + zbuf[r0:r0 + tm, c0:c0 + cstep] * cw_ref[j:j + 1, c0:c0 + cstep]
        y = acc * _sigmoid(acc)
        if c0 < D_ML:
            qml_ref[:, c0:c0 + cstep] = (y * (ML_HEAD_DIM ** -0.5)).astype(BF16)
        else:
            kt_ref[c0 - D_ML:c0 - D_ML + cstep, :] = y.T.astype(BF16)

    def proj(w_ref, c0, scale=None):
        z = jnp.dot(a, w_ref[:, c0:c0 + D_NA], preferred_element_type=F32)
        return (z if scale is None else z * scale).astype(BF16)

    conv(0)
    qna_ref[...] = proj(wna_ref, 0, NA_HEAD_DIM ** -0.5 * LOG2E)
    conv(256)
    kna_ref[...] = proj(wna_ref, D_NA)
    conv(512)
    vna_ref[...] = proj(wna_ref, 2 * D_NA)
    conv(768)
    vml_ref[...] = proj(wvo_ref, 0)
    for c in range(tm // ML_CHUNK):
        gates(c)
    oml_ref[...] = proj(wvo_ref, D_ML)


def _in_proj(h, seq, g_mix, w_in, b_gate, conv_w, conv_b):
    t = h.shape[0]
    tm = IN_PROJ_TILE
    n_tiles = t // tm
    tiles_per_seq = seq // tm
    nh = 4 * ML_HEADS
    o_qk = 3 * D_NA
    o_v = o_qk + 2 * D_ML
    o_g = o_v + 2 * D_ML
    wna = w_in[:, :o_qk].astype(BF16)
    wqk = w_in[:, o_qk:o_v].astype(BF16)
    wvo = w_in[:, o_v:o_g].astype(BF16)
    wg = w_in[:, o_g:o_g + nh]
    ng = 2 * GATE_LANES_PER_DIR
    gate_id = np.arange(ng)
    src_i = (gate_id // GATE_LANES_PER_DIR) * 2 * ML_HEADS + gate_id % ML_HEADS
    wgr = jnp.concatenate([wg[:, src_i], wg[:, src_i + ML_HEADS]], axis=1).T.astype(BF16)
    bgr = jnp.concatenate([b_gate[src_i], b_gate[src_i + ML_HEADS]]).reshape(2 * ng, 1)
    cw = jnp.pad(conv_w, ((0, SUBLANES - ML_CONV), (0, 0)))
    cb = conv_b.reshape(1, 2 * D_ML)
    full = lambda shape: pl.BlockSpec(shape, lambda i: (0,) * len(shape))
    row_blk = lambda w: pl.BlockSpec((tm, w), lambda i: (i, 0))
    hb = tm // HALO
    out_bf = jax.ShapeDtypeStruct((t, D_NA), BF16)
    return pl.pallas_call(
        functools.partial(_in_proj_kernel, tiles_per_seq=tiles_per_seq),
        grid=(n_tiles,),
        in_specs=[
            pl.BlockSpec((HALO, D_MODEL), lambda i: (jnp.maximum(i * hb - 1, 0), 0)),
            row_blk(D_MODEL),
            pl.BlockSpec((HALO, D_MODEL), lambda i: (jnp.minimum((i + 1) * hb, t // HALO - 1), 0)),
            full((1, D_MODEL)), full((D_MODEL, o_qk)), full((D_MODEL, 2 * D_ML)), full((D_MODEL, 2 * D_ML)),
            full((2 * ng, D_MODEL)), full((2 * ng, 1)),
            full((SUBLANES, 2 * D_ML)), full((1, 2 * D_ML)),
        ],
        out_specs=[row_blk(D_NA)] * 4 + [pl.BlockSpec((D_ML, tm), lambda i: (0, i))] + [row_blk(D_NA)] * 2
        + [row_blk(LANES), pl.BlockSpec((2 * ML_HEADS, tm), lambda i: (0, i))],
        out_shape=[out_bf] * 4 + [jax.ShapeDtypeStruct((D_ML, t), BF16)] + [out_bf] * 2
        + [jax.ShapeDtypeStruct((t, LANES), F32), jax.ShapeDtypeStruct((2 * ML_HEADS, t), F32)],
        scratch_shapes=[pltpu.VMEM((tm + 2 * HALO, 2 * D_ML), F32)],
        compiler_params=_params(1),
        name="in_proj",
    )(h, h, h, g_mix.reshape(1, D_MODEL), wna, wqk, wvo, wgr, bgr, cw, cb)


def _na_bias_table(rpb, rows):
    kh = min(NA_WIN_H, rows)
    o = np.arange(kh)[:, None]
    kr = np.arange(kh)[None, :]
    dr = kr - o + (NA_WIN_H - 1)
    qc = np.arange(GRID_W)[:, None]
    kc = np.arange(GRID_W)[None, :]
    cs = np.clip(qc - NA_WIN_W // 2, 0, GRID_W - NA_WIN_W)
    inw = (kc >= cs) & (kc < cs + NA_WIN_W)
    dc = np.clip(kc - qc, 1 - NA_WIN_W, NA_WIN_W - 1) + (NA_WIN_W - 1)
    sel_r = jnp.asarray(dr[:, :, None] == np.arange(2 * NA_WIN_H - 1), F32)
    sel_c = jnp.asarray(dc[:, :, None] == np.arange(2 * NA_WIN_W - 1), F32)
    rows_sel = jnp.einsum("hrc,okr->hokc", rpb.astype(F32), sel_r, precision=lax.Precision.HIGHEST)
    b = jnp.einsum("hokc,qlc->hoqkl", rows_sel, sel_c, precision=lax.Precision.HIGHEST)
    b = jnp.where(inw[None, None, :, None, :], b * LOG2E, NEG_INF)
    return b.reshape(rpb.shape[0], kh, GRID_W, kh * GRID_W)


NA_HEADS_PER_BLOCK = 4
NA_BAND_ROWS = 32
NA_UNROLL = 8


def _na_kernel(q_ref, k_ref, v_ref, bias_ref, g_ref, o_ref, *, rows, kh, band):
    nkeys = kh * GRID_W
    hpb = NA_HEADS_PER_BLOCK
    m_rows = hpb * GRID_W
    width = hpb * NA_HEAD_DIM
    band_i = pl.program_id(2)
    lane_head = lax.shift_right_logical(lax.broadcasted_iota(jnp.int32, (m_rows, width), 1), 6)
    row_head = lax.shift_right_logical(lax.broadcasted_iota(jnp.int32, (m_rows, width), 0), 6)
    own = lane_head == row_head
    g = g_ref[...]

    def one_row(rl):
        r = band_i * band + rl
        rs = jnp.clip(r - kh // 2, 0, rows - kh)
        off = r - rs
        q = q_ref[0, pl.ds(pl.multiple_of(rl * GRID_W, GRID_W), GRID_W), :]
        kb = k_ref[0, pl.ds(pl.multiple_of(rs * GRID_W, GRID_W), nkeys), :]
        vb = v_ref[0, pl.ds(pl.multiple_of(rs * GRID_W, GRID_W), nkeys), :]
        qs = jnp.where(own, jnp.concatenate([q] * hpb, axis=0), jnp.zeros((m_rows, width), BF16))
        s = lax.dot_general(qs, kb, (((1,), (1,)), ((), ())), preferred_element_type=F32)
        s = s + bias_ref[:, off].reshape(m_rows, nkeys)
        m = jnp.max(s, axis=-1, keepdims=True)
        p = jnp.exp2(s - m)
        l = jnp.sum(p, axis=-1, keepdims=True)
        pv = jnp.dot(p.astype(BF16), vb, preferred_element_type=F32)
        oh = jnp.where(own, pv * (1.0 / l), 0.0)
        ms = jnp.sum(oh * oh, axis=-1, keepdims=True) * (1.0 / NA_HEAD_DIM)
        yn = oh * lax.rsqrt(ms + EPS)
        y = yn[0:GRID_W]
        for hh in range(1, hpb):
            y = y + yn[hh * GRID_W:(hh + 1) * GRID_W]
        o_ref[0, pl.ds(pl.multiple_of(rl * GRID_W, GRID_W), GRID_W), :] = (y * g).astype(BF16)

    def body(it, carry):
        for u in range(NA_UNROLL):
            one_row(it * NA_UNROLL + u)
        return carry

    lax.fori_loop(0, band // NA_UNROLL, body, 0)


def _na_attention(q, k, v, bias, g_na, batch, seq):
    rows = seq // GRID_W
    kh = min(NA_WIN_H, rows)
    hpb = NA_HEADS_PER_BLOCK
    width = hpb * NA_HEAD_DIM
    band = min(NA_BAND_ROWS, rows)
    assert rows % band == 0 and band % NA_UNROLL == 0
    q3, k3, v3 = (x.reshape(batch, seq, D_NA) for x in (q, k, v))
    seq_blk = pl.BlockSpec((1, seq, width), lambda b, j, r: (b, 0, j))
    band_blk = pl.BlockSpec((1, band * GRID_W, width), lambda b, j, r: (b, r, j))
    out = pl.pallas_call(
        functools.partial(_na_kernel, rows=rows, kh=kh, band=band),
        grid=(batch, D_NA // width, rows // band),
        in_specs=[band_blk, seq_blk, seq_blk,
                  pl.BlockSpec((hpb, kh, GRID_W, kh * GRID_W), lambda b, j, r: (j, 0, 0, 0)),
                  pl.BlockSpec((1, width), lambda b, j, r: (0, j))],
        out_specs=band_blk,
        out_shape=jax.ShapeDtypeStruct((batch, seq, D_NA), BF16),
        compiler_params=_params(3),
        name="na_attn",
    )(q3, k3, v3, bias, g_na.reshape(1, D_NA))
    return out.reshape(batch * seq, D_NA)


ML_UNROLL = 4
ML_HEADS_PER_STEP = 1


def _mlstm_kernel(q_ref, kt_ref, v_ref, gr_ref, gc_ref, o_ref, st_ref, *, n_chunks):
    L = ML_CHUNK
    hps = ML_HEADS_PER_STEP
    lane = lax.broadcasted_iota(jnp.int32, (L, LANES), 1)
    grow_i = lax.broadcasted_iota(jnp.int32, (2 * ML_HEADS, L), 0)
    row_i = lax.broadcasted_iota(jnp.int32, (L, L), 0)
    col_i = lax.broadcasted_iota(jnp.int32, (L, L), 1)
    ones = jnp.ones((L, ML_HEAD_DIM), BF16)

    def step(c, m_prev, hb, reverse):
        off = pl.multiple_of(c * L, L)
        d = 1 if reverse else 0
        hd = pl.program_id(1) * hps + hb
        cols = slice(hb * ML_HEAD_DIM, (hb + 1) * ML_HEAD_DIM)
        qc = q_ref[0, pl.ds(off, L), cols]
        kt = kt_ref[cols, pl.ds(off, L)]
        vc = v_ref[0, pl.ds(off, L), cols]
        gcol = gc_ref[pl.ds(off, L), :]
        grow = gr_ref[:, pl.ds(off, L)]

        def col(qn):
            sel = lane == d * GATE_LANES_PER_DIR + qn * ML_HEADS + hd
            return jnp.sum(jnp.where(sel, gcol, 0.0), axis=1, keepdims=True)

        r_col, cmax_col, b_col = col(0), col(1), col(2)
        r_row = jnp.sum(jnp.where(grow_i == d * ML_HEADS + hd, grow, 0.0), axis=0, keepdims=True)
        end = 0 if reverse else L - 1
        mm_end = jnp.maximum(m_prev, cmax_col[end:end + 1, :])
        m_new = b_col[end:end + 1, :] + mm_end
        mm_col = jnp.maximum(m_prev, cmax_col)
        causal = (col_i >= row_i) if reverse else (col_i <= row_i)
        dmat = jnp.exp(jnp.where(causal, r_row - mm_col, -jnp.inf))
        s = jnp.dot(qc, kt, preferred_element_type=F32) * dmat
        w_inter = jnp.exp(m_prev - mm_col)
        floor = jnp.exp(-(b_col + mm_col))
        w_col = jnp.exp(r_col - mm_end)
        decay = jnp.exp(m_prev - mm_end)
        vext = jnp.concatenate([vc, ones], axis=1)
        state = st_ref[d, hb]
        nd = (jnp.dot(s.astype(BF16), vext, preferred_element_type=F32)
              + w_inter * jnp.dot(qc, state.astype(BF16), preferred_element_type=F32))
        h = nd[:, :ML_HEAD_DIM] / jnp.maximum(jnp.abs(nd[:, ML_HEAD_DIM:]), floor)
        wv = jnp.concatenate([(w_col * vc.astype(F32)).astype(BF16),
                              jnp.broadcast_to(w_col, (L, ML_HEAD_DIM)).astype(BF16)], axis=1)
        st_ref[d, hb] = decay * state + jnp.dot(kt, wv, preferred_element_type=F32)
        return h, m_new

    st_ref[...] = jnp.zeros_like(st_ref)

    def sweep(accumulate):
        def body(it, carry):
            m = list(carry)
            for u in range(ML_UNROLL):
                j = it * ML_UNROLL + u
                jb = n_chunks - 1 - j
                rows_f = pl.ds(pl.multiple_of(j * L, L), L)
                rows_b = pl.ds(pl.multiple_of(jb * L, L), L)
                for hb in range(hps):
                    cols = slice(hb * ML_HEAD_DIM, (hb + 1) * ML_HEAD_DIM)
                    h_f, m[hb] = step(j, m[hb], hb, False)
                    h_b, m[hps + hb] = step(jb, m[hps + hb], hb, True)
                    if accumulate:
                        o_ref[0, rows_f, cols] += h_f
                        o_ref[0, rows_b, cols] += h_b
                    else:
                        o_ref[0, rows_f, cols] = h_f
                        o_ref[0, rows_b, cols] = h_b
            return tuple(m)
        return body

    half = n_chunks // (2 * ML_UNROLL)
    zero = jnp.zeros((1, 1), F32)
    carry = lax.fori_loop(0, half, sweep(False), (zero,) * (2 * hps))
    lax.fori_loop(half, 2 * half, sweep(True), carry)


def _mlstm(q, kt, v, g_row, g_col, batch, seq):
    n_chunks = seq // ML_CHUNK
    assert n_chunks % (2 * ML_UNROLL) == 0
    hps = ML_HEADS_PER_STEP
    width = hps * ML_HEAD_DIM
    q3, v3 = (x.reshape(batch, seq, D_ML) for x in (q, v))
    seq_blk = pl.BlockSpec((1, seq, width), lambda b, j: (b, 0, j))
    once_per_batch = {}
    out = pl.pallas_call(
        functools.partial(_mlstm_kernel, n_chunks=n_chunks),
        grid=(batch, ML_HEADS // hps),
        in_specs=[seq_blk, pl.BlockSpec((width, seq), lambda b, j: (j, b)), seq_blk,
                  pl.BlockSpec((2 * ML_HEADS, seq), lambda b, j: (0, b), **once_per_batch),
                  pl.BlockSpec((seq, LANES), lambda b, j: (b, 0), **once_per_batch)],
        out_specs=seq_blk,
        out_shape=jax.ShapeDtypeStruct((batch, seq, D_ML), F32),
        scratch_shapes=[pltpu.VMEM((2, hps, ML_HEAD_DIM, 2 * ML_HEAD_DIM), F32)],
        compiler_params=_params(2),
        name="mlstm",
    )(q3, kt, v3, g_row, g_col)
    return out.reshape(batch * seq, D_ML)


def _mix_out_kernel(h_ref, yna_ref, hml_ref, oml_ref, gml_ref, wo_ref, gmoe_ref, wrh_ref, wrl_ref, br_ref, tri_ref,
                    h1_ref, t_ref, cnt_ref):
    i = pl.program_id(0)
    tm = h_ref.shape[0]

    @pl.when(i == 0)
    def _():
        cnt_ref[...] = jnp.zeros_like(cnt_ref)

    hml = hml_ref[...]
    parts = [_rms_scale(hml[:, d * ML_HEAD_DIM:(d + 1) * ML_HEAD_DIM]) for d in range(ML_HEADS)]
    yml = _sigmoid(oml_ref[...].astype(F32)) * (jnp.concatenate(parts, axis=1) * gml_ref[...])
    mix = (jnp.dot(yna_ref[...], wo_ref[0:D_NA, :], preferred_element_type=F32)
           + jnp.dot(yml.astype(BF16), wo_ref[D_NA:D_NA + D_ML, :], preferred_element_type=F32))
    h1 = h_ref[...] + mix
    h1_ref[...] = h1
    t = _rms_scale(h1) * gmoe_ref[...]
    bits = lax.bitcast_convert_type(t.astype(BF16).astype(F32), U32)
    t_ref[:, 0:PACKED] = bits[:, PACKED:] | lax.shift_right_logical(bits[:, :PACKED], U32(16))

    t_hi = t.astype(BF16)
    t_lo = (t - t_hi.astype(F32)).astype(BF16)
    nt = (((1,), (1,)), ((), ()))
    logits = (lax.dot_general(wrh_ref[...], t_hi, nt, preferred_element_type=F32)
              + lax.dot_general(wrh_ref[...], t_lo, nt, preferred_element_type=F32)
              + lax.dot_general(wrl_ref[...], t_hi, nt, preferred_element_type=F32)) + br_ref[...]
    epg = EXPERTS_PER_GROUP
    row8 = lax.broadcasted_iota(jnp.int32, (epg, tm), 0)

    def top(x):
        mx = jnp.max(x, axis=0, keepdims=True)
        idx = jnp.min(jnp.where(x == mx, row8, epg), axis=0, keepdims=True)
        return mx, idx

    lg = jnp.where(row8 < N_GROUPS, logits[0:epg], -jnp.inf)
    g_max, g_idx = top(lg)
    pg_top = 1.0 / jnp.sum(jnp.exp(lg - g_max), axis=0, keepdims=True)
    le = jnp.zeros((epg, tm), F32)
    for grp in range(N_GROUPS):
        le = jnp.where(g_idx == grp, logits[EXPERT_LANE0 + grp * epg:EXPERT_LANE0 + (grp + 1) * epg], le)
    e1_max, i1 = top(le)
    e_sum = jnp.sum(jnp.exp(le - e1_max), axis=0, keepdims=True)
    e2_max, i2 = top(jnp.where(row8 == i1, -jnp.inf, le))
    p1 = 1.0 / e_sum
    p2 = jnp.exp(e2_max - e1_max) / e_sum
    w1 = pg_top * p1 / (p1 + p2)
    w2 = pg_top * p2 / (p1 + p2)

    la = jnp.minimum(i1, i2)
    lb = jnp.maximum(i1, i2)
    cls = g_idx * PAIRS_PER_GROUP + lax.shift_right_logical(la * (2 * epg - 1 - la), 1) + (lb - la - 1)
    w_lo = jnp.where(i1 < i2, w1, w2)
    w_hi = jnp.where(i1 < i2, w2, w1)
    oh = (lax.broadcasted_iota(jnp.int32, (LANES, tm), 0) == cls).astype(F32)
    before = jnp.dot(oh.astype(BF16), tri_ref[...], preferred_element_type=F32) + cnt_ref[:, 0:1]
    rank = jnp.sum(oh * before, axis=0, keepdims=True)
    cnt_ref[...] = cnt_ref[...] + jnp.sum(oh, axis=1, keepdims=True)
    info = jnp.concatenate([cls.astype(F32), rank, w_lo, w_hi, jnp.zeros((LANES - 4, tm), F32)], axis=0)
    for c in range(tm // LANES):
        t_ref[c * LANES:(c + 1) * LANES, PACKED:PACKED + LANES] = lax.bitcast_convert_type(
            info[:, c * LANES:(c + 1) * LANES].T, U32)


def _mix_out(h, yna, hml, oml, g_ml, w_out, g_moe, w_rg, b_rg, w_re, b_re):
    t = h.shape[0]
    tm = ROW_TILE
    wr = jnp.zeros((LANES, D_MODEL), F32)
    wr = wr.at[:N_GROUPS].set(w_rg.T).at[EXPERT_LANE0:EXPERT_LANE0 + N_EXPERTS].set(w_re.T)
    wrh = wr.astype(BF16)
    wrl = (wr - wrh.astype(F32)).astype(BF16)
    br = jnp.zeros((LANES, 1), F32)
    br = br.at[:N_GROUPS, 0].set(b_rg).at[EXPERT_LANE0:EXPERT_LANE0 + N_EXPERTS, 0].set(b_re)
    tri = jnp.asarray(np.triu(np.ones((tm, tm), np.float32), 1), BF16)
    full = lambda shape: pl.BlockSpec(shape, lambda i: (0,) * len(shape))
    row_blk = lambda w: pl.BlockSpec((tm, w), lambda i: (i, 0))
    return pl.pallas_call(
        _mix_out_kernel,
        grid=(t // tm,),
        in_specs=[row_blk(D_MODEL), row_blk(D_NA), row_blk(D_ML), row_blk(D_ML), full((1, D_ML)),
                  full((D_MODEL, D_MODEL)), full((1, D_MODEL)), full((LANES, D_MODEL)), full((LANES, D_MODEL)),
                  full((LANES, 1)), full((tm, tm))],
        out_specs=[row_blk(D_MODEL), row_blk(ROUTED_WIDTH), full((LANES, LANES))],
        out_shape=[jax.ShapeDtypeStruct((t, D_MODEL), F32), jax.ShapeDtypeStruct((t, ROUTED_WIDTH), U32),
                   jax.ShapeDtypeStruct((LANES, LANES), F32)],
        compiler_params=_params(1),
        name="mix_out",
    )(h, yna, hml, oml, g_ml.reshape(1, D_ML), w_out.astype(BF16), g_moe.reshape(1, D_MODEL), wrh, wrl, br, tri)


def _dispatch_kernel(zflag_ref, dest_ref, t_ref, xs_ref, zbuf, ring, sem, zsem, *, n_blocks, n_tiles):
    tm = t_ref.shape[0]

    @pl.when(pl.program_id(0) == 0)
    def _():
        zbuf[...] = jnp.zeros_like(zbuf)

        def zero_copy(b):
            return pltpu.make_async_copy(zbuf, xs_ref.at[pl.ds(pl.multiple_of(b * MOE_BLOCK, MOE_BLOCK), MOE_BLOCK)], zsem)

        def zissue(b, carry):
            @pl.when(zflag_ref[b] != 0)
            def _():
                zero_copy(b).start()
            return carry

        def zwait(b, carry):
            @pl.when(zflag_ref[b] != 0)
            def _():
                zero_copy(b).wait()
            return carry

        lax.fori_loop(0, n_blocks, zissue, 0)
        lax.fori_loop(0, n_blocks, zwait, 0)

    i = pl.program_id(0)
    slot = i % 2

    def wait_slot(s):
        pltpu.make_async_copy(ring.at[s], xs_ref.at[pl.ds(0, tm)], sem.at[s]).wait()

    @pl.when(i >= 2)
    def _():
        wait_slot(slot)

    ring[slot] = t_ref[...]

    def issue(it, carry):
        for u in range(ISSUE_UNROLL):
            j = it * ISSUE_UNROLL + u
            pltpu.make_async_copy(ring.at[slot, pl.ds(j, 1)], xs_ref.at[pl.ds(dest_ref[0, 0, j], 1)], sem.at[slot]).start()
        return carry

    lax.fori_loop(0, tm // ISSUE_UNROLL, issue, 0)

    @pl.when(i == n_tiles - 1)
    def _():
        wait_slot(slot)

        @pl.when(i >= 1)
        def _():
            wait_slot(1 - slot)


def _dispatch(t, dest, zflag, n_blocks):
    tt = t.shape[0]
    tm = ROW_TILE
    n_tiles = tt // tm
    return pl.pallas_call(
        functools.partial(_dispatch_kernel, n_blocks=n_blocks, n_tiles=n_tiles),
        grid_spec=pltpu.PrefetchScalarGridSpec(
            num_scalar_prefetch=1,
            grid=(n_tiles,),
            in_specs=[pl.BlockSpec((1, 1, tm), lambda i, zf: (i, 0, 0), memory_space=pltpu.SMEM),
                      pl.BlockSpec((tm, ROUTED_WIDTH), lambda i, zf: (i, 0))],
            out_specs=pl.BlockSpec(memory_space=pl.ANY),
            scratch_shapes=[pltpu.VMEM((MOE_BLOCK, ROUTED_WIDTH), U32), pltpu.VMEM((2, tm, ROUTED_WIDTH), U32),
                            pltpu.SemaphoreType.DMA((2,)), pltpu.SemaphoreType.DMA],
        ),
        out_shape=jax.ShapeDtypeStruct((n_blocks * MOE_BLOCK, ROUTED_WIDTH), U32),
        compiler_params=_params(1),
        name="dispatch",
    )(zflag, dest.reshape(n_tiles, 1, tm), t)


def _ffn_kernel(ea_ref, eb_ref, nu_ref, x_ref, wg_hbm, wu_hbm, wd_hbm, y_ref,
                wg_s, wu_s, wd_s, stage_g, stage_u, stage_d, sem, *, layer):
    i = pl.program_id(0)
    used = i < nu_ref[0]
    group = lax.shift_right_logical(ea_ref[i], 3)
    prev_group = lax.shift_right_logical(ea_ref[jnp.maximum(i - 1, 0)], 3)
    mats = ((wg_hbm, stage_g, wg_s), (wu_hbm, stage_u, wu_s), (wd_hbm, stage_d, wd_s))

    @pl.when(used & ((i == 0) | (group != prev_group)))
    def _():
        def fetch(m, e):
            w_hbm, stage, _ = mats[m]
            return pltpu.make_async_copy(w_hbm.at[layer, group * EXPERTS_PER_GROUP + e], stage, sem.at[m])

        for m in range(3):
            fetch(m, 0).start()

        def load_expert(e, carry):
            for m in range(3):
                _, stage, w_s = mats[m]
                fetch(m, e).wait()
                w_s[e] = stage[...].astype(BF16)

                @pl.when(e + 1 < EXPERTS_PER_GROUP)
                def _():
                    fetch(m, e + 1).start()
            return carry

        lax.fori_loop(0, EXPERTS_PER_GROUP, load_expert, 0)

    def swiglu(x, e):
        g = jnp.dot(x, wg_s[e], preferred_element_type=F32)
        u = jnp.dot(x, wu_s[e], preferred_element_type=F32)
        hid = (g * _sigmoid(g) * u).astype(BF16)
        return jnp.dot(hid, wd_s[e], preferred_element_type=F32)

    @pl.when(used)
    def _():
        words = x_ref[:, 0:PACKED]
        as_f32 = lambda u: lax.bitcast_convert_type(u, F32)
        x = jnp.concatenate([as_f32(lax.shift_left(words, U32(16))).astype(BF16),
                             as_f32(words & U32(0xFFFF0000)).astype(BF16)], axis=1)
        y_ref[...] = (as_f32(x_ref[:, INFO_W_LO:INFO_W_LO + 1]) * swiglu(x, ea_ref[i] & (EXPERTS_PER_GROUP - 1))
                      + as_f32(x_ref[:, INFO_W_HI:INFO_W_HI + 1]) * swiglu(x, eb_ref[i] & (EXPERTS_PER_GROUP - 1)))

    @pl.when(jnp.logical_not(used))
    def _():
        y_ref[...] = jnp.zeros_like(y_ref)


def _expert_ffn(xs, block_ea, block_eb, n_used, w_gate, w_up, w_down, layer):
    n_rows = xs.shape[0]
    nb = n_rows // MOE_BLOCK
    blk = lambda i, ea, eb, nu: (jnp.minimum(i, nu[0] - 1), 0)
    hbm = pl.BlockSpec(memory_space=pl.ANY)
    return pl.pallas_call(
        functools.partial(_ffn_kernel, layer=layer),
        grid_spec=pltpu.PrefetchScalarGridSpec(
            num_scalar_prefetch=3,
            grid=(nb,),
            in_specs=[pl.BlockSpec((MOE_BLOCK, ROUTED_WIDTH), blk), hbm, hbm, hbm],
            out_specs=pl.BlockSpec((MOE_BLOCK, D_MODEL), lambda i, ea, eb, nu: (i, 0)),
            scratch_shapes=[pltpu.VMEM((EXPERTS_PER_GROUP, D_MODEL, D_EXPERT), BF16),
                            pltpu.VMEM((EXPERTS_PER_GROUP, D_MODEL, D_EXPERT), BF16),
                            pltpu.VMEM((EXPERTS_PER_GROUP, D_EXPERT, D_MODEL), BF16),
                            pltpu.VMEM((D_MODEL, D_EXPERT), F32), pltpu.VMEM((D_MODEL, D_EXPERT), F32),
                            pltpu.VMEM((D_EXPERT, D_MODEL), F32), pltpu.SemaphoreType.DMA((3,))],
        ),
        out_shape=jax.ShapeDtypeStruct((n_rows, D_MODEL), F32),
        compiler_params=_params(1),
        name="expert_ffn",
    )(block_ea, block_eb, n_used, xs, w_gate, w_up, w_down)


def _combine_kernel(dcur_ref, dnxt_ref, h1_ref, p_ref, gple_ref, wple_ref, wpg_ref, gfin_ref, ys_ref,
                    o_ref, buf, sem, *, final, n_tiles):
    i = pl.program_id(0)
    tm = h1_ref.shape[0]
    slot = i % 2

    def issue_tile(d_ref, s):
        def issue(it, carry):
            for u in range(ISSUE_UNROLL):
                j = it * ISSUE_UNROLL + u
                pltpu.make_async_copy(ys_ref.at[pl.ds(d_ref[0, 0, j], 1)], buf.at[s, pl.ds(j, 1)], sem.at[s]).start()
            return carry

        lax.fori_loop(0, tm // ISSUE_UNROLL, issue, 0)

    @pl.when(i == 0)
    def _():
        issue_tile(dcur_ref, 0)

    @pl.when(i + 1 < n_tiles)
    def _():
        issue_tile(dnxt_ref, 1 - slot)

    pltpu.make_async_copy(ys_ref.at[pl.ds(0, tm)], buf.at[slot], sem.at[slot]).wait()

    h2 = h1_ref[...] + buf[slot]
    a = (_rms_scale(h2) * gple_ref[...]).astype(BF16)
    gate = _sigmoid(jnp.dot(a, wpg_ref[...], preferred_element_type=F32))
    emb = jnp.dot(p_ref[0].astype(BF16), wple_ref[...], preferred_element_type=F32)
    h3 = h2 + emb * gate
    if final:
        h3 = _rms_scale(h3) * gfin_ref[...]
    o_ref[...] = h3


def _combine(h1, ys, dest, p_all, layer, g_ple, w_ple, w_pg, g_final, final):
    tt = h1.shape[0]
    tm = ROW_TILE
    n_tiles = tt // tm
    dest3 = dest.reshape(n_tiles, 1, tm)
    full = lambda shape: pl.BlockSpec(shape, lambda i: (0,) * len(shape))
    row_blk = lambda w: pl.BlockSpec((tm, w), lambda i: (i, 0))
    return pl.pallas_call(
        functools.partial(_combine_kernel, final=final, n_tiles=n_tiles),
        grid=(n_tiles,),
        in_specs=[pl.BlockSpec((1, 1, tm), lambda i: (i, 0, 0), memory_space=pltpu.SMEM),
                  pl.BlockSpec((1, 1, tm), lambda i: (jnp.minimum(i + 1, n_tiles - 1), 0, 0), memory_space=pltpu.SMEM),
                  row_blk(D_MODEL), pl.BlockSpec((1, tm, D_PLE), lambda i: (layer, i, 0)), full((1, D_MODEL)),
                  full((D_PLE, D_MODEL)), full((D_MODEL, D_MODEL)), full((1, D_MODEL)),
                  pl.BlockSpec(memory_space=pl.ANY)],
        out_specs=row_blk(D_MODEL),
        out_shape=jax.ShapeDtypeStruct((tt, D_MODEL), F32),
        scratch_shapes=[pltpu.VMEM((2, tm, D_MODEL), F32), pltpu.SemaphoreType.DMA((2,))],
        compiler_params=_params(1),
        name="combine",
    )(dest3, dest3, h1, p_all, g_ple.reshape(1, D_MODEL), w_ple.astype(BF16), w_pg.astype(BF16),
      g_final.reshape(1, D_MODEL), ys)


def _class_experts():
    lo, hi = [], []
    for g in range(N_GROUPS):
        for a in range(EXPERTS_PER_GROUP):
            for b in range(a + 1, EXPERTS_PER_GROUP):
                lo.append(g * EXPERTS_PER_GROUP + a)
                hi.append(g * EXPERTS_PER_GROUP + b)
    return np.asarray(lo, np.int32), np.asarray(hi, np.int32)


def _routing_tables(routed, counts_f, n_blocks):
    info = lax.bitcast_convert_type(routed[:, INFO_CLASS:INFO_RANK + 1], F32)
    cls = info[:, 0].astype(jnp.int32)
    rank = info[:, 1].astype(jnp.int32)
    counts = counts_f[:N_CLASSES, 0].astype(jnp.int32)
    padded = ((counts + MOE_BLOCK - 1) // MOE_BLOCK) * MOE_BLOCK
    pad_end = jnp.cumsum(padded)
    pad_start = pad_end - padded
    ids = jnp.arange(N_CLASSES, dtype=jnp.int32)
    dest = jnp.sum(jnp.where(cls[:, None] == ids, pad_start, 0), axis=-1) + rank
    n_used = (pad_end[-1] // MOE_BLOCK).astype(jnp.int32).reshape(1)
    starts = jnp.arange(n_blocks, dtype=jnp.int32) * MOE_BLOCK
    block_cls = jnp.minimum(jnp.sum((pad_end[None, :] <= starts[:, None]).astype(jnp.int32), axis=1), N_CLASSES - 1)
    pick = lambda table: jnp.sum(jnp.where(block_cls[:, None] == ids, table, 0), axis=-1)
    cls_lo, cls_hi = _class_experts()
    filled = jnp.clip(pick(counts) - (starts - pick(pad_start)), 0, MOE_BLOCK)
    zflag = (filled < MOE_BLOCK).astype(jnp.int32)
    return dest, pick(jnp.asarray(cls_lo)), pick(jnp.asarray(cls_hi)), n_used, zflag


def kernel(x, p, w_in, b_gate, conv_w, conv_b, rpb, g_na, g_ml, w_out, g_mix, g_moe, w_route_group, b_route_group, w_route_expert, b_route_expert, w_exp_gate, w_exp_up, w_exp_down, g_ple, w_ple, w_ple_gate, g_final):
    batch, seq, _ = x.shape
    depth = w_in.shape[0]
    tt = batch * seq
    assert seq % ROW_TILE == 0 and seq % GRID_W == 0
    n_blocks = tt // MOE_BLOCK + N_CLASSES
    h = x.reshape(tt, D_MODEL)
    for i in range(depth):
        qna, kna, vna, qml, kt_ml, vml, oml, g_col, g_row = _in_proj(h, seq, g_mix[i], w_in[i], b_gate[i], conv_w[i], conv_b[i])
        yna = _na_attention(qna, kna, vna, _na_bias_table(rpb[i], seq // GRID_W), g_na[i], batch, seq)
        hml = _mlstm(qml, kt_ml, vml, g_row, g_col, batch, seq)
        h1, routed, counts = _mix_out(h, yna, hml, oml, g_ml[i], w_out[i], g_moe[i],
                                      w_route_group[i], b_route_group[i], w_route_expert[i], b_route_expert[i])
        dest, block_ea, block_eb, n_used, zflag = _routing_tables(routed, counts, n_blocks)
        xs = _dispatch(routed, dest, zflag, n_blocks)
        ys = _expert_ffn(xs, block_ea, block_eb, n_used, w_exp_gate, w_exp_up, w_exp_down, i)
        h = _combine(h1, ys, dest, p.reshape(depth, tt, D_PLE), i, g_ple[i], w_ple[i], w_ple_gate[i], g_final,
                     final=(i == depth - 1))
    return h.reshape(batch, seq, D_MODEL)
```

```python
import functools

import numpy as np
import jax
import jax.numpy as jnp
from jax import lax
from jax.experimental import pallas as pl
from jax.experimental.pallas import tpu as pltpu

F32 = jnp.float32
BF16 = jnp.bfloat16

D_MODEL = 1024
GRID_W = 64
D_NA = 512
NA_HEADS = 8
NA_HEAD_DIM = 64
NA_WIN_H = 8
NA_WIN_W = 16
D_ML = 512
ML_HEADS = 4
ML_HEAD_DIM = 128
ML_CONV = 5
ML_CHUNK = 128
N_GROUPS = 4
EXPERTS_PER_GROUP = 8
N_EXPERTS = 32
D_EXPERT = 512
D_PLE = 256
EPS = 1e-6
NEG_INF = -1e30
LOG2E = 1.4426950408889634

LANES = 128
SUBLANES = 8
ROW_TILE = 1024
IN_PROJ_TILE = 1024
MOE_BLOCK = 256
PAIRS_PER_GROUP = EXPERTS_PER_GROUP * (EXPERTS_PER_GROUP - 1) // 2
N_CLASSES = N_GROUPS * PAIRS_PER_GROUP
PACKED = D_MODEL // 2
ROUTED_WIDTH = PACKED + LANES
U32 = jnp.uint32
INFO_CLASS = PACKED
INFO_RANK = PACKED + 1
INFO_W_LO = PACKED + 2
INFO_W_HI = PACKED + 3
HALO = SUBLANES
ISSUE_UNROLL = 8
EXPERT_LANE0 = 8
GATE_LANES_PER_DIR = 3 * ML_HEADS
VMEM_LIMIT = 56 * 1024 * 1024


def _params(n_axes, flags=None):
    return pltpu.CompilerParams(dimension_semantics=("arbitrary",) * n_axes, vmem_limit_bytes=VMEM_LIMIT, flags=flags)


def _sigmoid(x):
    return 1.0 / (1.0 + jnp.exp(-x))


def _log_sigmoid(x):
    return jnp.minimum(x, 0.0) - jnp.log1p(jnp.exp(-jnp.abs(x)))


def _rms_scale(x):
    return x * lax.rsqrt(jnp.mean(x * x, axis=-1, keepdims=True) + EPS)


def _scan_chunk(x, axis, reverse, op, identity):
    n = x.shape[axis]
    idx = lax.broadcasted_iota(jnp.int32, x.shape, axis)
    sh = 1
    while sh < n:
        if reverse:
            x = op(x, jnp.where(idx < n - sh, pltpu.roll(x, n - sh, axis), identity))
        else:
            x = op(x, jnp.where(idx >= sh, pltpu.roll(x, sh, axis), identity))
        sh *= 2
    return x


def _in_proj_kernel(hp_ref, h_ref, hn_ref, g_ref, wna_ref, wqk_ref, wvo_ref, wgr_ref, bgr_ref, cw_ref, cb_ref,
                    qna_ref, kna_ref, vna_ref, qml_ref, kt_ref, vml_ref, oml_ref, gc_ref, gr_ref,
                    zbuf, *, tiles_per_seq):
    i = pl.program_id(0)
    tm = h_ref.shape[0]
    g = g_ref[...]

    def norm(x):
        return (_rms_scale(x) * g).astype(BF16)

    a = norm(h_ref[...])

    pos = i % tiles_per_seq
    zp = jnp.dot(norm(hp_ref[...]), wqk_ref[...], preferred_element_type=F32)
    zn = jnp.dot(norm(hn_ref[...]), wqk_ref[...], preferred_element_type=F32)
    zbuf[0:HALO, :] = jnp.where(pos == 0, 0.0, zp)
    zbuf[HALO:HALO + tm, :] = jnp.dot(a, wqk_ref[...], preferred_element_type=F32)
    zbuf[HALO + tm:2 * HALO + tm, :] = jnp.where(pos == tiles_per_seq - 1, 0.0, zn)

    ng = 2 * GATE_LANES_PER_DIR
    nt = (((1,), (1,)), ((), ()))
    gi = lax.dot_general(wgr_ref[0:ng, :], a, nt, preferred_element_type=F32) + bgr_ref[0:ng, :]
    gf = lax.dot_general(wgr_ref[ng:2 * ng, :], a, nt, preferred_element_type=F32) + bgr_ref[ng:2 * ng, :]
    row = lax.broadcasted_iota(jnp.int32, (ng, ML_CHUNK), 0)
    is_fw = row < GATE_LANES_PER_DIR
    qsel = jnp.where(is_fw, row, row - GATE_LANES_PER_DIR)
    pad_rows = jnp.zeros((ML_CHUNK - ng, ML_CHUNK), F32)

    def gates(c):
        cols_c = slice(c * ML_CHUNK, (c + 1) * ML_CHUNK)
        lf = _log_sigmoid(gf[:, cols_c])
        b = jnp.where(is_fw, _scan_chunk(lf, 1, False, jnp.add, 0.0), _scan_chunk(lf, 1, True, jnp.add, 0.0))
        r = gi[:, cols_c] - b
        cmax = jnp.where(is_fw, _scan_chunk(r, 1, False, jnp.maximum, -jnp.inf),
                         _scan_chunk(r, 1, True, jnp.maximum, -jnp.inf))
        packed = jnp.where(qsel < ML_HEADS, r, jnp.where(qsel < 2 * ML_HEADS, cmax, b))
        gc_ref[cols_c, :] = jnp.concatenate([packed, pad_rows], axis=0).T
        gr_ref[:, cols_c] = jnp.concatenate([packed[0:ML_HEADS], packed[GATE_LANES_PER_DIR:GATE_LANES_PER_DIR + ML_HEADS]], axis=0)

    def conv(c0, cstep=256):
        acc = cb_ref[:, c0:c0 + cstep]
        for j in range(ML_CONV):
            r0 = HALO - ML_CONV // 2 + j
            acc = acc + zbuf[r0:r0 + tm, c0:c0 + cstep] * cw_ref[j:j + 1, c0:c0 + cstep]
        y = acc * _sigmoid(acc)
        if c0 < D_ML:
            qml_ref[:, c0:c0 + cstep] = (y * (ML_HEAD_DIM ** -0.5)).astype(BF16)
        else:
            kt_ref[c0 - D_ML:c0 - D_ML + cstep, :] = y.T.astype(BF16)

    def proj(w_ref, c0, scale=None):
        z = jnp.dot(a, w_ref[:, c0:c0 + D_NA], preferred_element_type=F32)
        return (z if scale is None else z * scale).astype(BF16)

    conv(0)
    qna_ref[...] = proj(wna_ref, 0, NA_HEAD_DIM ** -0.5 * LOG2E)
    conv(256)
    kna_ref[...] = proj(wna_ref, D_NA)
    conv(512)
    vna_ref[...] = proj(wna_ref, 2 * D_NA)
    conv(768)
    vml_ref[...] = proj(wvo_ref, 0)
    for c in range(tm // ML_CHUNK):
        gates(c)
    oml_ref[...] = proj(wvo_ref, D_ML)


def _in_proj(h, seq, g_mix, w_in, b_gate, conv_w, conv_b):
    t = h.shape[0]
    tm = IN_PROJ_TILE
    n_tiles = t // tm
    tiles_per_seq = seq // tm
    nh = 4 * ML_HEADS
    o_qk = 3 * D_NA
    o_v = o_qk + 2 * D_ML
    o_g = o_v + 2 * D_ML
    wna = w_in[:, :o_qk].astype(BF16)
    wqk = w_in[:, o_qk:o_v].astype(BF16)
    wvo = w_in[:, o_v:o_g].astype(BF16)
    wg = w_in[:, o_g:o_g + nh]
    ng = 2 * GATE_LANES_PER_DIR
    gate_id = np.arange(ng)
    src_i = (gate_id // GATE_LANES_PER_DIR) * 2 * ML_HEADS + gate_id % ML_HEADS
    wgr = jnp.concatenate([wg[:, src_i], wg[:, src_i + ML_HEADS]], axis=1).T.astype(BF16)
    bgr = jnp.concatenate([b_gate[src_i], b_gate[src_i + ML_HEADS]]).reshape(2 * ng, 1)
    cw = jnp.pad(conv_w, ((0, SUBLANES - ML_CONV), (0, 0)))
    cb = conv_b.reshape(1, 2 * D_ML)
    full = lambda shape: pl.BlockSpec(shape, lambda i: (0,) * len(shape))
    row_blk = lambda w: pl.BlockSpec((tm, w), lambda i: (i, 0))
    hb = tm // HALO
    out_bf = jax.ShapeDtypeStruct((t, D_NA), BF16)
    return pl.pallas_call(
        functools.partial(_in_proj_kernel, tiles_per_seq=tiles_per_seq),
        grid=(n_tiles,),
        in_specs=[
            pl.BlockSpec((HALO, D_MODEL), lambda i: (jnp.maximum(i * hb - 1, 0), 0)),
            row_blk(D_MODEL),
            pl.BlockSpec((HALO, D_MODEL), lambda i: (jnp.minimum((i + 1) * hb, t // HALO - 1), 0)),
            full((1, D_MODEL)), full((D_MODEL, o_qk)), full((D_MODEL, 2 * D_ML)), full((D_MODEL, 2 * D_ML)),
            full((2 * ng, D_MODEL)), full((2 * ng, 1)),
            full((SUBLANES, 2 * D_ML)), full((1, 2 * D_ML)),
        ],
        out_specs=[row_blk(D_NA)] * 4 + [pl.BlockSpec((D_ML, tm), lambda i: (0, i))] + [row_blk(D_NA)] * 2
        + [row_blk(LANES), pl.BlockSpec((2 * ML_HEADS, tm), lambda i: (0, i))],
        out_shape=[out_bf] * 4 + [jax.ShapeDtypeStruct((D_ML, t), BF16)] + [out_bf] * 2
        + [jax.ShapeDtypeStruct((t, LANES), F32), jax.ShapeDtypeStruct((2 * ML_HEADS, t), F32)],
        scratch_shapes=[pltpu.VMEM((tm + 2 * HALO, 2 * D_ML), F32)],
        compiler_params=_params(1),
        name="in_proj",
    )(h, h, h, g_mix.reshape(1, D_MODEL), wna, wqk, wvo, wgr, bgr, cw, cb)


def _na_bias_table(rpb, rows):
    kh = min(NA_WIN_H, rows)
    o = np.arange(kh)[:, None]
    kr = np.arange(kh)[None, :]
    dr = kr - o + (NA_WIN_H - 1)
    qc = np.arange(GRID_W)[:, None]
    kc = np.arange(GRID_W)[None, :]
    cs = np.clip(qc - NA_WIN_W // 2, 0, GRID_W - NA_WIN_W)
    inw = (kc >= cs) & (kc < cs + NA_WIN_W)
    dc = np.clip(kc - qc, 1 - NA_WIN_W, NA_WIN_W - 1) + (NA_WIN_W - 1)
    sel_r = jnp.asarray(dr[:, :, None] == np.arange(2 * NA_WIN_H - 1), F32)
    sel_c = jnp.asarray(dc[:, :, None] == np.arange(2 * NA_WIN_W - 1), F32)
    rows_sel = jnp.einsum("hrc,okr->hokc", rpb.astype(F32), sel_r, precision=lax.Precision.HIGHEST)
    b = jnp.einsum("hokc,qlc->hoqkl", rows_sel, sel_c, precision=lax.Precision.HIGHEST)
    b = jnp.where(inw[None, None, :, None, :], b * LOG2E, NEG_INF)
    return b.reshape(rpb.shape[0], kh, GRID_W, kh * GRID_W)


NA_HEADS_PER_BLOCK = 4
NA_BAND_ROWS = 64
NA_UNROLL = 16


def _na_kernel(q_ref, k_ref, v_ref, bias_ref, g_ref, o_ref, *, rows, kh, band):
    nkeys = kh * GRID_W
    hpb = NA_HEADS_PER_BLOCK
    m_rows = hpb * GRID_W
    width = hpb * NA_HEAD_DIM
    band_i = pl.program_id(2)
    lane_head = lax.shift_right_logical(lax.broadcasted_iota(jnp.int32, (m_rows, width), 1), 6)
    row_head = lax.shift_right_logical(lax.broadcasted_iota(jnp.int32, (m_rows, width), 0), 6)
    own = lane_head == row_head
    g = g_ref[...]

    def one_row(rl):
        r = band_i * band + rl
        rs = jnp.clip(r - kh // 2, 0, rows - kh)
        off = r - rs
        q = q_ref[0, pl.ds(pl.multiple_of(rl * GRID_W, GRID_W), GRID_W), :]
        kb = k_ref[0, pl.ds(pl.multiple_of(rs * GRID_W, GRID_W), nkeys), :]
        vb = v_ref[0, pl.ds(pl.multiple_of(rs * GRID_W, GRID_W), nkeys), :]
        qs = jnp.where(own, jnp.concatenate([q] * hpb, axis=0), jnp.zeros((m_rows, width), BF16))
        s = lax.dot_general(qs, kb, (((1,), (1,)), ((), ())), preferred_element_type=F32)
        s = s + bias_ref[:, off].reshape(m_rows, nkeys)
        m = jnp.max(s, axis=-1, keepdims=True)
        p = jnp.exp2(s - m)
        l = jnp.sum(p, axis=-1, keepdims=True)
        pv = jnp.dot(p.astype(BF16), vb, preferred_element_type=F32)
        oh = jnp.where(own, pv * (1.0 / l), 0.0)
        ms = jnp.sum(oh * oh, axis=-1, keepdims=True) * (1.0 / NA_HEAD_DIM)
        yn = oh * lax.rsqrt(ms + EPS)
        y = yn[0:GRID_W]
        for hh in range(1, hpb):
            y = y + yn[hh * GRID_W:(hh + 1) * GRID_W]
        o_ref[0, pl.ds(pl.multiple_of(rl * GRID_W, GRID_W), GRID_W), :] = (y * g).astype(BF16)

    def body(it, carry):
        for u in range(NA_UNROLL):
            one_row(it * NA_UNROLL + u)
        return carry

    lax.fori_loop(0, band // NA_UNROLL, body, 0)


def _na_attention(q, k, v, bias, g_na, batch, seq):
    rows = seq // GRID_W
    kh = min(NA_WIN_H, rows)
    hpb = NA_HEADS_PER_BLOCK
    width = hpb * NA_HEAD_DIM
    band = min(NA_BAND_ROWS, rows)
    assert rows % band == 0 and band % NA_UNROLL == 0
    q3, k3, v3 = (x.reshape(batch, seq, D_NA) for x in (q, k, v))
    seq_blk = pl.BlockSpec((1, seq, width), lambda b, j, r: (b, 0, j))
    band_blk = pl.BlockSpec((1, band * GRID_W, width), lambda b, j, r: (b, r, j))
    out = pl.pallas_call(
        functools.partial(_na_kernel, rows=rows, kh=kh, band=band),
        grid=(batch, D_NA // width, rows // band),
        in_specs=[band_blk, seq_blk, seq_blk,
                  pl.BlockSpec((hpb, kh, GRID_W, kh * GRID_W), lambda b, j, r: (j, 0, 0, 0)),
                  pl.BlockSpec((1, width), lambda b, j, r: (0, j))],
        out_specs=band_blk,
        out_shape=jax.ShapeDtypeStruct((batch, seq, D_NA), BF16),
        compiler_params=_params(3),
        name="na_attn",
    )(q3, k3, v3, bias, g_na.reshape(1, D_NA))
    return out.reshape(batch * seq, D_NA)


ML_UNROLL = 8
ML_HEADS_PER_STEP = 1


def _mlstm_kernel(q_ref, kt_ref, v_ref, gr_ref, gc_ref, o_ref, st_ref, *, n_chunks):
    L = ML_CHUNK
    hps = ML_HEADS_PER_STEP
    lane = lax.broadcasted_iota(jnp.int32, (L, LANES), 1)
    grow_i = lax.broadcasted_iota(jnp.int32, (2 * ML_HEADS, L), 0)
    row_i = lax.broadcasted_iota(jnp.int32, (L, L), 0)
    col_i = lax.broadcasted_iota(jnp.int32, (L, L), 1)
    ones = jnp.ones((L, ML_HEAD_DIM), BF16)

    def step(c, m_prev, hb, reverse):
        off = pl.multiple_of(c * L, L)
        d = 1 if reverse else 0
        hd = pl.program_id(1) * hps + hb
        cols = slice(hb * ML_HEAD_DIM, (hb + 1) * ML_HEAD_DIM)
        qc = q_ref[0, pl.ds(off, L), cols]
        kt = kt_ref[cols, pl.ds(off, L)]
        vc = v_ref[0, pl.ds(off, L), cols]
        gcol = gc_ref[pl.ds(off, L), :]
        grow = gr_ref[:, pl.ds(off, L)]

        def col(qn):
            sel = lane == d * GATE_LANES_PER_DIR + qn * ML_HEADS + hd
            return jnp.sum(jnp.where(sel, gcol, 0.0), axis=1, keepdims=True)

        r_col, cmax_col, b_col = col(0), col(1), col(2)
        r_row = jnp.sum(jnp.where(grow_i == d * ML_HEADS + hd, grow, 0.0), axis=0, keepdims=True)
        end = 0 if reverse else L - 1
        mm_end = jnp.maximum(m_prev, cmax_col[end:end + 1, :])
        m_new = b_col[end:end + 1, :] + mm_end
        mm_col = jnp.maximum(m_prev, cmax_col)
        causal = (col_i >= row_i) if reverse else (col_i <= row_i)
        dmat = jnp.exp(jnp.where(causal, r_row - mm_col, -jnp.inf))
        s = jnp.dot(qc, kt, preferred_element_type=F32) * dmat
        w_inter = jnp.exp(m_prev - mm_col)
        floor = jnp.exp(-(b_col + mm_col))
        w_col = jnp.exp(r_col - mm_end)
        decay = jnp.exp(m_prev - mm_end)
        vext = jnp.concatenate([vc, ones], axis=1)
        state = st_ref[d, hb]
        nd = (jnp.dot(s.astype(BF16), vext, preferred_element_type=F32)
              + w_inter * jnp.dot(qc, state.astype(BF16), preferred_element_type=F32))
        h = nd[:, :ML_HEAD_DIM] / jnp.maximum(jnp.abs(nd[:, ML_HEAD_DIM:]), floor)
        wv = jnp.concatenate([(w_col * vc.astype(F32)).astype(BF16),
                              jnp.broadcast_to(w_col, (L, ML_HEAD_DIM)).astype(BF16)], axis=1)
        st_ref[d, hb] = decay * state + jnp.dot(kt, wv, preferred_element_type=F32)
        return h, m_new

    st_ref[...] = jnp.zeros_like(st_ref)

    def sweep(accumulate):
        def body(it, carry):
            m = list(carry)
            for u in range(ML_UNROLL):
                j = it * ML_UNROLL + u
                jb = n_chunks - 1 - j
                rows_f = pl.ds(pl.multiple_of(j * L, L), L)
                rows_b = pl.ds(pl.multiple_of(jb * L, L), L)
                for hb in range(hps):
                    cols = slice(hb * ML_HEAD_DIM, (hb + 1) * ML_HEAD_DIM)
                    h_f, m[hb] = step(j, m[hb], hb, False)
                    h_b, m[hps + hb] = step(jb, m[hps + hb], hb, True)
                    if accumulate:
                        o_ref[0, rows_f, cols] += h_f
                        o_ref[0, rows_b, cols] += h_b
                    else:
                        o_ref[0, rows_f, cols] = h_f
                        o_ref[0, rows_b, cols] = h_b
            return tuple(m)
        return body

    half = n_chunks // (2 * ML_UNROLL)
    zero = jnp.zeros((1, 1), F32)
    carry = lax.fori_loop(0, half, sweep(False), (zero,) * (2 * hps))
    lax.fori_loop(half, 2 * half, sweep(True), carry)


def _mlstm(q, kt, v, g_row, g_col, batch, seq):
    n_chunks = seq // ML_CHUNK
    assert n_chunks % (2 * ML_UNROLL) == 0
    hps = ML_HEADS_PER_STEP
    width = hps * ML_HEAD_DIM
    q3, v3 = (x.reshape(batch, seq, D_ML) for x in (q, v))
    seq_blk = pl.BlockSpec((1, seq, width), lambda b, j: (b, 0, j))
    once_per_batch = {}
    out = pl.pallas_call(
        functools.partial(_mlstm_kernel, n_chunks=n_chunks),
        grid=(batch, ML_HEADS // hps),
        in_specs=[seq_blk, pl.BlockSpec((width, seq), lambda b, j: (j, b)), seq_blk,
                  pl.BlockSpec((2 * ML_HEADS, seq), lambda b, j: (0, b), **once_per_batch),
                  pl.BlockSpec((seq, LANES), lambda b, j: (b, 0), **once_per_batch)],
        out_specs=seq_blk,
        out_shape=jax.ShapeDtypeStruct((batch, seq, D_ML), F32),
        scratch_shapes=[pltpu.VMEM((2, hps, ML_HEAD_DIM, 2 * ML_HEAD_DIM), F32)],
        compiler_params=_params(2),
        name="mlstm",
    )(q3, kt, v3, g_row, g_col)
    return out.reshape(batch * seq, D_ML)


def _mix_out_kernel(h_ref, yna_ref, hml_ref, oml_ref, gml_ref, wo_ref, gmoe_ref, wrh_ref, wrl_ref, br_ref, tri_ref,
                    h1_ref, t_ref, cnt_ref):
    i = pl.program_id(0)
    tm = h_ref.shape[0]

    @pl.when(i == 0)
    def _():
        cnt_ref[...] = jnp.zeros_like(cnt_ref)

    hml = hml_ref[...]
    parts = [_rms_scale(hml[:, d * ML_HEAD_DIM:(d + 1) * ML_HEAD_DIM]) for d in range(ML_HEADS)]
    yml = _sigmoid(oml_ref[...].astype(F32)) * (jnp.concatenate(parts, axis=1) * gml_ref[...])
    mix = (jnp.dot(yna_ref[...], wo_ref[0:D_NA, :], preferred_element_type=F32)
           + jnp.dot(yml.astype(BF16), wo_ref[D_NA:D_NA + D_ML, :], preferred_element_type=F32))
    h1 = h_ref[...] + mix
    h1_ref[...] = h1
    t = _rms_scale(h1) * gmoe_ref[...]
    bits = lax.bitcast_convert_type(t.astype(BF16).astype(F32), U32)
    t_ref[:, 0:PACKED] = bits[:, PACKED:] | lax.shift_right_logical(bits[:, :PACKED], U32(16))

    t_hi = t.astype(BF16)
    t_lo = (t - t_hi.astype(F32)).astype(BF16)
    nt = (((1,), (1,)), ((), ()))
    logits = (lax.dot_general(wrh_ref[...], t_hi, nt, preferred_element_type=F32)
              + lax.dot_general(wrh_ref[...], t_lo, nt, preferred_element_type=F32)
              + lax.dot_general(wrl_ref[...], t_hi, nt, preferred_element_type=F32)) + br_ref[...]
    epg = EXPERTS_PER_GROUP
    row8 = lax.broadcasted_iota(jnp.int32, (epg, tm), 0)

    def top(x):
        mx = jnp.max(x, axis=0, keepdims=True)
        idx = jnp.min(jnp.where(x == mx, row8, epg), axis=0, keepdims=True)
        return mx, idx

    lg = jnp.where(row8 < N_GROUPS, logits[0:epg], -jnp.inf)
    g_max, g_idx = top(lg)
    pg_top = 1.0 / jnp.sum(jnp.exp(lg - g_max), axis=0, keepdims=True)
    le = jnp.zeros((epg, tm), F32)
    for grp in range(N_GROUPS):
        le = jnp.where(g_idx == grp, logits[EXPERT_LANE0 + grp * epg:EXPERT_LANE0 + (grp + 1) * epg], le)
    e1_max, i1 = top(le)
    e_sum = jnp.sum(jnp.exp(le - e1_max), axis=0, keepdims=True)
    e2_max, i2 = top(jnp.where(row8 == i1, -jnp.inf, le))
    p1 = 1.0 / e_sum
    p2 = jnp.exp(e2_max - e1_max) / e_sum
    w1 = pg_top * p1 / (p1 + p2)
    w2 = pg_top * p2 / (p1 + p2)

    la = jnp.minimum(i1, i2)
    lb = jnp.maximum(i1, i2)
    cls = g_idx * PAIRS_PER_GROUP + lax.shift_right_logical(la * (2 * epg - 1 - la), 1) + (lb - la - 1)
    w_lo = jnp.where(i1 < i2, w1, w2)
    w_hi = jnp.where(i1 < i2, w2, w1)
    oh = (lax.broadcasted_iota(jnp.int32, (LANES, tm), 0) == cls).astype(F32)
    before = jnp.dot(oh.astype(BF16), tri_ref[...], preferred_element_type=F32) + cnt_ref[:, 0:1]
    rank = jnp.sum(oh * before, axis=0, keepdims=True)
    cnt_ref[...] = cnt_ref[...] + jnp.sum(oh, axis=1, keepdims=True)
    info = jnp.concatenate([cls.astype(F32), rank, w_lo, w_hi, jnp.zeros((LANES - 4, tm), F32)], axis=0)
    for c in range(tm // LANES):
        t_ref[c * LANES:(c + 1) * LANES, PACKED:PACKED + LANES] = lax.bitcast_convert_type(
            info[:, c * LANES:(c + 1) * LANES].T, U32)


def _mix_out(h, yna, hml, oml, g_ml, w_out, g_moe, w_rg, b_rg, w_re, b_re):
    t = h.shape[0]
    tm = ROW_TILE
    wr = jnp.zeros((LANES, D_MODEL), F32)
    wr = wr.at[:N_GROUPS].set(w_rg.T).at[EXPERT_LANE0:EXPERT_LANE0 + N_EXPERTS].set(w_re.T)
    wrh = wr.astype(BF16)
    wrl = (wr - wrh.astype(F32)).astype(BF16)
    br = jnp.zeros((LANES, 1), F32)
    br = br.at[:N_GROUPS, 0].set(b_rg).at[EXPERT_LANE0:EXPERT_LANE0 + N_EXPERTS, 0].set(b_re)
    tri = jnp.asarray(np.triu(np.ones((tm, tm), np.float32), 1), BF16)
    full = lambda shape: pl.BlockSpec(shape, lambda i: (0,) * len(shape))
    row_blk = lambda w: pl.BlockSpec((tm, w), lambda i: (i, 0))
    return pl.pallas_call(
        _mix_out_kernel,
        grid=(t // tm,),
        in_specs=[row_blk(D_MODEL), row_blk(D_NA), row_blk(D_ML), row_blk(D_ML), full((1, D_ML)),
                  full((D_MODEL, D_MODEL)), full((1, D_MODEL)), full((LANES, D_MODEL)), full((LANES, D_MODEL)),
                  full((LANES, 1)), full((tm, tm))],
        out_specs=[row_blk(D_MODEL), row_blk(ROUTED_WIDTH), full((LANES, LANES))],
        out_shape=[jax.ShapeDtypeStruct((t, D_MODEL), F32), jax.ShapeDtypeStruct((t, ROUTED_WIDTH), U32),
                   jax.ShapeDtypeStruct((LANES, LANES), F32)],
        compiler_params=_params(1),
        name="mix_out",
    )(h, yna, hml, oml, g_ml.reshape(1, D_ML), w_out.astype(BF16), g_moe.reshape(1, D_MODEL), wrh, wrl, br, tri)


def _dispatch_kernel(zflag_ref, dest_ref, t_ref, xs_ref, zbuf, ring, sem, zsem, *, n_blocks, n_tiles):
    tm = t_ref.shape[0]

    @pl.when(pl.program_id(0) == 0)
    def _():
        zbuf[...] = jnp.zeros_like(zbuf)

        def zero_copy(b):
            return pltpu.make_async_copy(zbuf, xs_ref.at[pl.ds(pl.multiple_of(b * MOE_BLOCK, MOE_BLOCK), MOE_BLOCK)], zsem)

        def zissue(b, carry):
            @pl.when(zflag_ref[b] != 0)
            def _():
                zero_copy(b).start()
            return carry

        def zwait(b, carry):
            @pl.when(zflag_ref[b] != 0)
            def _():
                zero_copy(b).wait()
            return carry

        lax.fori_loop(0, n_blocks, zissue, 0)
        lax.fori_loop(0, n_blocks, zwait, 0)

    i = pl.program_id(0)
    slot = i % 2

    def wait_slot(s):
        pltpu.make_async_copy(ring.at[s], xs_ref.at[pl.ds(0, tm)], sem.at[s]).wait()

    @pl.when(i >= 2)
    def _():
        wait_slot(slot)

    ring[slot] = t_ref[...]

    def issue(it, carry):
        for u in range(ISSUE_UNROLL):
            j = it * ISSUE_UNROLL + u
            pltpu.make_async_copy(ring.at[slot, pl.ds(j, 1)], xs_ref.at[pl.ds(dest_ref[0, 0, j], 1)], sem.at[slot]).start()
        return carry

    lax.fori_loop(0, tm // ISSUE_UNROLL, issue, 0)

    @pl.when(i == n_tiles - 1)
    def _():
        wait_slot(slot)

        @pl.when(i >= 1)
        def _():
            wait_slot(1 - slot)


def _dispatch(t, dest, zflag, n_blocks):
    tt = t.shape[0]
    tm = ROW_TILE
    n_tiles = tt // tm
    return pl.pallas_call(
        functools.partial(_dispatch_kernel, n_blocks=n_blocks, n_tiles=n_tiles),
        grid_spec=pltpu.PrefetchScalarGridSpec(
            num_scalar_prefetch=1,
            grid=(n_tiles,),
            in_specs=[pl.BlockSpec((1, 1, tm), lambda i, zf: (i, 0, 0), memory_space=pltpu.SMEM),
                      pl.BlockSpec((tm, ROUTED_WIDTH), lambda i, zf: (i, 0))],
            out_specs=pl.BlockSpec(memory_space=pl.ANY),
            scratch_shapes=[pltpu.VMEM((MOE_BLOCK, ROUTED_WIDTH), U32), pltpu.VMEM((2, tm, ROUTED_WIDTH), U32),
                            pltpu.SemaphoreType.DMA((2,)), pltpu.SemaphoreType.DMA],
        ),
        out_shape=jax.ShapeDtypeStruct((n_blocks * MOE_BLOCK, ROUTED_WIDTH), U32),
        compiler_params=_params(1),
        name="dispatch",
    )(zflag, dest.reshape(n_tiles, 1, tm), t)


def _ffn_kernel(ea_ref, eb_ref, nu_ref, x_ref, wg_hbm, wu_hbm, wd_hbm, y_ref,
                wg_s, wu_s, wd_s, stage_g, stage_u, stage_d, sem, *, layer):
    i = pl.program_id(0)
    used = i < nu_ref[0]
    group = lax.shift_right_logical(ea_ref[i], 3)
    prev_group = lax.shift_right_logical(ea_ref[jnp.maximum(i - 1, 0)], 3)
    mats = ((wg_hbm, stage_g, wg_s), (wu_hbm, stage_u, wu_s), (wd_hbm, stage_d, wd_s))

    @pl.when(used & ((i == 0) | (group != prev_group)))
    def _():
        def fetch(m, e):
            w_hbm, stage, _ = mats[m]
            return pltpu.make_async_copy(w_hbm.at[layer, group * EXPERTS_PER_GROUP + e], stage, sem.at[m])

        for m in range(3):
            fetch(m, 0).start()

        def load_expert(e, carry):
            for m in range(3):
                _, stage, w_s = mats[m]
                fetch(m, e).wait()
                w_s[e] = stage[...].astype(BF16)

                @pl.when(e + 1 < EXPERTS_PER_GROUP)
                def _():
                    fetch(m, e + 1).start()
            return carry

        lax.fori_loop(0, EXPERTS_PER_GROUP, load_expert, 0)

    def swiglu(x, e):
        g = jnp.dot(x, wg_s[e], preferred_element_type=F32)
        u = jnp.dot(x, wu_s[e], preferred_element_type=F32)
        hid = (g * _sigmoid(g) * u).astype(BF16)
        return jnp.dot(hid, wd_s[e], preferred_element_type=F32)

    @pl.when(used)
    def _():
        words = x_ref[:, 0:PACKED]
        as_f32 = lambda u: lax.bitcast_convert_type(u, F32)
        x = jnp.concatenate([as_f32(lax.shift_left(words, U32(16))).astype(BF16),
                             as_f32(words & U32(0xFFFF0000)).astype(BF16)], axis=1)
        y_ref[...] = (as_f32(x_ref[:, INFO_W_LO:INFO_W_LO + 1]) * swiglu(x, ea_ref[i] & (EXPERTS_PER_GROUP - 1))
                      + as_f32(x_ref[:, INFO_W_HI:INFO_W_HI + 1]) * swiglu(x, eb_ref[i] & (EXPERTS_PER_GROUP - 1)))

    @pl.when(jnp.logical_not(used))
    def _():
        y_ref[...] = jnp.zeros_like(y_ref)


def _expert_ffn(xs, block_ea, block_eb, n_used, w_gate, w_up, w_down, layer):
    n_rows = xs.shape[0]
    nb = n_rows // MOE_BLOCK
    blk = lambda i, ea, eb, nu: (jnp.minimum(i, nu[0] - 1), 0)
    hbm = pl.BlockSpec(memory_space=pl.ANY)
    return pl.pallas_call(
        functools.partial(_ffn_kernel, layer=layer),
        grid_spec=pltpu.PrefetchScalarGridSpec(
            num_scalar_prefetch=3,
            grid=(nb,),
            in_specs=[pl.BlockSpec((MOE_BLOCK, ROUTED_WIDTH), blk), hbm, hbm, hbm],
            out_specs=pl.BlockSpec((MOE_BLOCK, D_MODEL), lambda i, ea, eb, nu: (i, 0)),
            scratch_shapes=[pltpu.VMEM((EXPERTS_PER_GROUP, D_MODEL, D_EXPERT), BF16),
                            pltpu.VMEM((EXPERTS_PER_GROUP, D_MODEL, D_EXPERT), BF16),
                            pltpu.VMEM((EXPERTS_PER_GROUP, D_EXPERT, D_MODEL), BF16),
                            pltpu.VMEM((D_MODEL, D_EXPERT), F32), pltpu.VMEM((D_MODEL, D_EXPERT), F32),
                            pltpu.VMEM((D_EXPERT, D_MODEL), F32), pltpu.SemaphoreType.DMA((3,))],
        ),
        out_shape=jax.ShapeDtypeStruct((n_rows, D_MODEL), F32),
        compiler_params=_params(1),
        name="expert_ffn",
    )(block_ea, block_eb, n_used, xs, w_gate, w_up, w_down)


def _combine_kernel(dcur_ref, dnxt_ref, h1_ref, p_ref, gple_ref, wple_ref, wpg_ref, gfin_ref, ys_ref,
                    o_ref, buf, sem, *, final, n_tiles):
    i = pl.program_id(0)
    tm = h1_ref.shape[0]
    slot = i % 2

    def issue_tile(d_ref, s):
        def issue(it, carry):
            for u in range(ISSUE_UNROLL):
                j = it * ISSUE_UNROLL + u
                pltpu.make_async_copy(ys_ref.at[pl.ds(d_ref[0, 0, j], 1)], buf.at[s, pl.ds(j, 1)], sem.at[s]).start()
            return carry

        lax.fori_loop(0, tm // ISSUE_UNROLL, issue, 0)

    @pl.when(i == 0)
    def _():
        issue_tile(dcur_ref, 0)

    @pl.when(i + 1 < n_tiles)
    def _():
        issue_tile(dnxt_ref, 1 - slot)

    pltpu.make_async_copy(ys_ref.at[pl.ds(0, tm)], buf.at[slot], sem.at[slot]).wait()

    h2 = h1_ref[...] + buf[slot]
    a = (_rms_scale(h2) * gple_ref[...]).astype(BF16)
    gate = _sigmoid(jnp.dot(a, wpg_ref[...], preferred_element_type=F32))
    emb = jnp.dot(p_ref[0].astype(BF16), wple_ref[...], preferred_element_type=F32)
    h3 = h2 + emb * gate
    if final:
        h3 = _rms_scale(h3) * gfin_ref[...]
    o_ref[...] = h3


def _combine(h1, ys, dest, p_all, layer, g_ple, w_ple, w_pg, g_final, final):
    tt = h1.shape[0]
    tm = ROW_TILE
    n_tiles = tt // tm
    dest3 = dest.reshape(n_tiles, 1, tm)
    full = lambda shape: pl.BlockSpec(shape, lambda i: (0,) * len(shape))
    row_blk = lambda w: pl.BlockSpec((tm, w), lambda i: (i, 0))
    return pl.pallas_call(
        functools.partial(_combine_kernel, final=final, n_tiles=n_tiles),
        grid=(n_tiles,),
        in_specs=[pl.BlockSpec((1, 1, tm), lambda i: (i, 0, 0), memory_space=pltpu.SMEM),
                  pl.BlockSpec((1, 1, tm), lambda i: (jnp.minimum(i + 1, n_tiles - 1), 0, 0), memory_space=pltpu.SMEM),
                  row_blk(D_MODEL), pl.BlockSpec((1, tm, D_PLE), lambda i: (layer, i, 0)), full((1, D_MODEL)),
                  full((D_PLE, D_MODEL)), full((D_MODEL, D_MODEL)), full((1, D_MODEL)),
                  pl.BlockSpec(memory_space=pl.ANY)],
        out_specs=row_blk(D_MODEL),
        out_shape=jax.ShapeDtypeStruct((tt, D_MODEL), F32),
        scratch_shapes=[pltpu.VMEM((2, tm, D_MODEL), F32), pltpu.SemaphoreType.DMA((2,))],
        compiler_params=_params(1),
        name="combine",
    )(dest3, dest3, h1, p_all, g_ple.reshape(1, D_MODEL), w_ple.astype(BF16), w_pg.astype(BF16),
      g_final.reshape(1, D_MODEL), ys)


def _class_experts():
    lo, hi = [], []
    for g in range(N_GROUPS):
        for a in range(EXPERTS_PER_GROUP):
            for b in range(a + 1, EXPERTS_PER_GROUP):
                lo.append(g * EXPERTS_PER_GROUP + a)
                hi.append(g * EXPERTS_PER_GROUP + b)
    return np.asarray(lo, np.int32), np.asarray(hi, np.int32)


def _routing_tables(routed, counts_f, n_blocks):
    info = lax.bitcast_convert_type(routed[:, INFO_CLASS:INFO_RANK + 1], F32)
    cls = info[:, 0].astype(jnp.int32)
    rank = info[:, 1].astype(jnp.int32)
    counts = counts_f[:N_CLASSES, 0].astype(jnp.int32)
    padded = ((counts + MOE_BLOCK - 1) // MOE_BLOCK) * MOE_BLOCK
    pad_end = jnp.cumsum(padded)
    pad_start = pad_end - padded
    ids = jnp.arange(N_CLASSES, dtype=jnp.int32)
    dest = jnp.sum(jnp.where(cls[:, None] == ids, pad_start, 0), axis=-1) + rank
    n_used = (pad_end[-1] // MOE_BLOCK).astype(jnp.int32).reshape(1)
    starts = jnp.arange(n_blocks, dtype=jnp.int32) * MOE_BLOCK
    block_cls = jnp.minimum(jnp.sum((pad_end[None, :] <= starts[:, None]).astype(jnp.int32), axis=1), N_CLASSES - 1)
    pick = lambda table: jnp.sum(jnp.where(block_cls[:, None] == ids, table, 0), axis=-1)
    cls_lo, cls_hi = _class_experts()
    filled = jnp.clip(pick(counts) - (starts - pick(pad_start)), 0, MOE_BLOCK)
    zflag = (filled < MOE_BLOCK).astype(jnp.int32)
    return dest, pick(jnp.asarray(cls_lo)), pick(jnp.asarray(cls_hi)), n_used, zflag


def kernel(x, p, w_in, b_gate, conv_w, conv_b, rpb, g_na, g_ml, w_out, g_mix, g_moe, w_route_group, b_route_group, w_route_expert, b_route_expert, w_exp_gate, w_exp_up, w_exp_down, g_ple, w_ple, w_ple_gate, g_final):
    batch, seq, _ = x.shape
    depth = w_in.shape[0]
    tt = batch * seq
    assert seq % ROW_TILE == 0 and seq % GRID_W == 0
    n_blocks = tt // MOE_BLOCK + N_CLASSES
    h = x.reshape(tt, D_MODEL)
    for i in range(depth):
        qna, kna, vna, qml, kt_ml, vml, oml, g_col, g_row = _in_proj(h, seq, g_mix[i], w_in[i], b_gate[i], conv_w[i], conv_b[i])
        yna = _na_attention(qna, kna, vna, _na_bias_table(rpb[i], seq // GRID_W), g_na[i], batch, seq)
        hml = _mlstm(qml, kt_ml, vml, g_row, g_col, batch, seq)
        h1, routed, counts = _mix_out(h, yna, hml, oml, g_ml[i], w_out[i], g_moe[i],
                                      w_route_group[i], b_route_group[i], w_route_expert[i], b_route_expert[i])
        dest, block_ea, block_eb, n_used, zflag = _routing_tables(routed, counts, n_blocks)
        xs = _dispatch(routed, dest, zflag, n_blocks)
        ys = _expert_ffn(xs, block_ea, block_eb, n_used, w_exp_gate, w_exp_up, w_exp_down, i)
        h = _combine(h1, ys, dest, p.reshape(depth, tt, D_PLE), i, g_ple[i], w_ple[i], w_ple_gate[i], g_final,
                     final=(i == depth - 1))
    return h.reshape(batch, seq, D_MODEL)
```

```python
import functools

import numpy as np
import jax
import jax.numpy as jnp
from jax import lax
from jax.experimental import pallas as pl
from jax.experimental.pallas import tpu as pltpu

F32 = jnp.float32
BF16 = jnp.bfloat16

D_MODEL = 1024
GRID_W = 64
D_NA = 512
NA_HEADS = 8
NA_HEAD_DIM = 64
NA_WIN_H = 8
NA_WIN_W = 16
D_ML = 512
ML_HEADS = 4
ML_HEAD_DIM = 128
ML_CONV = 5
ML_CHUNK = 128
N_GROUPS = 4
EXPERTS_PER_GROUP = 8
N_EXPERTS = 32
D_EXPERT = 512
D_PLE = 256
EPS = 1e-6
NEG_INF = -1e30
LOG2E = 1.4426950408889634

LANES = 128
SUBLANES = 8
ROW_TILE = 1024
IN_PROJ_TILE = 1024
MOE_BLOCK = 256
PAIRS_PER_GROUP = EXPERTS_PER_GROUP * (EXPERTS_PER_GROUP - 1) // 2
N_CLASSES = N_GROUPS * PAIRS_PER_GROUP
PACKED = D_MODEL // 2
ROUTED_WIDTH = PACKED + LANES
U32 = jnp.uint32
INFO_CLASS = PACKED
INFO_RANK = PACKED + 1
INFO_W_LO = PACKED + 2
INFO_W_HI = PACKED + 3
HALO = SUBLANES
ISSUE_UNROLL = 8
EXPERT_LANE0 = 8
GATE_LANES_PER_DIR = 3 * ML_HEADS
VMEM_LIMIT = 56 * 1024 * 1024


def _params(n_axes, flags=None):
    return pltpu.CompilerParams(dimension_semantics=("arbitrary",) * n_axes, vmem_limit_bytes=VMEM_LIMIT, flags=flags)


def _sigmoid(x):
    return 1.0 / (1.0 + jnp.exp(-x))


def _log_sigmoid(x):
    return jnp.minimum(x, 0.0) - jnp.log1p(jnp.exp(-jnp.abs(x)))


def _rms_scale(x):
    return x * lax.rsqrt(jnp.mean(x * x, axis=-1, keepdims=True) + EPS)


def _scan_chunk(x, axis, reverse, op, identity):
    n = x.shape[axis]
    idx = lax.broadcasted_iota(jnp.int32, x.shape, axis)
    sh = 1
    while sh < n:
        if reverse:
            x = op(x, jnp.where(idx < n - sh, pltpu.roll(x, n - sh, axis), identity))
        else:
            x = op(x, jnp.where(idx >= sh, pltpu.roll(x, sh, axis), identity))
        sh *= 2
    return x


def _in_proj_kernel(hp_ref, h_ref, hn_ref, g_ref, wna_ref, wqk_ref, wvo_ref, wgr_ref, bgr_ref, cw_ref, cb_ref,
                    qna_ref, kna_ref, vna_ref, qml_ref, kt_ref, vml_ref, oml_ref, gc_ref, gr_ref,
                    zbuf, *, tiles_per_seq):
    i = pl.program_id(0)
    tm = h_ref.shape[0]
    g = g_ref[...]

    def norm(x):
        return (_rms_scale(x) * g).astype(BF16)

    a = norm(h_ref[...])

    pos = i % tiles_per_seq
    a_ext = jnp.concatenate([a, norm(hp_ref[...]), norm(hn_ref[...])], axis=0)
    z_ext = jnp.dot(a_ext, wqk_ref[...], preferred_element_type=F32)
    zbuf[0:HALO, :] = jnp.where(pos == 0, 0.0, z_ext[tm:tm + HALO])
    zbuf[HALO:HALO + tm, :] = z_ext[0:tm]
    zbuf[HALO + tm:2 * HALO + tm, :] = jnp.where(pos == tiles_per_seq - 1, 0.0, z_ext[tm + HALO:tm + 2 * HALO])

    ng = 2 * GATE_LANES_PER_DIR
    nt = (((1,), (1,)), ((), ()))
    gi = lax.dot_general(wgr_ref[0:ng, :], a, nt, preferred_element_type=F32) + bgr_ref[0:ng, :]
    gf = lax.dot_general(wgr_ref[ng:2 * ng, :], a, nt, preferred_element_type=F32) + bgr_ref[ng:2 * ng, :]
    row = lax.broadcasted_iota(jnp.int32, (ng, ML_CHUNK), 0)
    is_fw = row < GATE_LANES_PER_DIR
    qsel = jnp.where(is_fw, row, row - GATE_LANES_PER_DIR)
    pad_rows = jnp.zeros((ML_CHUNK - ng, ML_CHUNK), F32)

    def gates(c):
        cols_c = slice(c * ML_CHUNK, (c + 1) * ML_CHUNK)
        lf = _log_sigmoid(gf[:, cols_c])
        b = jnp.where(is_fw, _scan_chunk(lf, 1, False, jnp.add, 0.0), _scan_chunk(lf, 1, True, jnp.add, 0.0))
        r = gi[:, cols_c] - b
        cmax = jnp.where(is_fw, _scan_chunk(r, 1, False, jnp.maximum, -jnp.inf),
                         _scan_chunk(r, 1, True, jnp.maximum, -jnp.inf))
        packed = jnp.where(qsel < ML_HEADS, r, jnp.where(qsel < 2 * ML_HEADS, cmax, b))
        gc_ref[cols_c, :] = jnp.concatenate([packed, pad_rows], axis=0).T
        gr_ref[:, cols_c] = jnp.concatenate([packed[0:ML_HEADS], packed[GATE_LANES_PER_DIR:GATE_LANES_PER_DIR + ML_HEADS]], axis=0)

    def conv(c0, cstep=256):
        acc = cb_ref[:, c0:c0 + cstep]
        for j in range(ML_CONV):
            r0 = HALO - ML_CONV // 2 + j
            acc = acc + zbuf[r0:r0 + tm, c0:c0 + cstep] * cw_ref[j:j + 1, c0:c0 + cstep]
        y = acc * _sigmoid(acc)
        if c0 < D_ML:
            qml_ref[:, c0:c0 + cstep] = (y * (ML_HEAD_DIM ** -0.5)).astype(BF16)
        else:
            kt_ref[c0 - D_ML:c0 - D_ML + cstep, :] = y.T.astype(BF16)

    def proj(w_ref, c0, scale=None):
        z = jnp.dot(a, w_ref[:, c0:c0 + D_NA], preferred_element_type=F32)
        return (z if scale is None else z * scale).astype(BF16)

    conv(0)
    qna_ref[...] = proj(wna_ref, 0, NA_HEAD_DIM ** -0.5 * LOG2E)
    conv(256)
    kna_ref[...] = proj(wna_ref, D_NA)
    conv(512)
    vna_ref[...] = proj(wna_ref, 2 * D_NA)
    conv(768)
    vml_ref[...] = proj(wvo_ref, 0)
    for c in range(tm // ML_CHUNK):
        gates(c)
    oml_ref[...] = proj(wvo_ref, D_ML)


def _in_proj(h, seq, g_mix, w_in, b_gate, conv_w, conv_b):
    t = h.shape[0]
    tm = IN_PROJ_TILE
    n_tiles = t // tm
    tiles_per_seq = seq // tm
    nh = 4 * ML_HEADS
    o_qk = 3 * D_NA
    o_v = o_qk + 2 * D_ML
    o_g = o_v + 2 * D_ML
    wna = w_in[:, :o_qk].astype(BF16)
    wqk = w_in[:, o_qk:o_v].astype(BF16)
    wvo = w_in[:, o_v:o_g].astype(BF16)
    wg = w_in[:, o_g:o_g + nh]
    ng = 2 * GATE_LANES_PER_DIR
    gate_id = np.arange(ng)
    src_i = (gate_id // GATE_LANES_PER_DIR) * 2 * ML_HEADS + gate_id % ML_HEADS
    wgr = jnp.concatenate([wg[:, src_i], wg[:, src_i + ML_HEADS]], axis=1).T.astype(BF16)
    bgr = jnp.concatenate([b_gate[src_i], b_gate[src_i + ML_HEADS]]).reshape(2 * ng, 1)
    cw = jnp.pad(conv_w, ((0, SUBLANES - ML_CONV), (0, 0)))
    cb = conv_b.reshape(1, 2 * D_ML)
    full = lambda shape: pl.BlockSpec(shape, lambda i: (0,) * len(shape))
    row_blk = lambda w: pl.BlockSpec((tm, w), lambda i: (i, 0))
    hb = tm // HALO
    out_bf = jax.ShapeDtypeStruct((t, D_NA), BF16)
    return pl.pallas_call(
        functools.partial(_in_proj_kernel, tiles_per_seq=tiles_per_seq),
        grid=(n_tiles,),
        in_specs=[
            pl.BlockSpec((HALO, D_MODEL), lambda i: (jnp.maximum(i * hb - 1, 0), 0)),
            row_blk(D_MODEL),
            pl.BlockSpec((HALO, D_MODEL), lambda i: (jnp.minimum((i + 1) * hb, t // HALO - 1), 0)),
            full((1, D_MODEL)), full((D_MODEL, o_qk)), full((D_MODEL, 2 * D_ML)), full((D_MODEL, 2 * D_ML)),
            full((2 * ng, D_MODEL)), full((2 * ng, 1)),
            full((SUBLANES, 2 * D_ML)), full((1, 2 * D_ML)),
        ],
        out_specs=[row_blk(D_NA)] * 4 + [pl.BlockSpec((D_ML, tm), lambda i: (0, i))] + [row_blk(D_NA)] * 2
        + [row_blk(LANES), pl.BlockSpec((2 * ML_HEADS, tm), lambda i: (0, i))],
        out_shape=[out_bf] * 4 + [jax.ShapeDtypeStruct((D_ML, t), BF16)] + [out_bf] * 2
        + [jax.ShapeDtypeStruct((t, LANES), F32), jax.ShapeDtypeStruct((2 * ML_HEADS, t), F32)],
        scratch_shapes=[pltpu.VMEM((tm + 2 * HALO, 2 * D_ML), F32)],
        compiler_params=_params(1),
        name="in_proj",
    )(h, h, h, g_mix.reshape(1, D_MODEL), wna, wqk, wvo, wgr, bgr, cw, cb)


def _na_bias_table(rpb, rows):
    kh = min(NA_WIN_H, rows)
    o = np.arange(kh)[:, None]
    kr = np.arange(kh)[None, :]
    dr = kr - o + (NA_WIN_H - 1)
    qc = np.arange(GRID_W)[:, None]
    kc = np.arange(GRID_W)[None, :]
    cs = np.clip(qc - NA_WIN_W // 2, 0, GRID_W - NA_WIN_W)
    inw = (kc >= cs) & (kc < cs + NA_WIN_W)
    dc = np.clip(kc - qc, 1 - NA_WIN_W, NA_WIN_W - 1) + (NA_WIN_W - 1)
    sel_r = jnp.asarray(dr[:, :, None] == np.arange(2 * NA_WIN_H - 1), F32)
    sel_c = jnp.asarray(dc[:, :, None] == np.arange(2 * NA_WIN_W - 1), F32)
    rows_sel = jnp.einsum("hrc,okr->hokc", rpb.astype(F32), sel_r, precision=lax.Precision.HIGHEST)
    b = jnp.einsum("hokc,qlc->hoqkl", rows_sel, sel_c, precision=lax.Precision.HIGHEST)
    b = jnp.where(inw[None, None, :, None, :], b * LOG2E, NEG_INF)
    return b.reshape(rpb.shape[0], kh, GRID_W, kh * GRID_W)


NA_HEADS_PER_BLOCK = 4
NA_BAND_ROWS = 64
NA_UNROLL = 16


def _na_kernel(q_ref, k_ref, v_ref, bias_ref, g_ref, o_ref, *, rows, kh, band):
    nkeys = kh * GRID_W
    hpb = NA_HEADS_PER_BLOCK
    m_rows = hpb * GRID_W
    width = hpb * NA_HEAD_DIM
    band_i = pl.program_id(2)
    lane_head = lax.shift_right_logical(lax.broadcasted_iota(jnp.int32, (m_rows, width), 1), 6)
    row_head = lax.shift_right_logical(lax.broadcasted_iota(jnp.int32, (m_rows, width), 0), 6)
    own = lane_head == row_head
    g = g_ref[...]

    def one_row(rl):
        r = band_i * band + rl
        rs = jnp.clip(r - kh // 2, 0, rows - kh)
        off = r - rs
        q = q_ref[0, pl.ds(pl.multiple_of(rl * GRID_W, GRID_W), GRID_W), :]
        kb = k_ref[0, pl.ds(pl.multiple_of(rs * GRID_W, GRID_W), nkeys), :]
        vb = v_ref[0, pl.ds(pl.multiple_of(rs * GRID_W, GRID_W), nkeys), :]
        qs = jnp.where(own, jnp.concatenate([q] * hpb, axis=0), jnp.zeros((m_rows, width), BF16))
        s = lax.dot_general(qs, kb, (((1,), (1,)), ((), ())), preferred_element_type=F32)
        s = s + bias_ref[:, off].reshape(m_rows, nkeys)
        m = jnp.max(s, axis=-1, keepdims=True)
        p = jnp.exp2(s - m)
        l = jnp.sum(p, axis=-1, keepdims=True)
        pv = jnp.dot(p.astype(BF16), vb, preferred_element_type=F32)
        oh = jnp.where(own, pv * (1.0 / l), 0.0)
        ms = jnp.sum(oh * oh, axis=-1, keepdims=True) * (1.0 / NA_HEAD_DIM)
        yn = oh * lax.rsqrt(ms + EPS)
        y = yn[0:GRID_W]
        for hh in range(1, hpb):
            y = y + yn[hh * GRID_W:(hh + 1) * GRID_W]
        o_ref[0, pl.ds(pl.multiple_of(rl * GRID_W, GRID_W), GRID_W), :] = (y * g).astype(BF16)

    def body(it, carry):
        for u in range(NA_UNROLL):
            one_row(it * NA_UNROLL + u)
        return carry

    lax.fori_loop(0, band // NA_UNROLL, body, 0)


def _na_attention(q, k, v, bias, g_na, batch, seq):
    rows = seq // GRID_W
    kh = min(NA_WIN_H, rows)
    hpb = NA_HEADS_PER_BLOCK
    width = hpb * NA_HEAD_DIM
    band = min(NA_BAND_ROWS, rows)
    assert rows % band == 0 and band % NA_UNROLL == 0
    q3, k3, v3 = (x.reshape(batch, seq, D_NA) for x in (q, k, v))
    seq_blk = pl.BlockSpec((1, seq, width), lambda b, j, r: (b, 0, j))
    band_blk = pl.BlockSpec((1, band * GRID_W, width), lambda b, j, r: (b, r, j))
    out = pl.pallas_call(
        functools.partial(_na_kernel, rows=rows, kh=kh, band=band),
        grid=(batch, D_NA // width, rows // band),
        in_specs=[band_blk, seq_blk, seq_blk,
                  pl.BlockSpec((hpb, kh, GRID_W, kh * GRID_W), lambda b, j, r: (j, 0, 0, 0)),
                  pl.BlockSpec((1, width), lambda b, j, r: (0, j))],
        out_specs=band_blk,
        out_shape=jax.ShapeDtypeStruct((batch, seq, D_NA), BF16),
        compiler_params=_params(3),
        name="na_attn",
    )(q3, k3, v3, bias, g_na.reshape(1, D_NA))
    return out.reshape(batch * seq, D_NA)


ML_UNROLL = 8
ML_HEADS_PER_STEP = 1


def _mlstm_kernel(q_ref, kt_ref, v_ref, gr_ref, gc_ref, o_ref, st_ref, *, n_chunks):
    L = ML_CHUNK
    hps = ML_HEADS_PER_STEP
    lane = lax.broadcasted_iota(jnp.int32, (L, LANES), 1)
    grow_i = lax.broadcasted_iota(jnp.int32, (2 * ML_HEADS, L), 0)
    row_i = lax.broadcasted_iota(jnp.int32, (L, L), 0)
    col_i = lax.broadcasted_iota(jnp.int32, (L, L), 1)
    ones = jnp.ones((L, ML_HEAD_DIM), BF16)

    def step(c, m_prev, hb, reverse):
        off = pl.multiple_of(c * L, L)
        d = 1 if reverse else 0
        hd = pl.program_id(1) * hps + hb
        cols = slice(hb * ML_HEAD_DIM, (hb + 1) * ML_HEAD_DIM)
        qc = q_ref[0, pl.ds(off, L), cols]
        kt = kt_ref[cols, pl.ds(off, L)]
        vc = v_ref[0, pl.ds(off, L), cols]
        gcol = gc_ref[pl.ds(off, L), :]
        grow = gr_ref[:, pl.ds(off, L)]

        def col(qn):
            sel = lane == d * GATE_LANES_PER_DIR + qn * ML_HEADS + hd
            return jnp.sum(jnp.where(sel, gcol, 0.0), axis=1, keepdims=True)

        r_col, cmax_col, b_col = col(0), col(1), col(2)
        r_row = jnp.sum(jnp.where(grow_i == d * ML_HEADS + hd, grow, 0.0), axis=0, keepdims=True)
        end = 0 if reverse else L - 1
        mm_end = jnp.maximum(m_prev, cmax_col[end:end + 1, :])
        m_new = b_col[end:end + 1, :] + mm_end
        mm_col = jnp.maximum(m_prev, cmax_col)
        causal = (col_i >= row_i) if reverse else (col_i <= row_i)
        dmat = jnp.exp(jnp.where(causal, r_row - mm_col, -jnp.inf))
        s = jnp.dot(qc, kt, preferred_element_type=F32) * dmat
        w_inter = jnp.exp(m_prev - mm_col)
        floor = jnp.exp(-(b_col + mm_col))
        w_col = jnp.exp(r_col - mm_end)
        decay = jnp.exp(m_prev - mm_end)
        vext = jnp.concatenate([vc, ones], axis=1)
        state = st_ref[d, hb]
        nd = (jnp.dot(s.astype(BF16), vext, preferred_element_type=F32)
              + w_inter * jnp.dot(qc, state.astype(BF16), preferred_element_type=F32))
        h = nd[:, :ML_HEAD_DIM] / jnp.maximum(jnp.abs(nd[:, ML_HEAD_DIM:]), floor)
        wv = jnp.concatenate([(w_col * vc.astype(F32)).astype(BF16),
                              jnp.broadcast_to(w_col, (L, ML_HEAD_DIM)).astype(BF16)], axis=1)
        st_ref[d, hb] = decay * state + jnp.dot(kt, wv, preferred_element_type=F32)
        return h, m_new

    st_ref[...] = jnp.zeros_like(st_ref)

    def sweep(accumulate):
        def body(it, carry):
            m = list(carry)
            for u in range(ML_UNROLL):
                j = it * ML_UNROLL + u
                jb = n_chunks - 1 - j
                rows_f = pl.ds(pl.multiple_of(j * L, L), L)
                rows_b = pl.ds(pl.multiple_of(jb * L, L), L)
                for hb in range(hps):
                    cols = slice(hb * ML_HEAD_DIM, (hb + 1) * ML_HEAD_DIM)
                    h_f, m[hb] = step(j, m[hb], hb, False)
                    h_b, m[hps + hb] = step(jb, m[hps + hb], hb, True)
                    if accumulate:
                        o_ref[0, rows_f, cols] += h_f
                        o_ref[0, rows_b, cols] += h_b
                    else:
                        o_ref[0, rows_f, cols] = h_f
                        o_ref[0, rows_b, cols] = h_b
            return tuple(m)
        return body

    half = n_chunks // (2 * ML_UNROLL)
    zero = jnp.zeros((1, 1), F32)
    carry = lax.fori_loop(0, half, sweep(False), (zero,) * (2 * hps))
    lax.fori_loop(half, 2 * half, sweep(True), carry)


def _mlstm(q, kt, v, g_row, g_col, batch, seq):
    n_chunks = seq // ML_CHUNK
    assert n_chunks % (2 * ML_UNROLL) == 0
    hps = ML_HEADS_PER_STEP
    width = hps * ML_HEAD_DIM
    q3, v3 = (x.reshape(batch, seq, D_ML) for x in (q, v))
    seq_blk = pl.BlockSpec((1, seq, width), lambda b, j: (b, 0, j))
    once_per_batch = {}
    out = pl.pallas_call(
        functools.partial(_mlstm_kernel, n_chunks=n_chunks),
        grid=(batch, ML_HEADS // hps),
        in_specs=[seq_blk, pl.BlockSpec((width, seq), lambda b, j: (j, b)), seq_blk,
                  pl.BlockSpec((2 * ML_HEADS, seq), lambda b, j: (0, b), **once_per_batch),
                  pl.BlockSpec((seq, LANES), lambda b, j: (b, 0), **once_per_batch)],
        out_specs=seq_blk,
        out_shape=jax.ShapeDtypeStruct((batch, seq, D_ML), F32),
        scratch_shapes=[pltpu.VMEM((2, hps, ML_HEAD_DIM, 2 * ML_HEAD_DIM), F32)],
        compiler_params=_params(2),
        name="mlstm",
    )(q3, kt, v3, g_row, g_col)
    return out.reshape(batch * seq, D_ML)


def _mix_out_kernel(h_ref, yna_ref, hml_ref, oml_ref, gml_ref, wo_ref, gmoe_ref, wrh_ref, wrl_ref, br_ref, tri_ref,
                    h1_ref, t_ref, cnt_ref):
    i = pl.program_id(0)
    tm = h_ref.shape[0]

    @pl.when(i == 0)
    def _():
        cnt_ref[...] = jnp.zeros_like(cnt_ref)

    hml = hml_ref[...]
    parts = [_rms_scale(hml[:, d * ML_HEAD_DIM:(d + 1) * ML_HEAD_DIM]) for d in range(ML_HEADS)]
    yml = _sigmoid(oml_ref[...].astype(F32)) * (jnp.concatenate(parts, axis=1) * gml_ref[...])
    mix = (jnp.dot(yna_ref[...], wo_ref[0:D_NA, :], preferred_element_type=F32)
           + jnp.dot(yml.astype(BF16), wo_ref[D_NA:D_NA + D_ML, :], preferred_element_type=F32))
    h1 = h_ref[...] + mix
    h1_ref[...] = h1
    t = _rms_scale(h1) * gmoe_ref[...]
    bits = lax.bitcast_convert_type(t.astype(BF16).astype(F32), U32)
    t_ref[:, 0:PACKED] = bits[:, PACKED:] | lax.shift_right_logical(bits[:, :PACKED], U32(16))

    t_hi = t.astype(BF16)
    t_lo = (t - t_hi.astype(F32)).astype(BF16)
    nt = (((1,), (1,)), ((), ()))
    logits = (lax.dot_general(wrh_ref[...], t_hi, nt, preferred_element_type=F32)
              + lax.dot_general(wrh_ref[...], t_lo, nt, preferred_element_type=F32)
              + lax.dot_general(wrl_ref[...], t_hi, nt, preferred_element_type=F32)) + br_ref[...]
    epg = EXPERTS_PER_GROUP
    row8 = lax.broadcasted_iota(jnp.int32, (epg, tm), 0)

    def top(x):
        mx = jnp.max(x, axis=0, keepdims=True)
        idx = jnp.min(jnp.where(x == mx, row8, epg), axis=0, keepdims=True)
        return mx, idx

    lg = jnp.where(row8 < N_GROUPS, logits[0:epg], -jnp.inf)
    g_max, g_idx = top(lg)
    pg_top = 1.0 / jnp.sum(jnp.exp(lg - g_max), axis=0, keepdims=True)
    le = jnp.zeros((epg, tm), F32)
    for grp in range(N_GROUPS):
        le = jnp.where(g_idx == grp, logits[EXPERT_LANE0 + grp * epg:EXPERT_LANE0 + (grp + 1) * epg], le)
    e1_max, i1 = top(le)
    e_sum = jnp.sum(jnp.exp(le - e1_max), axis=0, keepdims=True)
    e2_max, i2 = top(jnp.where(row8 == i1, -jnp.inf, le))
    p1 = 1.0 / e_sum
    p2 = jnp.exp(e2_max - e1_max) / e_sum
    w1 = pg_top * p1 / (p1 + p2)
    w2 = pg_top * p2 / (p1 + p2)

    la = jnp.minimum(i1, i2)
    lb = jnp.maximum(i1, i2)
    cls = g_idx * PAIRS_PER_GROUP + lax.shift_right_logical(la * (2 * epg - 1 - la), 1) + (lb - la - 1)
    w_lo = jnp.where(i1 < i2, w1, w2)
    w_hi = jnp.where(i1 < i2, w2, w1)
    oh = (lax.broadcasted_iota(jnp.int32, (LANES, tm), 0) == cls).astype(F32)
    before = jnp.dot(oh.astype(BF16), tri_ref[...], preferred_element_type=F32) + cnt_ref[:, 0:1]
    rank = jnp.sum(oh * before, axis=0, keepdims=True)
    cnt_ref[...] = cnt_ref[...] + jnp.sum(oh, axis=1, keepdims=True)
    info = jnp.concatenate([cls.astype(F32), rank, w_lo, w_hi, jnp.zeros((LANES - 4, tm), F32)], axis=0)
    for c in range(tm // LANES):
        t_ref[c * LANES:(c + 1) * LANES, PACKED:PACKED + LANES] = lax.bitcast_convert_type(
            info[:, c * LANES:(c + 1) * LANES].T, U32)


def _mix_out(h, yna, hml, oml, g_ml, w_out, g_moe, w_rg, b_rg, w_re, b_re):
    t = h.shape[0]
    tm = ROW_TILE
    wr = jnp.zeros((LANES, D_MODEL), F32)
    wr = wr.at[:N_GROUPS].set(w_rg.T).at[EXPERT_LANE0:EXPERT_LANE0 + N_EXPERTS].set(w_re.T)
    wrh = wr.astype(BF16)
    wrl = (wr - wrh.astype(F32)).astype(BF16)
    br = jnp.zeros((LANES, 1), F32)
    br = br.at[:N_GROUPS, 0].set(b_rg).at[EXPERT_LANE0:EXPERT_LANE0 + N_EXPERTS, 0].set(b_re)
    tri = jnp.asarray(np.triu(np.ones((tm, tm), np.float32), 1), BF16)
    full = lambda shape: pl.BlockSpec(shape, lambda i: (0,) * len(shape))
    row_blk = lambda w: pl.BlockSpec((tm, w), lambda i: (i, 0))
    return pl.pallas_call(
        _mix_out_kernel,
        grid=(t // tm,),
        in_specs=[row_blk(D_MODEL), row_blk(D_NA), row_blk(D_ML), row_blk(D_ML), full((1, D_ML)),
                  full((D_MODEL, D_MODEL)), full((1, D_MODEL)), full((LANES, D_MODEL)), full((LANES, D_MODEL)),
                  full((LANES, 1)), full((tm, tm))],
        out_specs=[row_blk(D_MODEL), row_blk(ROUTED_WIDTH), full((LANES, LANES))],
        out_shape=[jax.ShapeDtypeStruct((t, D_MODEL), F32), jax.ShapeDtypeStruct((t, ROUTED_WIDTH), U32),
                   jax.ShapeDtypeStruct((LANES, LANES), F32)],
        compiler_params=_params(1),
        name="mix_out",
    )(h, yna, hml, oml, g_ml.reshape(1, D_ML), w_out.astype(BF16), g_moe.reshape(1, D_MODEL), wrh, wrl, br, tri)


def _dispatch_kernel(zflag_ref, dest_ref, t_ref, xs_ref, zbuf, ring, sem, zsem, *, n_blocks, n_tiles):
    tm = t_ref.shape[0]

    @pl.when(pl.program_id(0) == 0)
    def _():
        zbuf[...] = jnp.zeros_like(zbuf)

        def zero_copy(b):
            return pltpu.make_async_copy(zbuf, xs_ref.at[pl.ds(pl.multiple_of(b * MOE_BLOCK, MOE_BLOCK), MOE_BLOCK)], zsem)

        def zissue(b, carry):
            @pl.when(zflag_ref[b] != 0)
            def _():
                zero_copy(b).start()
            return carry

        def zwait(b, carry):
            @pl.when(zflag_ref[b] != 0)
            def _():
                zero_copy(b).wait()
            return carry

        lax.fori_loop(0, n_blocks, zissue, 0)
        lax.fori_loop(0, n_blocks, zwait, 0)

    i = pl.program_id(0)
    slot = i % 2

    def wait_slot(s):
        pltpu.make_async_copy(ring.at[s], xs_ref.at[pl.ds(0, tm)], sem.at[s]).wait()

    @pl.when(i >= 2)
    def _():
        wait_slot(slot)

    ring[slot] = t_ref[...]

    def issue(it, carry):
        for u in range(ISSUE_UNROLL):
            j = it * ISSUE_UNROLL + u
            pltpu.make_async_copy(ring.at[slot, pl.ds(j, 1)], xs_ref.at[pl.ds(dest_ref[0, 0, j], 1)], sem.at[slot]).start()
        return carry

    lax.fori_loop(0, tm // ISSUE_UNROLL, issue, 0)

    @pl.when(i == n_tiles - 1)
    def _():
        wait_slot(slot)

        @pl.when(i >= 1)
        def _():
            wait_slot(1 - slot)


def _dispatch(t, dest, zflag, n_blocks):
    tt = t.shape[0]
    tm = ROW_TILE
    n_tiles = tt // tm
    return pl.pallas_call(
        functools.partial(_dispatch_kernel, n_blocks=n_blocks, n_tiles=n_tiles),
        grid_spec=pltpu.PrefetchScalarGridSpec(
            num_scalar_prefetch=1,
            grid=(n_tiles,),
            in_specs=[pl.BlockSpec((1, 1, tm), lambda i, zf: (i, 0, 0), memory_space=pltpu.SMEM),
                      pl.BlockSpec((tm, ROUTED_WIDTH), lambda i, zf: (i, 0))],
            out_specs=pl.BlockSpec(memory_space=pl.ANY),
            scratch_shapes=[pltpu.VMEM((MOE_BLOCK, ROUTED_WIDTH), U32), pltpu.VMEM((2, tm, ROUTED_WIDTH), U32),
                            pltpu.SemaphoreType.DMA((2,)), pltpu.SemaphoreType.DMA],
        ),
        out_shape=jax.ShapeDtypeStruct((n_blocks * MOE_BLOCK, ROUTED_WIDTH), U32),
        compiler_params=_params(1),
        name="dispatch",
    )(zflag, dest.reshape(n_tiles, 1, tm), t)


def _ffn_kernel(ea_ref, eb_ref, nu_ref, x_ref, wg_hbm, wu_hbm, wd_hbm, y_ref,
                wg_s, wu_s, wd_s, stage_g, stage_u, stage_d, sem, *, layer):
    i = pl.program_id(0)
    used = i < nu_ref[0]
    group = lax.shift_right_logical(ea_ref[i], 3)
    prev_group = lax.shift_right_logical(ea_ref[jnp.maximum(i - 1, 0)], 3)
    mats = ((wg_hbm, stage_g, wg_s), (wu_hbm, stage_u, wu_s), (wd_hbm, stage_d, wd_s))

    @pl.when(used & ((i == 0) | (group != prev_group)))
    def _():
        def fetch(m, e):
            w_hbm, stage, _ = mats[m]
            return pltpu.make_async_copy(w_hbm.at[layer, group * EXPERTS_PER_GROUP + e], stage, sem.at[m])

        for m in range(3):
            fetch(m, 0).start()

        def load_expert(e, carry):
            for m in range(3):
                _, stage, w_s = mats[m]
                fetch(m, e).wait()
                w_s[e] = stage[...].astype(BF16)

                @pl.when(e + 1 < EXPERTS_PER_GROUP)
                def _():
                    fetch(m, e + 1).start()
            return carry

        lax.fori_loop(0, EXPERTS_PER_GROUP, load_expert, 0)

    def swiglu(x, e):
        g = jnp.dot(x, wg_s[e], preferred_element_type=F32)
        u = jnp.dot(x, wu_s[e], preferred_element_type=F32)
        hid = (g * _sigmoid(g) * u).astype(BF16)
        return jnp.dot(hid, wd_s[e], preferred_element_type=F32)

    @pl.when(used)
    def _():
        words = x_ref[:, 0:PACKED]
        as_f32 = lambda u: lax.bitcast_convert_type(u, F32)
        x = jnp.concatenate([as_f32(lax.shift_left(words, U32(16))).astype(BF16),
                             as_f32(words & U32(0xFFFF0000)).astype(BF16)], axis=1)
        y_ref[...] = (as_f32(x_ref[:, INFO_W_LO:INFO_W_LO + 1]) * swiglu(x, ea_ref[i] & (EXPERTS_PER_GROUP - 1))
                      + as_f32(x_ref[:, INFO_W_HI:INFO_W_HI + 1]) * swiglu(x, eb_ref[i] & (EXPERTS_PER_GROUP - 1)))

    @pl.when(jnp.logical_not(used))
    def _():
        y_ref[...] = jnp.zeros_like(y_ref)


def _expert_ffn(xs, block_ea, block_eb, n_used, w_gate, w_up, w_down, layer):
    n_rows = xs.shape[0]
    nb = n_rows // MOE_BLOCK
    blk = lambda i, ea, eb, nu: (jnp.minimum(i, nu[0] - 1), 0)
    hbm = pl.BlockSpec(memory_space=pl.ANY)
    return pl.pallas_call(
        functools.partial(_ffn_kernel, layer=layer),
        grid_spec=pltpu.PrefetchScalarGridSpec(
            num_scalar_prefetch=3,
            grid=(nb,),
            in_specs=[pl.BlockSpec((MOE_BLOCK, ROUTED_WIDTH), blk), hbm, hbm, hbm],
            out_specs=pl.BlockSpec((MOE_BLOCK, D_MODEL), lambda i, ea, eb, nu: (i, 0)),
            scratch_shapes=[pltpu.VMEM((EXPERTS_PER_GROUP, D_MODEL, D_EXPERT), BF16),
                            pltpu.VMEM((EXPERTS_PER_GROUP, D_MODEL, D_EXPERT), BF16),
                            pltpu.VMEM((EXPERTS_PER_GROUP, D_EXPERT, D_MODEL), BF16),
                            pltpu.VMEM((D_MODEL, D_EXPERT), F32), pltpu.VMEM((D_MODEL, D_EXPERT), F32),
                            pltpu.VMEM((D_EXPERT, D_MODEL), F32), pltpu.SemaphoreType.DMA((3,))],
        ),
        out_shape=jax.ShapeDtypeStruct((n_rows, D_MODEL), F32),
        compiler_params=_params(1),
        name="expert_ffn",
    )(block_ea, block_eb, n_used, xs, w_gate, w_up, w_down)


def _combine_kernel(dcur_ref, dnxt_ref, h1_ref, p_ref, gple_ref, wple_ref, wpg_ref, gfin_ref, ys_ref,
                    o_ref, buf, sem, *, final, n_tiles):
    i = pl.program_id(0)
    tm = h1_ref.shape[0]
    slot = i % 2

    def issue_tile(d_ref, s):
        def issue(it, carry):
            for u in range(ISSUE_UNROLL):
                j = it * ISSUE_UNROLL + u
                pltpu.make_async_copy(ys_ref.at[pl.ds(d_ref[0, 0, j], 1)], buf.at[s, pl.ds(j, 1)], sem.at[s]).start()
            return carry

        lax.fori_loop(0, tm // ISSUE_UNROLL, issue, 0)

    @pl.when(i == 0)
    def _():
        issue_tile(dcur_ref, 0)

    @pl.when(i + 1 < n_tiles)
    def _():
        issue_tile(dnxt_ref, 1 - slot)

    pltpu.make_async_copy(ys_ref.at[pl.ds(0, tm)], buf.at[slot], sem.at[slot]).wait()

    h2 = h1_ref[...] + buf[slot]
    a = (_rms_scale(h2) * gple_ref[...]).astype(BF16)
    gate = _sigmoid(jnp.dot(a, wpg_ref[...], preferred_element_type=F32))
    emb = jnp.dot(p_ref[0].astype(BF16), wple_ref[...], preferred_element_type=F32)
    h3 = h2 + emb * gate
    if final:
        h3 = _rms_scale(h3) * gfin_ref[...]
    o_ref[...] = h3


def _combine(h1, ys, dest, p_all, layer, g_ple, w_ple, w_pg, g_final, final):
    tt = h1.shape[0]
    tm = ROW_TILE
    n_tiles = tt // tm
    dest3 = dest.reshape(n_tiles, 1, tm)
    full = lambda shape: pl.BlockSpec(shape, lambda i: (0,) * len(shape))
    row_blk = lambda w: pl.BlockSpec((tm, w), lambda i: (i, 0))
    return pl.pallas_call(
        functools.partial(_combine_kernel, final=final, n_tiles=n_tiles),
        grid=(n_tiles,),
        in_specs=[pl.BlockSpec((1, 1, tm), lambda i: (i, 0, 0), memory_space=pltpu.SMEM),
                  pl.BlockSpec((1, 1, tm), lambda i: (jnp.minimum(i + 1, n_tiles - 1), 0, 0), memory_space=pltpu.SMEM),
                  row_blk(D_MODEL), pl.BlockSpec((1, tm, D_PLE), lambda i: (layer, i, 0)), full((1, D_MODEL)),
                  full((D_PLE, D_MODEL)), full((D_MODEL, D_MODEL)), full((1, D_MODEL)),
                  pl.BlockSpec(memory_space=pl.ANY)],
        out_specs=row_blk(D_MODEL),
        out_shape=jax.ShapeDtypeStruct((tt, D_MODEL), F32),
        scratch_shapes=[pltpu.VMEM((2, tm, D_MODEL), F32), pltpu.SemaphoreType.DMA((2,))],
        compiler_params=_params(1),
        name="combine",
    )(dest3, dest3, h1, p_all, g_ple.reshape(1, D_MODEL), w_ple.astype(BF16), w_pg.astype(BF16),
      g_final.reshape(1, D_MODEL), ys)


def _class_experts():
    lo, hi = [], []
    for g in range(N_GROUPS):
        for a in range(EXPERTS_PER_GROUP):
            for b in range(a + 1, EXPERTS_PER_GROUP):
                lo.append(g * EXPERTS_PER_GROUP + a)
                hi.append(g * EXPERTS_PER_GROUP + b)
    return np.asarray(lo, np.int32), np.asarray(hi, np.int32)


def _routing_tables(routed, counts_f, n_blocks):
    info = lax.bitcast_convert_type(routed[:, INFO_CLASS:INFO_RANK + 1], F32)
    cls = info[:, 0].astype(jnp.int32)
    rank = info[:, 1].astype(jnp.int32)
    counts = counts_f[:N_CLASSES, 0].astype(jnp.int32)
    padded = ((counts + MOE_BLOCK - 1) // MOE_BLOCK) * MOE_BLOCK
    pad_end = jnp.cumsum(padded)
    pad_start = pad_end - padded
    ids = jnp.arange(N_CLASSES, dtype=jnp.int32)
    dest = jnp.sum(jnp.where(cls[:, None] == ids, pad_start, 0), axis=-1) + rank
    n_used = (pad_end[-1] // MOE_BLOCK).astype(jnp.int32).reshape(1)
    starts = jnp.arange(n_blocks, dtype=jnp.int32) * MOE_BLOCK
    block_cls = jnp.minimum(jnp.sum((pad_end[None, :] <= starts[:, None]).astype(jnp.int32), axis=1), N_CLASSES - 1)
    pick = lambda table: jnp.sum(jnp.where(block_cls[:, None] == ids, table, 0), axis=-1)
    cls_lo, cls_hi = _class_experts()
    filled = jnp.clip(pick(counts) - (starts - pick(pad_start)), 0, MOE_BLOCK)
    zflag = (filled < MOE_BLOCK).astype(jnp.int32)
    return dest, pick(jnp.asarray(cls_lo)), pick(jnp.asarray(cls_hi)), n_used, zflag


def kernel(x, p, w_in, b_gate, conv_w, conv_b, rpb, g_na, g_ml, w_out, g_mix, g_moe, w_route_group, b_route_group, w_route_expert, b_route_expert, w_exp_gate, w_exp_up, w_exp_down, g_ple, w_ple, w_ple_gate, g_final):
    batch, seq, _ = x.shape
    depth = w_in.shape[0]
    tt = batch * seq
    assert seq % ROW_TILE == 0 and seq % GRID_W == 0
    n_blocks = tt // MOE_BLOCK + N_CLASSES
    h = x.reshape(tt, D_MODEL)
    for i in range(depth):
        qna, kna, vna, qml, kt_ml, vml, oml, g_col, g_row = _in_proj(h, seq, g_mix[i], w_in[i], b_gate[i], conv_w[i], conv_b[i])
        yna = _na_attention(qna, kna, vna, _na_bias_table(rpb[i], seq // GRID_W), g_na[i], batch, seq)
        hml = _mlstm(qml, kt_ml, vml, g_row, g_col, batch, seq)
        h1, routed, counts = _mix_out(h, yna, hml, oml, g_ml[i], w_out[i], g_moe[i],
                                      w_route_group[i], b_route_group[i], w_route_expert[i], b_route_expert[i])
        dest, block_ea, block_eb, n_used, zflag = _routing_tables(routed, counts, n_blocks)
        xs = _dispatch(routed, dest, zflag, n_blocks)
        ys = _expert_ffn(xs, block_ea, block_eb, n_used, w_exp_gate, w_exp_up, w_exp_down, i)
        h = _combine(h1, ys, dest, p.reshape(depth, tt, D_PLE), i, g_ple[i], w_ple[i], w_ple_gate[i], g_final,
                     final=(i == depth - 1))
    return h.reshape(batch, seq, D_MODEL)
```

```python
import functools

import numpy as np
import jax
import jax.numpy as jnp
from jax import lax
from jax.experimental import pallas as pl
from jax.experimental.pallas import tpu as pltpu

F32 = jnp.float32
BF16 = jnp.bfloat16

D_MODEL = 1024
GRID_W = 64
D_NA = 512
NA_HEADS = 8
NA_HEAD_DIM = 64
NA_WIN_H = 8
NA_WIN_W = 16
D_ML = 512
ML_HEADS = 4
ML_HEAD_DIM = 128
ML_CONV = 5
ML_CHUNK = 128
N_GROUPS = 4
EXPERTS_PER_GROUP = 8
N_EXPERTS = 32
D_EXPERT = 512
D_PLE = 256
EPS = 1e-6
NEG_INF = -1e30
LOG2E = 1.4426950408889634

LANES = 128
SUBLANES = 8
ROW_TILE = 1024
IN_PROJ_TILE = 1024
MOE_BLOCK = 256
PAIRS_PER_GROUP = EXPERTS_PER_GROUP * (EXPERTS_PER_GROUP - 1) // 2
N_CLASSES = N_GROUPS * PAIRS_PER_GROUP
PACKED = D_MODEL // 2
ROUTED_WIDTH = PACKED + LANES
U32 = jnp.uint32
INFO_CLASS = PACKED
INFO_RANK = PACKED + 1
INFO_W_LO = PACKED + 2
INFO_W_HI = PACKED + 3
HALO = SUBLANES
ISSUE_UNROLL = 8
EXPERT_LANE0 = 8
GATE_LANES_PER_DIR = 3 * ML_HEADS
VMEM_LIMIT = 56 * 1024 * 1024


def _params(n_axes, flags=None):
    return pltpu.CompilerParams(dimension_semantics=("arbitrary",) * n_axes, vmem_limit_bytes=VMEM_LIMIT, flags=flags)


def _sigmoid(x):
    return 1.0 / (1.0 + jnp.exp(-x))


def _log_sigmoid(x):
    return jnp.minimum(x, 0.0) - jnp.log1p(jnp.exp(-jnp.abs(x)))


def _rms_scale(x):
    return x * lax.rsqrt(jnp.mean(x * x, axis=-1, keepdims=True) + EPS)


def _scan_chunk(x, axis, reverse, op, identity):
    n = x.shape[axis]
    idx = lax.broadcasted_iota(jnp.int32, x.shape, axis)
    sh = 1
    while sh < n:
        if reverse:
            x = op(x, jnp.where(idx < n - sh, pltpu.roll(x, n - sh, axis), identity))
        else:
            x = op(x, jnp.where(idx >= sh, pltpu.roll(x, sh, axis), identity))
        sh *= 2
    return x


def _in_proj_kernel(hp_ref, h_ref, hn_ref, g_ref, wna_ref, wqk_ref, wvo_ref, wgr_ref, bgr_ref, cw_ref, cb_ref,
                    qna_ref, kna_ref, vna_ref, qml_ref, kt_ref, vml_ref, oml_ref, gc_ref, gr_ref,
                    zbuf, *, tiles_per_seq):
    i = pl.program_id(0)
    tm = h_ref.shape[0]
    g = g_ref[...]

    def norm(x):
        return (_rms_scale(x) * g).astype(BF16)

    a = norm(h_ref[...])

    pos = i % tiles_per_seq
    a_ext = jnp.concatenate([a, norm(hp_ref[...]), norm(hn_ref[...])], axis=0)
    z_ext = jnp.dot(a_ext, wqk_ref[...], preferred_element_type=F32)
    zbuf[0:HALO, :] = jnp.where(pos == 0, 0.0, z_ext[tm:tm + HALO])
    zbuf[HALO:HALO + tm, :] = z_ext[0:tm]
    zbuf[HALO + tm:2 * HALO + tm, :] = jnp.where(pos == tiles_per_seq - 1, 0.0, z_ext[tm + HALO:tm + 2 * HALO])

    ng = 2 * GATE_LANES_PER_DIR
    nt = (((1,), (1,)), ((), ()))
    gi = lax.dot_general(wgr_ref[0:ng, :], a, nt, preferred_element_type=F32) + bgr_ref[0:ng, :]
    gf = lax.dot_general(wgr_ref[ng:2 * ng, :], a, nt, preferred_element_type=F32) + bgr_ref[ng:2 * ng, :]
    row = lax.broadcasted_iota(jnp.int32, (ng, ML_CHUNK), 0)
    is_fw = row < GATE_LANES_PER_DIR
    qsel = jnp.where(is_fw, row, row - GATE_LANES_PER_DIR)
    pad_rows = jnp.zeros((ML_CHUNK - ng, ML_CHUNK), F32)

    def gates(c):
        cols_c = slice(c * ML_CHUNK, (c + 1) * ML_CHUNK)
        lf = _log_sigmoid(gf[:, cols_c])
        b = jnp.where(is_fw, _scan_chunk(lf, 1, False, jnp.add, 0.0), _scan_chunk(lf, 1, True, jnp.add, 0.0))
        r = gi[:, cols_c] - b
        cmax = jnp.where(is_fw, _scan_chunk(r, 1, False, jnp.maximum, -jnp.inf),
                         _scan_chunk(r, 1, True, jnp.maximum, -jnp.inf))
        packed = jnp.where(qsel < ML_HEADS, r, jnp.where(qsel < 2 * ML_HEADS, cmax, b))
        gc_ref[cols_c, :] = jnp.concatenate([packed, pad_rows], axis=0).T
        gr_ref[:, cols_c] = jnp.concatenate([packed[0:ML_HEADS], packed[GATE_LANES_PER_DIR:GATE_LANES_PER_DIR + ML_HEADS]], axis=0)

    def conv(c0, cstep=256):
        acc = cb_ref[:, c0:c0 + cstep]
        for j in range(ML_CONV):
            r0 = HALO - ML_CONV // 2 + j
            acc = acc + zbuf[r0:r0 + tm, c0:c0 + cstep] * cw_ref[j:j + 1, c0:c0 + cstep]
        y = acc * _sigmoid(acc)
        if c0 < D_ML:
            qml_ref[:, c0:c0 + cstep] = (y * (ML_HEAD_DIM ** -0.5)).astype(BF16)
        else:
            kt_ref[c0 - D_ML:c0 - D_ML + cstep, :] = y.T.astype(BF16)

    def proj(w_ref, c0, scale=None):
        z = jnp.dot(a, w_ref[:, c0:c0 + D_NA], preferred_element_type=F32)
        return (z if scale is None else z * scale).astype(BF16)

    conv(0)
    qna_ref[...] = proj(wna_ref, 0, NA_HEAD_DIM ** -0.5 * LOG2E)
    conv(256)
    kna_ref[...] = proj(wna_ref, D_NA)
    conv(512)
    vna_ref[...] = proj(wna_ref, 2 * D_NA)
    conv(768)
    vml_ref[...] = proj(wvo_ref, 0)
    for c in range(tm // ML_CHUNK):
        gates(c)
    oml_ref[...] = proj(wvo_ref, D_ML)


def _in_proj(h, seq, g_mix, w_in, b_gate, conv_w, conv_b):
    t = h.shape[0]
    tm = IN_PROJ_TILE
    n_tiles = t // tm
    tiles_per_seq = seq // tm
    nh = 4 * ML_HEADS
    o_qk = 3 * D_NA
    o_v = o_qk + 2 * D_ML
    o_g = o_v + 2 * D_ML
    wna = w_in[:, :o_qk].astype(BF16)
    wqk = w_in[:, o_qk:o_v].astype(BF16)
    wvo = w_in[:, o_v:o_g].astype(BF16)
    wg = w_in[:, o_g:o_g + nh]
    ng = 2 * GATE_LANES_PER_DIR
    gate_id = np.arange(ng)
    src_i = (gate_id // GATE_LANES_PER_DIR) * 2 * ML_HEADS + gate_id % ML_HEADS
    wgr = jnp.concatenate([wg[:, src_i], wg[:, src_i + ML_HEADS]], axis=1).T.astype(BF16)
    bgr = jnp.concatenate([b_gate[src_i], b_gate[src_i + ML_HEADS]]).reshape(2 * ng, 1)
    cw = jnp.pad(conv_w, ((0, SUBLANES - ML_CONV), (0, 0)))
    cb = conv_b.reshape(1, 2 * D_ML)
    full = lambda shape: pl.BlockSpec(shape, lambda i: (0,) * len(shape))
    row_blk = lambda w: pl.BlockSpec((tm, w), lambda i: (i, 0))
    hb = tm // HALO
    out_bf = jax.ShapeDtypeStruct((t, D_NA), BF16)
    return pl.pallas_call(
        functools.partial(_in_proj_kernel, tiles_per_seq=tiles_per_seq),
        grid=(n_tiles,),
        in_specs=[
            pl.BlockSpec((HALO, D_MODEL), lambda i: (jnp.maximum(i * hb - 1, 0), 0)),
            row_blk(D_MODEL),
            pl.BlockSpec((HALO, D_MODEL), lambda i: (jnp.minimum((i + 1) * hb, t // HALO - 1), 0)),
            full((1, D_MODEL)), full((D_MODEL, o_qk)), full((D_MODEL, 2 * D_ML)), full((D_MODEL, 2 * D_ML)),
            full((2 * ng, D_MODEL)), full((2 * ng, 1)),
            full((SUBLANES, 2 * D_ML)), full((1, 2 * D_ML)),
        ],
        out_specs=[row_blk(D_NA)] * 4 + [pl.BlockSpec((D_ML, tm), lambda i: (0, i))] + [row_blk(D_NA)] * 2
        + [row_blk(LANES), pl.BlockSpec((2 * ML_HEADS, tm), lambda i: (0, i))],
        out_shape=[out_bf] * 4 + [jax.ShapeDtypeStruct((D_ML, t), BF16)] + [out_bf] * 2
        + [jax.ShapeDtypeStruct((t, LANES), F32), jax.ShapeDtypeStruct((2 * ML_HEADS, t), F32)],
        scratch_shapes=[pltpu.VMEM((tm + 2 * HALO, 2 * D_ML), F32)],
        compiler_params=_params(1),
        name="in_proj",
    )(h, h, h, g_mix.reshape(1, D_MODEL), wna, wqk, wvo, wgr, bgr, cw, cb)


def _na_bias_table(rpb, rows):
    kh = min(NA_WIN_H, rows)
    o = np.arange(kh)[:, None]
    kr = np.arange(kh)[None, :]
    dr = kr - o + (NA_WIN_H - 1)
    qc = np.arange(GRID_W)[:, None]
    kc = np.arange(GRID_W)[None, :]
    cs = np.clip(qc - NA_WIN_W // 2, 0, GRID_W - NA_WIN_W)
    inw = (kc >= cs) & (kc < cs + NA_WIN_W)
    dc = np.clip(kc - qc, 1 - NA_WIN_W, NA_WIN_W - 1) + (NA_WIN_W - 1)
    sel_r = jnp.asarray(dr[:, :, None] == np.arange(2 * NA_WIN_H - 1), F32)
    sel_c = jnp.asarray(dc[:, :, None] == np.arange(2 * NA_WIN_W - 1), F32)
    rows_sel = jnp.einsum("hrc,okr->hokc", rpb.astype(F32), sel_r, precision=lax.Precision.HIGHEST)
    b = jnp.einsum("hokc,qlc->hoqkl", rows_sel, sel_c, precision=lax.Precision.HIGHEST)
    b = jnp.where(inw[None, None, :, None, :], b * LOG2E, NEG_INF)
    return b.reshape(rpb.shape[0], kh, GRID_W, kh * GRID_W)


NA_HEADS_PER_BLOCK = 4
NA_BAND_ROWS = 64
NA_UNROLL = 16


def _na_kernel(q_ref, k_ref, v_ref, bias_ref, g_ref, o_ref, *, rows, kh, band):
    nkeys = kh * GRID_W
    hpb = NA_HEADS_PER_BLOCK
    m_rows = hpb * GRID_W
    width = hpb * NA_HEAD_DIM
    band_i = pl.program_id(2)
    lane_head = lax.shift_right_logical(lax.broadcasted_iota(jnp.int32, (m_rows, width), 1), 6)
    row_head = lax.shift_right_logical(lax.broadcasted_iota(jnp.int32, (m_rows, width), 0), 6)
    own = lane_head == row_head
    g = g_ref[...]

    def one_row(rl):
        r = band_i * band + rl
        rs = jnp.clip(r - kh // 2, 0, rows - kh)
        off = r - rs
        q = q_ref[0, pl.ds(pl.multiple_of(rl * GRID_W, GRID_W), GRID_W), :]
        kb = k_ref[0, pl.ds(pl.multiple_of(rs * GRID_W, GRID_W), nkeys), :]
        vb = v_ref[0, pl.ds(pl.multiple_of(rs * GRID_W, GRID_W), nkeys), :]
        qs = jnp.where(own, jnp.concatenate([q] * hpb, axis=0), jnp.zeros((m_rows, width), BF16))
        s = lax.dot_general(qs, kb, (((1,), (1,)), ((), ())), preferred_element_type=F32)
        s = s + bias_ref[:, off].reshape(m_rows, nkeys)
        m = jnp.max(s, axis=-1, keepdims=True)
        p = jnp.exp2(s - m)
        l = jnp.sum(p, axis=-1, keepdims=True)
        pv = jnp.dot(p.astype(BF16), vb, preferred_element_type=F32)
        oh = jnp.where(own, pv * (1.0 / l), 0.0)
        ms = jnp.sum(oh * oh, axis=-1, keepdims=True) * (1.0 / NA_HEAD_DIM)
        yn = oh * lax.rsqrt(ms + EPS)
        y = yn[0:GRID_W]
        for hh in range(1, hpb):
            y = y + yn[hh * GRID_W:(hh + 1) * GRID_W]
        o_ref[0, pl.ds(pl.multiple_of(rl * GRID_W, GRID_W), GRID_W), :] = (y * g).astype(BF16)

    def body(it, carry):
        for u in range(NA_UNROLL):
            one_row(it * NA_UNROLL + u)
        return carry

    lax.fori_loop(0, band // NA_UNROLL, body, 0)


def _na_attention(q, k, v, bias, g_na, batch, seq):
    rows = seq // GRID_W
    kh = min(NA_WIN_H, rows)
    hpb = NA_HEADS_PER_BLOCK
    width = hpb * NA_HEAD_DIM
    band = min(NA_BAND_ROWS, rows)
    assert rows % band == 0 and band % NA_UNROLL == 0
    q3, k3, v3 = (x.reshape(batch, seq, D_NA) for x in (q, k, v))
    seq_blk = pl.BlockSpec((1, seq, width), lambda b, j, r: (b, 0, j))
    band_blk = pl.BlockSpec((1, band * GRID_W, width), lambda b, j, r: (b, r, j))
    out = pl.pallas_call(
        functools.partial(_na_kernel, rows=rows, kh=kh, band=band),
        grid=(batch, D_NA // width, rows // band),
        in_specs=[band_blk, seq_blk, seq_blk,
                  pl.BlockSpec((hpb, kh, GRID_W, kh * GRID_W), lambda b, j, r: (j, 0, 0, 0)),
                  pl.BlockSpec((1, width), lambda b, j, r: (0, j))],
        out_specs=band_blk,
        out_shape=jax.ShapeDtypeStruct((batch, seq, D_NA), BF16),
        compiler_params=_params(3),
        name="na_attn",
    )(q3, k3, v3, bias, g_na.reshape(1, D_NA))
    return out.reshape(batch * seq, D_NA)


ML_UNROLL = 8
ML_HEADS_PER_STEP = 1


def _mlstm_kernel(q_ref, kt_ref, v_ref, gr_ref, gc_ref, o_ref, st_ref, *, n_chunks):
    L = ML_CHUNK
    hps = ML_HEADS_PER_STEP
    lane = lax.broadcasted_iota(jnp.int32, (L, LANES), 1)
    grow_i = lax.broadcasted_iota(jnp.int32, (2 * ML_HEADS, L), 0)
    row_i = lax.broadcasted_iota(jnp.int32, (L, L), 0)
    col_i = lax.broadcasted_iota(jnp.int32, (L, L), 1)
    ones = jnp.ones((L, ML_HEAD_DIM), BF16)

    def step(c, m_prev, hb, reverse):
        off = pl.multiple_of(c * L, L)
        d = 1 if reverse else 0
        hd = pl.program_id(1) * hps + hb
        cols = slice(hb * ML_HEAD_DIM, (hb + 1) * ML_HEAD_DIM)
        qc = q_ref[0, pl.ds(off, L), cols]
        kt = kt_ref[cols, pl.ds(off, L)]
        vc = v_ref[0, pl.ds(off, L), cols]
        gcol = gc_ref[pl.ds(off, L), :]
        grow = gr_ref[:, pl.ds(off, L)]

        def col(qn):
            sel = lane == d * GATE_LANES_PER_DIR + qn * ML_HEADS + hd
            return jnp.sum(jnp.where(sel, gcol, 0.0), axis=1, keepdims=True)

        r_col, cmax_col, b_col = col(0), col(1), col(2)
        r_row = jnp.sum(jnp.where(grow_i == d * ML_HEADS + hd, grow, 0.0), axis=0, keepdims=True)
        end = 0 if reverse else L - 1
        mm_end = jnp.maximum(m_prev, cmax_col[end:end + 1, :])
        m_new = b_col[end:end + 1, :] + mm_end
        mm_col = jnp.maximum(m_prev, cmax_col)
        causal = (col_i >= row_i) if reverse else (col_i <= row_i)
        dmat = jnp.exp(jnp.where(causal, r_row - mm_col, -jnp.inf))
        s = jnp.dot(qc, kt, preferred_element_type=F32) * dmat
        w_inter = jnp.exp(m_prev - mm_col)
        floor = jnp.exp(-(b_col + mm_col))
        w_col = jnp.exp(r_col - mm_end)
        decay = jnp.exp(m_prev - mm_end)
        vext = jnp.concatenate([vc, ones], axis=1)
        state = st_ref[d, hb]
        nd = (jnp.dot(s.astype(BF16), vext, preferred_element_type=F32)
              + w_inter * jnp.dot(qc, state.astype(BF16), preferred_element_type=F32))
        h = nd[:, :ML_HEAD_DIM] / jnp.maximum(jnp.abs(nd[:, ML_HEAD_DIM:]), floor)
        wv = jnp.concatenate([(w_col * vc.astype(F32)).astype(BF16),
                              jnp.broadcast_to(w_col, (L, ML_HEAD_DIM)).astype(BF16)], axis=1)
        st_ref[d, hb] = decay * state + jnp.dot(kt, wv, preferred_element_type=F32)
        return h, m_new

    st_ref[...] = jnp.zeros_like(st_ref)

    def sweep(accumulate):
        def body(it, carry):
            m = list(carry)
            for u in range(ML_UNROLL):
                j = it * ML_UNROLL + u
                jb = n_chunks - 1 - j
                rows_f = pl.ds(pl.multiple_of(j * L, L), L)
                rows_b = pl.ds(pl.multiple_of(jb * L, L), L)
                for hb in range(hps):
                    cols = slice(hb * ML_HEAD_DIM, (hb + 1) * ML_HEAD_DIM)
                    h_f, m[hb] = step(j, m[hb], hb, False)
                    h_b, m[hps + hb] = step(jb, m[hps + hb], hb, True)
                    if accumulate:
                        o_ref[0, rows_f, cols] += h_f
                        o_ref[0, rows_b, cols] += h_b
                    else:
                        o_ref[0, rows_f, cols] = h_f
                        o_ref[0, rows_b, cols] = h_b
            return tuple(m)
        return body

    half = n_chunks // (2 * ML_UNROLL)
    zero = jnp.zeros((1, 1), F32)
    carry = lax.fori_loop(0, half, sweep(False), (zero,) * (2 * hps))
    lax.fori_loop(half, 2 * half, sweep(True), carry)


def _mlstm(q, kt, v, g_row, g_col, batch, seq):
    n_chunks = seq // ML_CHUNK
    assert n_chunks % (2 * ML_UNROLL) == 0
    hps = ML_HEADS_PER_STEP
    width = hps * ML_HEAD_DIM
    q3, v3 = (x.reshape(batch, seq, D_ML) for x in (q, v))
    seq_blk = pl.BlockSpec((1, seq, width), lambda b, j: (b, 0, j))
    once_per_batch = {}
    out = pl.pallas_call(
        functools.partial(_mlstm_kernel, n_chunks=n_chunks),
        grid=(batch, ML_HEADS // hps),
        in_specs=[seq_blk, pl.BlockSpec((width, seq), lambda b, j: (j, b)), seq_blk,
                  pl.BlockSpec((2 * ML_HEADS, seq), lambda b, j: (0, b), **once_per_batch),
                  pl.BlockSpec((seq, LANES), lambda b, j: (b, 0), **once_per_batch)],
        out_specs=seq_blk,
        out_shape=jax.ShapeDtypeStruct((batch, seq, D_ML), F32),
        scratch_shapes=[pltpu.VMEM((2, hps, ML_HEAD_DIM, 2 * ML_HEAD_DIM), F32)],
        compiler_params=_params(2),
        name="mlstm",
    )(q3, kt, v3, g_row, g_col)
    return out.reshape(batch * seq, D_ML)


def _mix_out_kernel(h_ref, yna_ref, hml_ref, oml_ref, gml_ref, wo_ref, gmoe_ref, wrh_ref, wrl_ref, br_ref, tri_ref,
                    h1_ref, t_ref, cnt_ref):
    i = pl.program_id(0)
    tm = h_ref.shape[0]

    @pl.when(i == 0)
    def _():
        cnt_ref[...] = jnp.zeros_like(cnt_ref)

    hml = hml_ref[...]
    parts = [_rms_scale(hml[:, d * ML_HEAD_DIM:(d + 1) * ML_HEAD_DIM]) for d in range(ML_HEADS)]
    yml = _sigmoid(oml_ref[...].astype(F32)) * (jnp.concatenate(parts, axis=1) * gml_ref[...])
    mix = (jnp.dot(yna_ref[...], wo_ref[0:D_NA, :], preferred_element_type=F32)
           + jnp.dot(yml.astype(BF16), wo_ref[D_NA:D_NA + D_ML, :], preferred_element_type=F32))
    h1 = h_ref[...] + mix
    h1_ref[...] = h1
    t = _rms_scale(h1) * gmoe_ref[...]
    bits = lax.bitcast_convert_type(t.astype(BF16).astype(F32), U32)
    t_ref[:, 0:PACKED] = bits[:, PACKED:] | lax.shift_right_logical(bits[:, :PACKED], U32(16))

    t_hi = t.astype(BF16)
    t_lo = (t - t_hi.astype(F32)).astype(BF16)
    nt = (((1,), (1,)), ((), ()))
    logits = (lax.dot_general(wrh_ref[...], t_hi, nt, preferred_element_type=F32)
              + lax.dot_general(wrh_ref[...], t_lo, nt, preferred_element_type=F32)
              + lax.dot_general(wrl_ref[...], t_hi, nt, preferred_element_type=F32)) + br_ref[...]
    epg = EXPERTS_PER_GROUP
    row8 = lax.broadcasted_iota(jnp.int32, (epg, tm), 0)

    def top(x):
        mx = jnp.max(x, axis=0, keepdims=True)
        idx = jnp.min(jnp.where(x == mx, row8, epg), axis=0, keepdims=True)
        return mx, idx

    lg = jnp.where(row8 < N_GROUPS, logits[0:epg], -jnp.inf)
    g_max, g_idx = top(lg)
    pg_top = 1.0 / jnp.sum(jnp.exp(lg - g_max), axis=0, keepdims=True)
    le = jnp.zeros((epg, tm), F32)
    for grp in range(N_GROUPS):
        le = jnp.where(g_idx == grp, logits[EXPERT_LANE0 + grp * epg:EXPERT_LANE0 + (grp + 1) * epg], le)
    e1_max, i1 = top(le)
    e_sum = jnp.sum(jnp.exp(le - e1_max), axis=0, keepdims=True)
    e2_max, i2 = top(jnp.where(row8 == i1, -jnp.inf, le))
    p1 = 1.0 / e_sum
    p2 = jnp.exp(e2_max - e1_max) / e_sum
    w1 = pg_top * p1 / (p1 + p2)
    w2 = pg_top * p2 / (p1 + p2)

    la = jnp.minimum(i1, i2)
    lb = jnp.maximum(i1, i2)
    cls = g_idx * PAIRS_PER_GROUP + lax.shift_right_logical(la * (2 * epg - 1 - la), 1) + (lb - la - 1)
    w_lo = jnp.where(i1 < i2, w1, w2)
    w_hi = jnp.where(i1 < i2, w2, w1)
    oh = (lax.broadcasted_iota(jnp.int32, (LANES, tm), 0) == cls).astype(F32)
    before = jnp.dot(oh.astype(BF16), tri_ref[...], preferred_element_type=F32) + cnt_ref[:, 0:1]
    rank = jnp.sum(oh * before, axis=0, keepdims=True)
    cnt_ref[...] = cnt_ref[...] + jnp.sum(oh, axis=1, keepdims=True)
    info = jnp.concatenate([cls.astype(F32), rank, w_lo, w_hi, jnp.zeros((LANES - 4, tm), F32)], axis=0)
    for c in range(tm // LANES):
        t_ref[c * LANES:(c + 1) * LANES, PACKED:PACKED + LANES] = lax.bitcast_convert_type(
            info[:, c * LANES:(c + 1) * LANES].T, U32)


def _mix_out(h, yna, hml, oml, g_ml, w_out, g_moe, w_rg, b_rg, w_re, b_re):
    t = h.shape[0]
    tm = ROW_TILE
    wr = jnp.zeros((LANES, D_MODEL), F32)
    wr = wr.at[:N_GROUPS].set(w_rg.T).at[EXPERT_LANE0:EXPERT_LANE0 + N_EXPERTS].set(w_re.T)
    wrh = wr.astype(BF16)
    wrl = (wr - wrh.astype(F32)).astype(BF16)
    br = jnp.zeros((LANES, 1), F32)
    br = br.at[:N_GROUPS, 0].set(b_rg).at[EXPERT_LANE0:EXPERT_LANE0 + N_EXPERTS, 0].set(b_re)
    tri = jnp.asarray(np.triu(np.ones((tm, tm), np.float32), 1), BF16)
    full = lambda shape: pl.BlockSpec(shape, lambda i: (0,) * len(shape))
    row_blk = lambda w: pl.BlockSpec((tm, w), lambda i: (i, 0))
    return pl.pallas_call(
        _mix_out_kernel,
        grid=(t // tm,),
        in_specs=[row_blk(D_MODEL), row_blk(D_NA), row_blk(D_ML), row_blk(D_ML), full((1, D_ML)),
                  full((D_MODEL, D_MODEL)), full((1, D_MODEL)), full((LANES, D_MODEL)), full((LANES, D_MODEL)),
                  full((LANES, 1)), full((tm, tm))],
        out_specs=[row_blk(D_MODEL), row_blk(ROUTED_WIDTH), full((LANES, LANES))],
        out_shape=[jax.ShapeDtypeStruct((t, D_MODEL), F32), jax.ShapeDtypeStruct((t, ROUTED_WIDTH), U32),
                   jax.ShapeDtypeStruct((LANES, LANES), F32)],
        compiler_params=_params(1),
        name="mix_out",
    )(h, yna, hml, oml, g_ml.reshape(1, D_ML), w_out.astype(BF16), g_moe.reshape(1, D_MODEL), wrh, wrl, br, tri)


def _dispatch_kernel(zflag_ref, dest_ref, t_ref, xs_ref, zbuf, ring, sem, zsem, *, n_blocks, n_tiles):
    tm = t_ref.shape[0]

    @pl.when(pl.program_id(0) == 0)
    def _():
        zbuf[...] = jnp.zeros_like(zbuf)

        def zero_copy(b):
            return pltpu.make_async_copy(zbuf, xs_ref.at[pl.ds(pl.multiple_of(b * MOE_BLOCK, MOE_BLOCK), MOE_BLOCK)], zsem)

        def zissue(b, carry):
            @pl.when(zflag_ref[b] != 0)
            def _():
                zero_copy(b).start()
            return carry

        def zwait(b, carry):
            @pl.when(zflag_ref[b] != 0)
            def _():
                zero_copy(b).wait()
            return carry

        lax.fori_loop(0, n_blocks, zissue, 0)
        lax.fori_loop(0, n_blocks, zwait, 0)

    i = pl.program_id(0)
    slot = i % 2

    def wait_slot(s):
        pltpu.make_async_copy(ring.at[s], xs_ref.at[pl.ds(0, tm)], sem.at[s]).wait()

    @pl.when(i >= 2)
    def _():
        wait_slot(slot)

    ring[slot] = t_ref[...]

    def issue(it, carry):
        for u in range(ISSUE_UNROLL):
            j = it * ISSUE_UNROLL + u
            pltpu.make_async_copy(ring.at[slot, pl.ds(j, 1)], xs_ref.at[pl.ds(dest_ref[0, 0, j], 1)],
                                  sem.at[slot]).start(priority=u % 2)
        return carry

    lax.fori_loop(0, tm // ISSUE_UNROLL, issue, 0)

    @pl.when(i == n_tiles - 1)
    def _():
        wait_slot(slot)

        @pl.when(i >= 1)
        def _():
            wait_slot(1 - slot)


def _dispatch(t, dest, zflag, n_blocks):
    tt = t.shape[0]
    tm = ROW_TILE
    n_tiles = tt // tm
    return pl.pallas_call(
        functools.partial(_dispatch_kernel, n_blocks=n_blocks, n_tiles=n_tiles),
        grid_spec=pltpu.PrefetchScalarGridSpec(
            num_scalar_prefetch=1,
            grid=(n_tiles,),
            in_specs=[pl.BlockSpec((1, 1, tm), lambda i, zf: (i, 0, 0), memory_space=pltpu.SMEM),
                      pl.BlockSpec((tm, ROUTED_WIDTH), lambda i, zf: (i, 0))],
            out_specs=pl.BlockSpec(memory_space=pl.ANY),
            scratch_shapes=[pltpu.VMEM((MOE_BLOCK, ROUTED_WIDTH), U32), pltpu.VMEM((2, tm, ROUTED_WIDTH), U32),
                            pltpu.SemaphoreType.DMA((2,)), pltpu.SemaphoreType.DMA],
        ),
        out_shape=jax.ShapeDtypeStruct((n_blocks * MOE_BLOCK, ROUTED_WIDTH), U32),
        compiler_params=_params(1),
        name="dispatch",
    )(zflag, dest.reshape(n_tiles, 1, tm), t)


def _ffn_kernel(ea_ref, eb_ref, nu_ref, x_ref, wg_hbm, wu_hbm, wd_hbm, y_ref,
                wg_s, wu_s, wd_s, stage_g, stage_u, stage_d, sem, *, layer):
    i = pl.program_id(0)
    used = i < nu_ref[0]
    group = lax.shift_right_logical(ea_ref[i], 3)
    prev_group = lax.shift_right_logical(ea_ref[jnp.maximum(i - 1, 0)], 3)
    mats = ((wg_hbm, stage_g, wg_s), (wu_hbm, stage_u, wu_s), (wd_hbm, stage_d, wd_s))

    @pl.when(used & ((i == 0) | (group != prev_group)))
    def _():
        def fetch(m, e):
            w_hbm, stage, _ = mats[m]
            return pltpu.make_async_copy(w_hbm.at[layer, group * EXPERTS_PER_GROUP + e], stage, sem.at[m])

        for m in range(3):
            fetch(m, 0).start()

        def load_expert(e, carry):
            for m in range(3):
                _, stage, w_s = mats[m]
                fetch(m, e).wait()
                w_s[e] = stage[...].astype(BF16)

                @pl.when(e + 1 < EXPERTS_PER_GROUP)
                def _():
                    fetch(m, e + 1).start()
            return carry

        lax.fori_loop(0, EXPERTS_PER_GROUP, load_expert, 0)

    def swiglu(x, e):
        g = jnp.dot(x, wg_s[e], preferred_element_type=F32)
        u = jnp.dot(x, wu_s[e], preferred_element_type=F32)
        hid = (g * _sigmoid(g) * u).astype(BF16)
        return jnp.dot(hid, wd_s[e], preferred_element_type=F32)

    @pl.when(used)
    def _():
        words = x_ref[:, 0:PACKED]
        as_f32 = lambda u: lax.bitcast_convert_type(u, F32)
        x = jnp.concatenate([as_f32(lax.shift_left(words, U32(16))).astype(BF16),
                             as_f32(words & U32(0xFFFF0000)).astype(BF16)], axis=1)
        y_ref[...] = (as_f32(x_ref[:, INFO_W_LO:INFO_W_LO + 1]) * swiglu(x, ea_ref[i] & (EXPERTS_PER_GROUP - 1))
                      + as_f32(x_ref[:, INFO_W_HI:INFO_W_HI + 1]) * swiglu(x, eb_ref[i] & (EXPERTS_PER_GROUP - 1)))

    @pl.when(jnp.logical_not(used))
    def _():
        y_ref[...] = jnp.zeros_like(y_ref)


def _expert_ffn(xs, block_ea, block_eb, n_used, w_gate, w_up, w_down, layer):
    n_rows = xs.shape[0]
    nb = n_rows // MOE_BLOCK
    blk = lambda i, ea, eb, nu: (jnp.minimum(i, nu[0] - 1), 0)
    hbm = pl.BlockSpec(memory_space=pl.ANY)
    return pl.pallas_call(
        functools.partial(_ffn_kernel, layer=layer),
        grid_spec=pltpu.PrefetchScalarGridSpec(
            num_scalar_prefetch=3,
            grid=(nb,),
            in_specs=[pl.BlockSpec((MOE_BLOCK, ROUTED_WIDTH), blk), hbm, hbm, hbm],
            out_specs=pl.BlockSpec((MOE_BLOCK, D_MODEL), lambda i, ea, eb, nu: (i, 0)),
            scratch_shapes=[pltpu.VMEM((EXPERTS_PER_GROUP, D_MODEL, D_EXPERT), BF16),
                            pltpu.VMEM((EXPERTS_PER_GROUP, D_MODEL, D_EXPERT), BF16),
                            pltpu.VMEM((EXPERTS_PER_GROUP, D_EXPERT, D_MODEL), BF16),
                            pltpu.VMEM((D_MODEL, D_EXPERT), F32), pltpu.VMEM((D_MODEL, D_EXPERT), F32),
                            pltpu.VMEM((D_EXPERT, D_MODEL), F32), pltpu.SemaphoreType.DMA((3,))],
        ),
        out_shape=jax.ShapeDtypeStruct((n_rows, D_MODEL), F32),
        compiler_params=_params(1),
        name="expert_ffn",
    )(block_ea, block_eb, n_used, xs, w_gate, w_up, w_down)


def _combine_kernel(dcur_ref, dnxt_ref, h1_ref, p_ref, gple_ref, wple_ref, wpg_ref, gfin_ref, ys_ref,
                    o_ref, buf, sem, *, final, n_tiles):
    i = pl.program_id(0)
    tm = h1_ref.shape[0]
    slot = i % 2

    def issue_tile(d_ref, s):
        def issue(it, carry):
            for u in range(ISSUE_UNROLL):
                j = it * ISSUE_UNROLL + u
                pltpu.make_async_copy(ys_ref.at[pl.ds(d_ref[0, 0, j], 1)], buf.at[s, pl.ds(j, 1)], sem.at[s]).start()
            return carry

        lax.fori_loop(0, tm // ISSUE_UNROLL, issue, 0)

    @pl.when(i == 0)
    def _():
        issue_tile(dcur_ref, 0)

    @pl.when(i + 1 < n_tiles)
    def _():
        issue_tile(dnxt_ref, 1 - slot)

    pltpu.make_async_copy(ys_ref.at[pl.ds(0, tm)], buf.at[slot], sem.at[slot]).wait()

    h2 = h1_ref[...] + buf[slot]
    a = (_rms_scale(h2) * gple_ref[...]).astype(BF16)
    gate = _sigmoid(jnp.dot(a, wpg_ref[...], preferred_element_type=F32))
    emb = jnp.dot(p_ref[0].astype(BF16), wple_ref[...], preferred_element_type=F32)
    h3 = h2 + emb * gate
    if final:
        h3 = _rms_scale(h3) * gfin_ref[...]
    o_ref[...] = h3


def _combine(h1, ys, dest, p_all, layer, g_ple, w_ple, w_pg, g_final, final):
    tt = h1.shape[0]
    tm = ROW_TILE
    n_tiles = tt // tm
    dest3 = dest.reshape(n_tiles, 1, tm)
    full = lambda shape: pl.BlockSpec(shape, lambda i: (0,) * len(shape))
    row_blk = lambda w: pl.BlockSpec((tm, w), lambda i: (i, 0))
    return pl.pallas_call(
        functools.partial(_combine_kernel, final=final, n_tiles=n_tiles),
        grid=(n_tiles,),
        in_specs=[pl.BlockSpec((1, 1, tm), lambda i: (i, 0, 0), memory_space=pltpu.SMEM),
                  pl.BlockSpec((1, 1, tm), lambda i: (jnp.minimum(i + 1, n_tiles - 1), 0, 0), memory_space=pltpu.SMEM),
                  row_blk(D_MODEL), pl.BlockSpec((1, tm, D_PLE), lambda i: (layer, i, 0)), full((1, D_MODEL)),
                  full((D_PLE, D_MODEL)), full((D_MODEL, D_MODEL)), full((1, D_MODEL)),
                  pl.BlockSpec(memory_space=pl.ANY)],
        out_specs=row_blk(D_MODEL),
        out_shape=jax.ShapeDtypeStruct((tt, D_MODEL), F32),
        scratch_shapes=[pltpu.VMEM((2, tm, D_MODEL), F32), pltpu.SemaphoreType.DMA((2,))],
        compiler_params=_params(1),
        name="combine",
    )(dest3, dest3, h1, p_all, g_ple.reshape(1, D_MODEL), w_ple.astype(BF16), w_pg.astype(BF16),
      g_final.reshape(1, D_MODEL), ys)


def _class_experts():
    lo, hi = [], []
    for g in range(N_GROUPS):
        for a in range(EXPERTS_PER_GROUP):
            for b in range(a + 1, EXPERTS_PER_GROUP):
                lo.append(g * EXPERTS_PER_GROUP + a)
                hi.append(g * EXPERTS_PER_GROUP + b)
    return np.asarray(lo, np.int32), np.asarray(hi, np.int32)


def _routing_tables(routed, counts_f, n_blocks):
    info = lax.bitcast_convert_type(routed[:, INFO_CLASS:INFO_RANK + 1], F32)
    cls = info[:, 0].astype(jnp.int32)
    rank = info[:, 1].astype(jnp.int32)
    counts = counts_f[:N_CLASSES, 0].astype(jnp.int32)
    padded = ((counts + MOE_BLOCK - 1) // MOE_BLOCK) * MOE_BLOCK
    pad_end = jnp.cumsum(padded)
    pad_start = pad_end - padded
    ids = jnp.arange(N_CLASSES, dtype=jnp.int32)
    dest = jnp.sum(jnp.where(cls[:, None] == ids, pad_start, 0), axis=-1) + rank
    n_used = (pad_end[-1] // MOE_BLOCK).astype(jnp.int32).reshape(1)
    starts = jnp.arange(n_blocks, dtype=jnp.int32) * MOE_BLOCK
    block_cls = jnp.minimum(jnp.sum((pad_end[None, :] <= starts[:, None]).astype(jnp.int32), axis=1), N_CLASSES - 1)
    pick = lambda table: jnp.sum(jnp.where(block_cls[:, None] == ids, table, 0), axis=-1)
    cls_lo, cls_hi = _class_experts()
    filled = jnp.clip(pick(counts) - (starts - pick(pad_start)), 0, MOE_BLOCK)
    zflag = (filled < MOE_BLOCK).astype(jnp.int32)
    return dest, pick(jnp.asarray(cls_lo)), pick(jnp.asarray(cls_hi)), n_used, zflag


def kernel(x, p, w_in, b_gate, conv_w, conv_b, rpb, g_na, g_ml, w_out, g_mix, g_moe, w_route_group, b_route_group, w_route_expert, b_route_expert, w_exp_gate, w_exp_up, w_exp_down, g_ple, w_ple, w_ple_gate, g_final):
    batch, seq, _ = x.shape
    depth = w_in.shape[0]
    tt = batch * seq
    assert seq % ROW_TILE == 0 and seq % GRID_W == 0
    n_blocks = tt // MOE_BLOCK + N_CLASSES
    h = x.reshape(tt, D_MODEL)
    for i in range(depth):
        qna, kna, vna, qml, kt_ml, vml, oml, g_col, g_row = _in_proj(h, seq, g_mix[i], w_in[i], b_gate[i], conv_w[i], conv_b[i])
        yna = _na_attention(qna, kna, vna, _na_bias_table(rpb[i], seq // GRID_W), g_na[i], batch, seq)
        hml = _mlstm(qml, kt_ml, vml, g_row, g_col, batch, seq)
        h1, routed, counts = _mix_out(h, yna, hml, oml, g_ml[i], w_out[i], g_moe[i],
                                      w_route_group[i], b_route_group[i], w_route_expert[i], b_route_expert[i])
        dest, block_ea, block_eb, n_used, zflag = _routing_tables(routed, counts, n_blocks)
        xs = _dispatch(routed, dest, zflag, n_blocks)
        ys = _expert_ffn(xs, block_ea, block_eb, n_used, w_exp_gate, w_exp_up, w_exp_down, i)
        h = _combine(h1, ys, dest, p.reshape(depth, tt, D_PLE), i, g_ple[i], w_ple[i], w_ple_gate[i], g_final,
                     final=(i == depth - 1))
    return h.reshape(batch, seq, D_MODEL)
```
